```python
import math
import numpy as np
import jax
import jax.numpy as jnp
from jax import lax

D_MODEL = 1024
BATCH = 8
SEQ = 4096
DEPTH = 4

GRID_W = 64
CTX_LEN = 256

NA_HEAD_DIM = 64
NA_WIDTH = D_MODEL // 4
NA_HEADS = NA_WIDTH // NA_HEAD_DIM
NA_WIN_ROWS = 8
NA_WIN_COLS = 16

DN_HEAD_DIM = 128
DN_WIDTH = D_MODEL // 2
DN_HEADS = DN_WIDTH // DN_HEAD_DIM
DN_CONV_W = 5
DN_CHUNK = 64

HY_WIDTH = D_MODEL // 4
HY_ORDER = 2
HY_CONV_W = 3
HY_EMB = 33
HY_HIDDEN = 64
HY_DECAY_TARGET = 1e-2
HY_FAST_DECAY = 0.3
HY_SLOW_DECAY = 1.5

MIX_WIDTH = NA_WIDTH + DN_WIDTH + HY_WIDTH
IN_SIZES = (3 * NA_WIDTH, NA_WIDTH, 3 * DN_WIDTH, DN_WIDTH, 2 * DN_HEADS, 2 * DN_HEADS, 3 * HY_WIDTH, HY_WIDTH)
IN_WIDTH = sum(IN_SIZES)

ROPE_BASE = 10000.0
NORM_EPS = 1e-6
ADA_SCALE = 0.5

kernel_name = 'hybrid_na_deltanet_hyena_block'


def rms_norm(x, g):
    xf = x.astype(jnp.float32)
    y = xf * lax.rsqrt(jnp.mean(jnp.square(xf), axis=-1, keepdims=True) + NORM_EPS)
    return (y * g.astype(jnp.float32)).astype(x.dtype)


def l2_normalize(x):
    xf = x.astype(jnp.float32)
    return xf * lax.rsqrt(jnp.sum(jnp.square(xf), axis=-1, keepdims=True) + NORM_EPS)


def split_cols(h):
    return jnp.split(h, np.cumsum(IN_SIZES)[:-1].tolist(), axis=-1)


def dwconv_centred(x, w):
    k, ch = w.shape
    return lax.conv_general_dilated(x, w.astype(x.dtype)[:, None, :], (1,), [(k // 2, k // 2)],
                                    dimension_numbers=('NWC', 'WIO', 'NWC'), feature_group_count=ch)


def modulation(cond, w_ada, b_ada):
    m = jnp.matmul(jax.nn.silu(cond), w_ada) + b_ada
    return jnp.split(m, 3, axis=-1)


def heads(t, n_heads, head_dim):
    return t.reshape(t.shape[0], t.shape[1], n_heads, head_dim)


def axial_rope(x):
    l, dh = x.shape[1], x.shape[-1]
    half = dh // 2
    nf = half // 2
    t = jnp.arange(l)
    inv = ROPE_BASE ** (-jnp.arange(nf, dtype=jnp.float32) / nf)
    xf = x.astype(jnp.float32)

    def rot(xp, pos):
        ang = pos.astype(jnp.float32)[:, None] * inv[None, :]
        cos = jnp.cos(ang)[None, :, None, :]
        sin = jnp.sin(ang)[None, :, None, :]
        a, b = xp[..., :nf], xp[..., nf:]
        return jnp.concatenate([a * cos - b * sin, b * cos + a * sin], axis=-1)

    out = jnp.concatenate([rot(xf[..., :half], t // GRID_W), rot(xf[..., half:], t % GRID_W)], axis=-1)
    return out.astype(x.dtype)


def na_latent(q, k, v, k_ctx, v_ctx, rpb):
    b, l, h, dh = q.shape
    rows = l // GRID_W
    kr = min(NA_WIN_ROWS, rows)
    kc = NA_WIN_COLS
    r = jnp.arange(rows)
    col = jnp.arange(GRID_W)
    band = jnp.clip(r - kr // 2, 0, rows - kr)[:, None] + jnp.arange(kr)[None, :]
    c0 = jnp.clip(col - kc // 2, 0, GRID_W - kc)
    in_win = (col[None, :] >= c0[:, None]) & (col[None, :] < c0[:, None] + kc)
    qg = q.reshape(b, rows, GRID_W, h, dh)
    kg = k.reshape(b, rows, GRID_W, h, dh)[:, band]
    vg = v.reshape(b, rows, GRID_W, h, dh)[:, band]
    scale = dh ** -0.5
    s_loc = jnp.einsum('brqhd,brkwhd->brhqkw', qg, kg).astype(jnp.float32) * scale
    dr = band - r[:, None] + (NA_WIN_ROWS - 1)
    dc = jnp.clip(col[None, :] - col[:, None], -(kc - 1), kc - 1) + (kc - 1)
    bias = rpb.astype(jnp.float32)[:, dr][:, :, :, dc]
    bias = jnp.transpose(bias, (1, 0, 3, 2, 4))
    s_loc = jnp.where(in_win[:, None, :], s_loc + bias, -jnp.inf)
    s_ctx = jnp.einsum('brqhd,bchd->brhqc', qg, k_ctx).astype(jnp.float32) * scale
    n_loc = kr * GRID_W
    s = jnp.concatenate([s_loc.reshape(b, rows, h, GRID_W, n_loc), s_ctx], axis=-1)
    p = jax.nn.softmax(s, axis=-1).astype(v.dtype)
    p_loc = p[..., :n_loc].reshape(b, rows, h, GRID_W, kr, GRID_W)
    o = (jnp.einsum('brhqkw,brkwhd->brqhd', p_loc, vg)
         + jnp.einsum('brhqc,bchd->brqhd', p[..., n_loc:], v_ctx))
    return o.reshape(b, l, h * dh)


def dense_ctx_attention(q, k, v):
    b, l, h, dh = q.shape
    s = jnp.einsum('bqhd,bkhd->bhqk', q, k).astype(jnp.float32) * dh ** -0.5
    p = jax.nn.softmax(s, axis=-1).astype(v.dtype)
    return jnp.einsum('bhqk,bkhd->bqhd', p, v).reshape(b, l, h * dh)


def dn_inputs(qkv_raw, b_raw, a_raw, conv_w, a_log, dt_bias, rope):
    bsz, l, _ = qkv_raw.shape
    qkv = jax.nn.silu(dwconv_centred(qkv_raw, conv_w))
    q, k, v = [heads(t, DN_HEADS, DN_HEAD_DIM) for t in jnp.split(qkv, 3, axis=-1)]
    if rope:
        q, k = axial_rope(q), axial_rope(k)
    q = l2_normalize(q) * (DN_HEAD_DIM ** -0.5)
    k = l2_normalize(k)
    v = v.astype(jnp.float32)
    beta = jax.nn.sigmoid(b_raw.astype(jnp.float32)).reshape(bsz, l, 2, DN_HEADS)
    g = -jnp.exp(a_log.astype(jnp.float32)) * jax.nn.softplus(
        a_raw.astype(jnp.float32).reshape(bsz, l, 2, DN_HEADS) + dt_bias.astype(jnp.float32))
    beta = jnp.transpose(beta, (2, 0, 3, 1))
    g = jnp.transpose(g, (2, 0, 3, 1))
    to_bhl = lambda t: jnp.transpose(t, (0, 2, 1, 3))
    return to_bhl(q), to_bhl(k), to_bhl(v), beta, g


def gated_delta_chunked(q, k, v, beta, g, s0):
    b, h, l, _ = q.shape
    dv = v.shape[-1]
    n = l // DN_CHUNK
    q, k, v = [t.reshape(b, h, n, DN_CHUNK, t.shape[-1]) for t in (q, k, v)]
    beta = beta.reshape(b, h, n, DN_CHUNK)
    gc = jnp.cumsum(g.reshape(b, h, n, DN_CHUNK), axis=-1)
    pos = jnp.arange(DN_CHUNK)
    tri = pos[:, None] >= pos[None, :]
    strict = pos[:, None] > pos[None, :]
    diff = gc[..., :, None] - gc[..., None, :]
    decay = jnp.where(tri, jnp.exp(jnp.where(tri, diff, 0.0)), 0.0)
    kb = k * beta[..., None]
    a_kk = jnp.where(strict, jnp.einsum('bhnid,bhnjd->bhnij', kb, k) * decay, 0.0)
    rhs = jnp.concatenate([v * beta[..., None], kb * jnp.exp(gc)[..., None]], axis=-1)
    sol = lax.linalg.triangular_solve(a_kk + jnp.eye(DN_CHUNK, dtype=jnp.float32), rhs,
                                      left_side=True, lower=True, unit_diagonal=True)
    u, w = sol[..., :dv], sol[..., dv:]
    a_qk = jnp.where(tri, jnp.einsum('bhnid,bhnjd->bhnij', q, k) * decay, 0.0)

    def step(s, xs):
        q_i, k_i, u_i, w_i, g_i, aqk_i = xs
        v_new = u_i - jnp.einsum('bhcd,bhde->bhce', w_i, s)
        o_i = (jnp.einsum('bhcd,bhde->bhce', q_i * jnp.exp(g_i)[..., None], s)
               + jnp.einsum('bhij,bhje->bhie', aqk_i, v_new))
        g_last = g_i[..., -1]
        s = (s * jnp.exp(g_last)[..., None, None]
             + jnp.einsum('bhcd,bhce->bhde', k_i * jnp.exp(g_last[..., None] - g_i)[..., None], v_new))
        return s, o_i

    xs = tuple(jnp.moveaxis(t, 2, 0) for t in (q, k, u, w, gc, a_qk))
    s_fin, o = lax.scan(step, s0, xs)
    return jnp.moveaxis(o, 0, 2).reshape(b, h, l, dv), s_fin


def dn_bidirectional(q, k, v, beta, g, s0):
    flip = lambda t: jnp.flip(t, axis=2)
    o_f, s_f = gated_delta_chunked(q, k, v, beta[0], g[0], s0[0])
    o_b, s_b = gated_delta_chunked(flip(q), flip(k), flip(v), flip(beta[1]), flip(g[1]), s0[1])
    return o_f + flip(o_b), jnp.stack([s_f, s_b])


def dn_output(o, z, norm_w):
    b, h, l, dv = o.shape
    o = jnp.transpose(o, (0, 2, 1, 3))
    y = rms_norm(o, norm_w) * jax.nn.silu(z.reshape(b, l, h, dv).astype(jnp.float32))
    return y.reshape(b, l, h * dv).astype(z.dtype)


def hyena_filters(l, w1, b1, w2, b2, w3, b3, freq, wout):
    f32 = jnp.float32
    t = jnp.linspace(0.0, 1.0, l, dtype=f32)[:, None]
    bands = (HY_EMB - 1) // 2
    fr = jnp.linspace(1e-4, bands - 1, bands, dtype=f32)
    ang = (2.0 * math.pi / l) * jnp.arange(l, dtype=f32)[:, None] * fr[None, :]
    z = jnp.concatenate([t, jnp.cos(ang), -jnp.sin(ang)], axis=-1)
    freq = freq.astype(f32)
    hid = jnp.sin(freq[0] * (z @ w1.astype(f32) + b1.astype(f32)))
    hid = jnp.sin(freq[1] * (hid @ w2.astype(f32) + b2.astype(f32)))
    hid = jnp.sin(freq[2] * (hid @ w3.astype(f32) + b3.astype(f32)))
    filt = (hid @ wout.astype(f32)).reshape(l, HY_ORDER, 2, HY_WIDTH)
    max_decay = math.log(HY_DECAY_TARGET) / HY_FAST_DECAY
    min_decay = math.log(HY_DECAY_TARGET) / HY_SLOW_DECAY
    deltas = jnp.abs(jnp.linspace(min_decay, max_decay, HY_WIDTH, dtype=f32))
    window = jnp.exp(-t * deltas[None, :])
    return filt * window[:, None, None, :]


def bidir_long_conv(u, h_fwd, h_bwd):
    b, l, ch = u.shape
    n = 2 * l
    kern = jnp.concatenate([h_fwd, jnp.zeros((1, ch), jnp.float32), h_bwd[:0:-1]], axis=0)
    y = jnp.fft.irfft(jnp.fft.rfft(u.astype(jnp.float32), n=n, axis=1) * jnp.fft.rfft(kern, n=n, axis=0)[None],
                      n=n, axis=1)
    return y[:, :l]


def hyena_mix(vx_raw, z, conv_w, filt, skip):
    vx = dwconv_centred(vx_raw, conv_w).astype(jnp.float32)
    v, x1, x2 = jnp.split(vx, 3, axis=-1)
    y = v
    for n, gate in enumerate((x1, x2)):
        y = gate * (bidir_long_conv(y, filt[:, n, 0], filt[:, n, 1]) + y * skip[n].astype(jnp.float32))
    return (y * jax.nn.silu(z.astype(jnp.float32))).astype(z.dtype)


def setup_inputs(seed: int = 0) -> dict:
    key = jax.random.key(seed)
    ks = jax.random.split(key, 26)
    f32 = jnp.float32
    nrm = lambda kk, shape, s: jax.random.normal(kk, shape, f32) * s
    d = D_MODEL
    dt = jnp.exp(jax.random.uniform(ks[13], (DEPTH, 2, DN_HEADS), f32, math.log(1e-3), math.log(1e-1)))
    return {
        'x': nrm(ks[0], (BATCH, SEQ, d), 1.0),
        'c': nrm(ks[1], (BATCH, d), 1.0),
        'ctx': nrm(ks[2], (BATCH, CTX_LEN, d), 1.0),
        'c_ctx': nrm(ks[3], (d,), 1.0),
        'w_ada': nrm(ks[4], (DEPTH, d, 3 * d), ADA_SCALE * d ** -0.5),
        'b_ada': nrm(ks[5], (DEPTH, 3 * d), 0.02),
        'g_pre': 1.0 + nrm(ks[6], (DEPTH, d), 0.02),
        'g_post': 1.0 + nrm(ks[7], (DEPTH, d), 0.02),
        'w_in': nrm(ks[8], (DEPTH, d, IN_WIDTH), d ** -0.5),
        'w_out': nrm(ks[9], (DEPTH, MIX_WIDTH, d), MIX_WIDTH ** -0.5),
        'na_rpb': nrm(ks[10], (DEPTH, NA_HEADS, 2 * NA_WIN_ROWS - 1, 2 * NA_WIN_COLS - 1), 0.1),
        'dn_conv': nrm(ks[11], (DEPTH, DN_CONV_W, 3 * DN_WIDTH), DN_CONV_W ** -0.5),
        'dn_a_log': jnp.log(jax.random.uniform(ks[12], (DEPTH, 2, DN_HEADS), f32, 1.0, 16.0)),
        'dn_dt_bias': dt + jnp.log(-jnp.expm1(-dt)),
        'dn_norm': 1.0 + nrm(ks[14], (DEPTH, DN_HEAD_DIM), 0.02),
        'hy_conv': nrm(ks[15], (DEPTH, HY_CONV_W, 3 * HY_WIDTH), HY_CONV_W ** -0.5),
        'hy_w1': nrm(ks[16], (DEPTH, HY_EMB, HY_HIDDEN), HY_EMB ** -0.5),
        'hy_b1': nrm(ks[17], (DEPTH, HY_HIDDEN), 0.1),
        'hy_w2': nrm(ks[18], (DEPTH, HY_HIDDEN, HY_HIDDEN), HY_HIDDEN ** -0.5),
        'hy_b2': nrm(ks[19], (DEPTH, HY_HIDDEN), 0.1),
        'hy_w3': nrm(ks[20], (DEPTH, HY_HIDDEN, HY_HIDDEN), HY_HIDDEN ** -0.5),
        'hy_b3': nrm(ks[21], (DEPTH, HY_HIDDEN), 0.1),
        'hy_freq': 1.0 + nrm(ks[22], (DEPTH, 3, HY_HIDDEN), 0.02),
        'hy_wout': nrm(ks[23], (DEPTH, HY_HIDDEN, HY_ORDER * 2 * HY_WIDTH), 0.1 * HY_HIDDEN ** -0.5),
        'hy_skip': nrm(ks[24], (DEPTH, HY_ORDER, HY_WIDTH), 0.5),
    }


def reference(x, c, ctx, c_ctx, w_ada, b_ada, g_pre, g_post, w_in, w_out, na_rpb, dn_conv, dn_a_log, dn_dt_bias,
              dn_norm, hy_conv, hy_w1, hy_b1, hy_w2, hy_b2, hy_w3, hy_b3, hy_freq, hy_wout, hy_skip):
    bsz, l, _ = x.shape
    lc = ctx.shape[1]
    for i in range(DEPTH):
        last = i == DEPTH - 1
        shift_x, scale_x, gate_x = modulation(c, w_ada[i], b_ada[i])
        shift_c, scale_c, gate_c = modulation(c_ctx, w_ada[i], b_ada[i])
        hx = rms_norm(x, g_pre[i]) * (1.0 + scale_x[:, None]) + shift_x[:, None]
        hc = rms_norm(ctx, g_pre[i]) * (1.0 + scale_c) + shift_c
        na_qkv_x, na_z_x, dn_qkv_x, dn_z_x, dn_b_x, dn_a_x, hy_vx_x, hy_z_x = split_cols(hx @ w_in[i])
        na_qkv_c, na_z_c, dn_qkv_c, dn_z_c, dn_b_c, dn_a_c, hy_vx_c, hy_z_c = split_cols(hc @ w_in[i])

        q_x, k_x, v_x = [heads(t, NA_HEADS, NA_HEAD_DIM) for t in jnp.split(na_qkv_x, 3, axis=-1)]
        q_c, k_c, v_c = [heads(t, NA_HEADS, NA_HEAD_DIM) for t in jnp.split(na_qkv_c, 3, axis=-1)]
        out_a_x = na_latent(q_x, k_x, v_x, k_c, v_c, na_rpb[i]) * jax.nn.silu(na_z_x)

        dq_c, dk_c, dv_c, beta_c, g_c = dn_inputs(dn_qkv_c, dn_b_c, dn_a_c, dn_conv[i], dn_a_log[i], dn_dt_bias[i], False)
        s0 = jnp.zeros((2, bsz, DN_HEADS, DN_HEAD_DIM, DN_HEAD_DIM), jnp.float32)
        do_c, s_ctx = dn_bidirectional(dq_c, dk_c, dv_c, beta_c, g_c, s0)
        dq_x, dk_x, dv_x, beta_x, g_x = dn_inputs(dn_qkv_x, dn_b_x, dn_a_x, dn_conv[i], dn_a_log[i], dn_dt_bias[i], True)
        do_x, _ = dn_bidirectional(dq_x, dk_x, dv_x, beta_x, g_x, s_ctx)
        out_b_x = dn_output(do_x, dn_z_x, dn_norm[i])

        filt_x = hyena_filters(l, hy_w1[i], hy_b1[i], hy_w2[i], hy_b2[i], hy_w3[i], hy_b3[i], hy_freq[i], hy_wout[i])
        out_c_x = hyena_mix(hy_vx_x, hy_z_x, hy_conv[i], filt_x, hy_skip[i])

        mix_x = jnp.concatenate([out_a_x, out_b_x, out_c_x], axis=-1)
        new_x = x + gate_x[:, None] * rms_norm(mix_x @ w_out[i], g_post[i])

        if not last:
            out_a_c = dense_ctx_attention(q_c, k_c, v_c) * jax.nn.silu(na_z_c)
            out_b_c = dn_output(do_c, dn_z_c, dn_norm[i])
            filt_c = hyena_filters(lc, hy_w1[i], hy_b1[i], hy_w2[i], hy_b2[i], hy_w3[i], hy_b3[i], hy_freq[i], hy_wout[i])
            out_c_c = hyena_mix(hy_vx_c, hy_z_c, hy_conv[i], filt_c, hy_skip[i])
            mix_c = jnp.concatenate([out_a_c, out_b_c, out_c_c], axis=-1)
            ctx = ctx + gate_c * rms_norm(mix_c @ w_out[i], g_post[i])
        x = new_x
    return x
```

```python
import functools
import math

import numpy as np
import jax
import jax.numpy as jnp
from jax import lax
from jax.experimental import pallas as pl
from jax.experimental.pallas import tpu as pltpu

D_MODEL = 1024
GRID_W = 64
NA_HEAD_DIM = 64
NA_WIDTH = 256
NA_HEADS = 4
NA_WIN_ROWS = 8
NA_WIN_COLS = 16
DN_HEAD_DIM = 128
DN_WIDTH = 512
DN_HEADS = 4
DN_CONV_W = 5
DN_CHUNK = 64
HY_WIDTH = 256
HY_ORDER = 2
HY_EMB = 33
HY_DECAY_TARGET = 1e-2
HY_FAST_DECAY = 0.3
HY_SLOW_DECAY = 1.5
ROPE_BASE = 10000.0
NORM_EPS = 1e-6

C_NA_QKV, C_NA_Z, C_DN_QKV, C_DN_Z, C_HY_VX, C_HY_Z = 0, 768, 1024, 2560, 3072, 3840
MAIN_W = 4096
GATE_W = 128
LANE = 128
VMEM_LIMIT = 52 * 1024 * 1024

f32 = jnp.float32
bf16 = jnp.bfloat16


def _mod_kernel(c_ref, w_ref, b_ref, o_ref):
    c = c_ref[...]
    a = c * jax.nn.sigmoid(c)
    o_ref[0] = jnp.dot(a, w_ref[0], preferred_element_type=f32, precision=lax.Precision.HIGHEST) + b_ref[0]


def modulation_all(cond, w_ada, b_ada):
    depth, d, d3 = w_ada.shape
    r = cond.shape[0]
    tn = 512
    return pl.pallas_call(
        _mod_kernel,
        grid=(depth, d3 // tn),
        in_specs=[pl.BlockSpec((r, d), lambda i, j: (0, 0)),
                  pl.BlockSpec((1, d, tn), lambda i, j: (i, 0, j)),
                  pl.BlockSpec((1, 1, tn), lambda i, j: (i, 0, j))],
        out_specs=pl.BlockSpec((1, r, tn), lambda i, j: (i, 0, j)),
        out_shape=jax.ShapeDtypeStruct((depth, r, d3), f32),
        compiler_params=pltpu.CompilerParams(dimension_semantics=("arbitrary", "arbitrary"),
                                             vmem_limit_bytes=VMEM_LIMIT),
        name="modulation",
    )(cond, w_ada, b_ada.reshape(depth, 1, d3))


def _inproj_kernel(x_ref, g_ref, sc_ref, sh_ref, w_ref, wg_ref, o_ref, og_ref):
    x = x_ref[0]
    ms = jnp.mean(x * x, axis=-1, keepdims=True)
    h = x * lax.rsqrt(ms + NORM_EPS) * g_ref[...]
    h = (h * (1.0 + sc_ref[0]) + sh_ref[0]).astype(bf16)
    for j in range(MAIN_W // 1024):
        o_ref[0, :, j * 1024:(j + 1) * 1024] = jnp.dot(
            h, w_ref[:, j * 1024:(j + 1) * 1024], preferred_element_type=f32).astype(bf16)
    og_ref[0] = jnp.dot(h, wg_ref[...], preferred_element_type=f32)


def in_projection(x, g, scale, shift, w_main, w_gate, tm):
    b, l, d = x.shape
    per_b = scale.shape[0] == b
    mod_map = (lambda bi, i: (bi, 0, 0)) if per_b else (lambda bi, i: (0, 0, 0))
    return pl.pallas_call(
        _inproj_kernel,
        grid=(b, l // tm),
        in_specs=[pl.BlockSpec((1, tm, d), lambda bi, i: (bi, i, 0)),
                  pl.BlockSpec((1, d), lambda bi, i: (0, 0)),
                  pl.BlockSpec((1, 1, d), mod_map),
                  pl.BlockSpec((1, 1, d), mod_map),
                  pl.BlockSpec((d, MAIN_W), lambda bi, i: (0, 0)),
                  pl.BlockSpec((d, GATE_W), lambda bi, i: (0, 0))],
        out_specs=[pl.BlockSpec((1, tm, MAIN_W), lambda bi, i: (bi, i, 0)),
                   pl.BlockSpec((1, tm, GATE_W), lambda bi, i: (bi, i, 0))],
        out_shape=[jax.ShapeDtypeStruct((b, l, MAIN_W), bf16),
                   jax.ShapeDtypeStruct((b, l, GATE_W), f32)],
        compiler_params=pltpu.CompilerParams(dimension_semantics=("arbitrary", "arbitrary"),
                                             vmem_limit_bytes=VMEM_LIMIT),
        name="in_projection",
    )(x, g.reshape(1, d), scale, shift, w_main, w_gate)


def _outproj_kernel(m_ref, w_ref, g_ref, gate_ref, x_ref, o_ref):
    y = jnp.dot(m_ref[0], w_ref[...], preferred_element_type=f32)
    ms = jnp.mean(y * y, axis=-1, keepdims=True)
    yn = y * lax.rsqrt(ms + NORM_EPS) * g_ref[...]
    o_ref[0] = x_ref[0] + gate_ref[0] * yn


def out_projection(mix, w_out, g_post, gate, x, tm):
    b, l, d = x.shape
    per_b = gate.shape[0] == b
    mod_map = (lambda bi, i: (bi, 0, 0)) if per_b else (lambda bi, i: (0, 0, 0))
    return pl.pallas_call(
        _outproj_kernel,
        grid=(b, l // tm),
        in_specs=[pl.BlockSpec((1, tm, d), lambda bi, i: (bi, i, 0)),
                  pl.BlockSpec((d, d), lambda bi, i: (0, 0)),
                  pl.BlockSpec((1, d), lambda bi, i: (0, 0)),
                  pl.BlockSpec((1, 1, d), mod_map),
                  pl.BlockSpec((1, tm, d), lambda bi, i: (bi, i, 0))],
        out_specs=pl.BlockSpec((1, tm, d), lambda bi, i: (bi, i, 0)),
        out_shape=jax.ShapeDtypeStruct((b, l, d), f32),
        compiler_params=pltpu.CompilerParams(dimension_semantics=("arbitrary", "arbitrary"),
                                             vmem_limit_bytes=VMEM_LIMIT),
        name="out_projection",
    )(mix, w_out, g_post.reshape(1, d), gate, x)


def rms_norm(x, g):
    xf = x.astype(f32)
    y = xf * lax.rsqrt(jnp.mean(jnp.square(xf), axis=-1, keepdims=True) + NORM_EPS)
    return (y * g.astype(f32)).astype(x.dtype)


def l2_normalize(x):
    xf = x.astype(f32)
    return xf * lax.rsqrt(jnp.sum(jnp.square(xf), axis=-1, keepdims=True) + NORM_EPS)


def dwconv_centred(x, w):
    k, ch = w.shape
    return lax.conv_general_dilated(x, w.astype(x.dtype)[:, None, :], (1,), [(k // 2, k // 2)],
                                    dimension_numbers=('NWC', 'WIO', 'NWC'), feature_group_count=ch)


def heads(t, n_heads, head_dim):
    return t.reshape(t.shape[0], t.shape[1], n_heads, head_dim)


def axial_rope(x):
    l, dh = x.shape[1], x.shape[-1]
    half = dh // 2
    nf = half // 2
    t = jnp.arange(l)
    inv = ROPE_BASE ** (-jnp.arange(nf, dtype=f32) / nf)
    xf = x.astype(f32)

    def rot(xp, pos):
        ang = pos.astype(f32)[:, None] * inv[None, :]
        cos = jnp.cos(ang)[None, :, None, :]
        sin = jnp.sin(ang)[None, :, None, :]
        a, b = xp[..., :nf], xp[..., nf:]
        return jnp.concatenate([a * cos - b * sin, b * cos + a * sin], axis=-1)

    out = jnp.concatenate([rot(xf[..., :half], t // GRID_W), rot(xf[..., half:], t % GRID_W)], axis=-1)
    return out.astype(x.dtype)


def na_latent(q, k, v, k_ctx, v_ctx, rpb):
    b, l, h, dh = q.shape
    rows = l // GRID_W
    kr = min(NA_WIN_ROWS, rows)
    kc = NA_WIN_COLS
    r = jnp.arange(rows)
    col = jnp.arange(GRID_W)
    band = jnp.clip(r - kr // 2, 0, rows - kr)[:, None] + jnp.arange(kr)[None, :]
    c0 = jnp.clip(col - kc // 2, 0, GRID_W - kc)
    in_win = (col[None, :] >= c0[:, None]) & (col[None, :] < c0[:, None] + kc)
    qg = q.reshape(b, rows, GRID_W, h, dh)
    kg = k.reshape(b, rows, GRID_W, h, dh)[:, band]
    vg = v.reshape(b, rows, GRID_W, h, dh)[:, band]
    scale = dh ** -0.5
    s_loc = jnp.einsum('brqhd,brkwhd->brhqkw', qg, kg).astype(f32) * scale
    dr = band - r[:, None] + (NA_WIN_ROWS - 1)
    dc = jnp.clip(col[None, :] - col[:, None], -(kc - 1), kc - 1) + (kc - 1)
    bias = rpb.astype(f32)[:, dr][:, :, :, dc]
    bias = jnp.transpose(bias, (1, 0, 3, 2, 4))
    s_loc = jnp.where(in_win[:, None, :], s_loc + bias, -jnp.inf)
    s_ctx = jnp.einsum('brqhd,bchd->brhqc', qg, k_ctx).astype(f32) * scale
    n_loc = kr * GRID_W
    s = jnp.concatenate([s_loc.reshape(b, rows, h, GRID_W, n_loc), s_ctx], axis=-1)
    p = jax.nn.softmax(s, axis=-1).astype(v.dtype)
    p_loc = p[..., :n_loc].reshape(b, rows, h, GRID_W, kr, GRID_W)
    o = (jnp.einsum('brhqkw,brkwhd->brqhd', p_loc, vg)
         + jnp.einsum('brhqc,bchd->brqhd', p[..., n_loc:], v_ctx))
    return o.reshape(b, l, h * dh)


def dense_ctx_attention(q, k, v):
    b, l, h, dh = q.shape
    s = jnp.einsum('bqhd,bkhd->bhqk', q, k).astype(f32) * dh ** -0.5
    p = jax.nn.softmax(s, axis=-1).astype(v.dtype)
    return jnp.einsum('bhqk,bkhd->bqhd', p, v).reshape(b, l, h * dh)


def dn_inputs(qkv_raw, b_raw, a_raw, conv_w, a_log, dt_bias, rope):
    bsz, l, _ = qkv_raw.shape
    qkv = jax.nn.silu(dwconv_centred(qkv_raw, conv_w))
    q, k, v = [heads(t, DN_HEADS, DN_HEAD_DIM) for t in jnp.split(qkv, 3, axis=-1)]
    if rope:
        q, k = axial_rope(q), axial_rope(k)
    q = l2_normalize(q) * (DN_HEAD_DIM ** -0.5)
    k = l2_normalize(k)
    v = v.astype(f32)
    beta = jax.nn.sigmoid(b_raw.astype(f32)).reshape(bsz, l, 2, DN_HEADS)
    g = -jnp.exp(a_log.astype(f32)) * jax.nn.softplus(
        a_raw.astype(f32).reshape(bsz, l, 2, DN_HEADS) + dt_bias.astype(f32))
    beta = jnp.transpose(beta, (2, 0, 3, 1))
    g = jnp.transpose(g, (2, 0, 3, 1))
    to_bhl = lambda t: jnp.transpose(t, (0, 2, 1, 3))
    return to_bhl(q), to_bhl(k), to_bhl(v), beta, g


def gated_delta_chunked(q, k, v, beta, g, s0):
    b, h, l, _ = q.shape
    dv = v.shape[-1]
    n = l // DN_CHUNK
    q, k, v = [t.reshape(b, h, n, DN_CHUNK, t.shape[-1]) for t in (q, k, v)]
    beta = beta.reshape(b, h, n, DN_CHUNK)
    gc = jnp.cumsum(g.reshape(b, h, n, DN_CHUNK), axis=-1)
    pos = jnp.arange(DN_CHUNK)
    tri = pos[:, None] >= pos[None, :]
    strict = pos[:, None] > pos[None, :]
    diff = gc[..., :, None] - gc[..., None, :]
    decay = jnp.where(tri, jnp.exp(jnp.where(tri, diff, 0.0)), 0.0)
    kb = k * beta[..., None]
    a_kk = jnp.where(strict, jnp.einsum('bhnid,bhnjd->bhnij', kb, k) * decay, 0.0)
    rhs = jnp.concatenate([v * beta[..., None], kb * jnp.exp(gc)[..., None]], axis=-1)
    sol = lax.linalg.triangular_solve(a_kk + jnp.eye(DN_CHUNK, dtype=f32), rhs,
                                      left_side=True, lower=True, unit_diagonal=True)
    u, w = sol[..., :dv], sol[..., dv:]
    a_qk = jnp.where(tri, jnp.einsum('bhnid,bhnjd->bhnij', q, k) * decay, 0.0)

    def step(s, xs):
        q_i, k_i, u_i, w_i, g_i, aqk_i = xs
        v_new = u_i - jnp.einsum('bhcd,bhde->bhce', w_i, s)
        o_i = (jnp.einsum('bhcd,bhde->bhce', q_i * jnp.exp(g_i)[..., None], s)
               + jnp.einsum('bhij,bhje->bhie', aqk_i, v_new))
        g_last = g_i[..., -1]
        s = (s * jnp.exp(g_last)[..., None, None]
             + jnp.einsum('bhcd,bhce->bhde', k_i * jnp.exp(g_last[..., None] - g_i)[..., None], v_new))
        return s, o_i

    xs = tuple(jnp.moveaxis(t, 2, 0) for t in (q, k, u, w, gc, a_qk))
    s_fin, o = lax.scan(step, s0, xs)
    return jnp.moveaxis(o, 0, 2).reshape(b, h, l, dv), s_fin


def dn_bidirectional(q, k, v, beta, g, s0):
    flip = lambda t: jnp.flip(t, axis=2)
    o_f, s_f = gated_delta_chunked(q, k, v, beta[0], g[0], s0[0])
    o_b, s_b = gated_delta_chunked(flip(q), flip(k), flip(v), flip(beta[1]), flip(g[1]), s0[1])
    return o_f + flip(o_b), jnp.stack([s_f, s_b])


def dn_output(o, z, norm_w):
    b, h, l, dv = o.shape
    o = jnp.transpose(o, (0, 2, 1, 3))
    y = rms_norm(o, norm_w) * jax.nn.silu(z.reshape(b, l, h, dv).astype(f32))
    return y.reshape(b, l, h * dv).astype(z.dtype)


def hyena_filters(l, w1, b1, w2, b2, w3, b3, freq, wout):
    t = jnp.linspace(0.0, 1.0, l, dtype=f32)[:, None]
    bands = (HY_EMB - 1) // 2
    fr = jnp.linspace(1e-4, bands - 1, bands, dtype=f32)
    ang = (2.0 * math.pi / l) * jnp.arange(l, dtype=f32)[:, None] * fr[None, :]
    z = jnp.concatenate([t, jnp.cos(ang), -jnp.sin(ang)], axis=-1)
    freq = freq.astype(f32)
    hid = jnp.sin(freq[0] * (z @ w1.astype(f32) + b1.astype(f32)))
    hid = jnp.sin(freq[1] * (hid @ w2.astype(f32) + b2.astype(f32)))
    hid = jnp.sin(freq[2] * (hid @ w3.astype(f32) + b3.astype(f32)))
    filt = (hid @ wout.astype(f32)).reshape(l, HY_ORDER, 2, HY_WIDTH)
    max_decay = math.log(HY_DECAY_TARGET) / HY_FAST_DECAY
    min_decay = math.log(HY_DECAY_TARGET) / HY_SLOW_DECAY
    deltas = jnp.abs(jnp.linspace(min_decay, max_decay, HY_WIDTH, dtype=f32))
    window = jnp.exp(-t * deltas[None, :])
    return filt * window[:, None, None, :]


def bidir_long_conv(u, h_fwd, h_bwd):
    b, l, ch = u.shape
    n = 2 * l
    kern = jnp.concatenate([h_fwd, jnp.zeros((1, ch), f32), h_bwd[:0:-1]], axis=0)
    y = jnp.fft.irfft(jnp.fft.rfft(u.astype(f32), n=n, axis=1) * jnp.fft.rfft(kern, n=n, axis=0)[None],
                      n=n, axis=1)
    return y[:, :l]


def hyena_mix(vx_raw, z, conv_w, filt, skip):
    vx = dwconv_centred(vx_raw, conv_w).astype(f32)
    v, x1, x2 = jnp.split(vx, 3, axis=-1)
    y = v
    for n, gate in enumerate((x1, x2)):
        y = gate * (bidir_long_conv(y, filt[:, n, 0], filt[:, n, 1]) + y * skip[n].astype(f32))
    return (y * jax.nn.silu(z.astype(f32))).astype(z.dtype)


def _pack_w_in(w_in):
    depth, d, _ = w_in.shape
    main = jnp.concatenate([w_in[:, :, :3072], w_in[:, :, 3088:4112]], axis=-1).astype(bf16)
    gate = jnp.concatenate([w_in[:, :, 3072:3088], jnp.zeros((depth, d, GATE_W - 16), w_in.dtype)],
                           axis=-1).astype(bf16)
    return main, gate


def kernel(x, c, ctx, c_ctx, w_ada, b_ada, g_pre, g_post, w_in, w_out, na_rpb, dn_conv, dn_a_log, dn_dt_bias,
           dn_norm, hy_conv, hy_w1, hy_b1, hy_w2, hy_b2, hy_w3, hy_b3, hy_freq, hy_wout, hy_skip):
    bsz, l, d = x.shape
    lc = ctx.shape[1]
    depth = w_in.shape[0]
    cond = jnp.concatenate([c, c_ctx[None], jnp.zeros((7, d), f32)], axis=0)
    mod = modulation_all(cond, w_ada, b_ada)
    w_main, w_gate = _pack_w_in(w_in)
    w_out_b = w_out.astype(bf16)

    for i in range(depth):
        last = i == depth - 1
        shift_x, scale_x, gate_x = [mod[i, :bsz, j * d:(j + 1) * d][:, None] for j in range(3)]
        shift_c, scale_c, gate_c = [mod[i, bsz:bsz + 1, j * d:(j + 1) * d][:, None] for j in range(3)]
        px, gx = in_projection(x, g_pre[i], scale_x, shift_x, w_main[i], w_gate[i], 512)
        pc, gc = in_projection(ctx, g_pre[i], scale_c, shift_c, w_main[i], w_gate[i], 256)

        pxf, pcf = px.astype(f32), pc.astype(f32)
        sl = lambda t, a, n: t[..., a:a + n]
        na_qkv_x, na_z_x = sl(pxf, C_NA_QKV, 768), sl(pxf, C_NA_Z, 256)
        dn_qkv_x, dn_z_x = sl(pxf, C_DN_QKV, 1536), sl(pxf, C_DN_Z, 512)
        hy_vx_x, hy_z_x = sl(pxf, C_HY_VX, 768), sl(pxf, C_HY_Z, 256)
        dn_b_x, dn_a_x = gx[..., :8], gx[..., 8:16]
        na_qkv_c, na_z_c = sl(pcf, C_NA_QKV, 768), sl(pcf, C_NA_Z, 256)
        dn_qkv_c, dn_z_c = sl(pcf, C_DN_QKV, 1536), sl(pcf, C_DN_Z, 512)
        hy_vx_c, hy_z_c = sl(pcf, C_HY_VX, 768), sl(pcf, C_HY_Z, 256)
        dn_b_c, dn_a_c = gc[..., :8], gc[..., 8:16]

        q_x, k_x, v_x = [heads(t, NA_HEADS, NA_HEAD_DIM) for t in jnp.split(na_qkv_x, 3, axis=-1)]
        q_c, k_c, v_c = [heads(t, NA_HEADS, NA_HEAD_DIM) for t in jnp.split(na_qkv_c, 3, axis=-1)]
        out_a_x = na_latent(q_x, k_x, v_x, k_c, v_c, na_rpb[i]) * jax.nn.silu(na_z_x)

        dq_c, dk_c, dv_c, beta_c, g_c = dn_inputs(dn_qkv_c, dn_b_c, dn_a_c, dn_conv[i], dn_a_log[i], dn_dt_bias[i], False)
        s0 = jnp.zeros((2, bsz, DN_HEADS, DN_HEAD_DIM, DN_HEAD_DIM), f32)
        do_c, s_ctx = dn_bidirectional(dq_c, dk_c, dv_c, beta_c, g_c, s0)
        dq_x, dk_x, dv_x, beta_x, g_x = dn_inputs(dn_qkv_x, dn_b_x, dn_a_x, dn_conv[i], dn_a_log[i], dn_dt_bias[i], True)
        do_x, _ = dn_bidirectional(dq_x, dk_x, dv_x, beta_x, g_x, s_ctx)
        out_b_x = dn_output(do_x, dn_z_x, dn_norm[i])

        filt_x = hyena_filters(l, hy_w1[i], hy_b1[i], hy_w2[i], hy_b2[i], hy_w3[i], hy_b3[i], hy_freq[i], hy_wout[i])
        out_c_x = hyena_mix(hy_vx_x, hy_z_x, hy_conv[i], filt_x, hy_skip[i])

        mix_x = jnp.concatenate([out_a_x, out_b_x, out_c_x], axis=-1).astype(bf16)
        new_x = out_projection(mix_x, w_out_b[i], g_post[i], gate_x, x, 512)

        if not last:
            out_a_c = dense_ctx_attention(q_c, k_c, v_c) * jax.nn.silu(na_z_c)
            out_b_c = dn_output(do_c, dn_z_c, dn_norm[i])
            filt_c = hyena_filters(lc, hy_w1[i], hy_b1[i], hy_w2[i], hy_b2[i], hy_w3[i], hy_b3[i], hy_freq[i], hy_wout[i])
            out_c_c = hyena_mix(hy_vx_c, hy_z_c, hy_conv[i], filt_c, hy_skip[i])
            mix_c = jnp.concatenate([out_a_c, out_b_c, out_c_c], axis=-1).astype(bf16)
            ctx = out_projection(mix_c, w_out_b[i], g_post[i], gate_c, ctx, 256)
        x = new_x
    return x
```

```python
import functools
import math

import numpy as np
import jax
import jax.numpy as jnp
from jax import lax
from jax.experimental import pallas as pl
from jax.experimental.pallas import tpu as pltpu

D_MODEL = 1024
GRID_W = 64
NA_HEAD_DIM = 64
NA_WIDTH = 256
NA_HEADS = 4
NA_WIN_ROWS = 8
NA_WIN_COLS = 16
DN_HEAD_DIM = 128
DN_WIDTH = 512
DN_HEADS = 4
DN_CONV_W = 5
DN_CHUNK = 64
HY_WIDTH = 256
HY_ORDER = 2
HY_EMB = 33
HY_DECAY_TARGET = 1e-2
HY_FAST_DECAY = 0.3
HY_SLOW_DECAY = 1.5
ROPE_BASE = 10000.0
NORM_EPS = 1e-6

C_NA_QKV, C_NA_Z, C_DN_QKV, C_DN_Z, C_HY_VX, C_HY_Z = 0, 768, 1024, 2560, 3072, 3840
MAIN_W = 4096
GATE_W = 128
LANE = 128
VMEM_LIMIT = 52 * 1024 * 1024

f32 = jnp.float32
bf16 = jnp.bfloat16


def _mod_kernel(c_ref, w_ref, b_ref, o_ref):
    c = c_ref[...]
    a = c * jax.nn.sigmoid(c)
    o_ref[0] = jnp.dot(a, w_ref[0], preferred_element_type=f32, precision=lax.Precision.HIGHEST) + b_ref[0]


def modulation_all(cond, w_ada, b_ada):
    depth, d, d3 = w_ada.shape
    r = cond.shape[0]
    tn = 512
    return pl.pallas_call(
        _mod_kernel,
        grid=(depth, d3 // tn),
        in_specs=[pl.BlockSpec((r, d), lambda i, j: (0, 0)),
                  pl.BlockSpec((1, d, tn), lambda i, j: (i, 0, j)),
                  pl.BlockSpec((1, 1, tn), lambda i, j: (i, 0, j))],
        out_specs=pl.BlockSpec((1, r, tn), lambda i, j: (i, 0, j)),
        out_shape=jax.ShapeDtypeStruct((depth, r, d3), f32),
        compiler_params=pltpu.CompilerParams(dimension_semantics=("arbitrary", "arbitrary"),
                                             vmem_limit_bytes=VMEM_LIMIT),
        name="modulation",
    )(cond, w_ada, b_ada.reshape(depth, 1, d3))


def _inproj_kernel(x_ref, g_ref, sc_ref, sh_ref, w_ref, wg_ref, o_ref, og_ref):
    x = x_ref[0]
    ms = jnp.mean(x * x, axis=-1, keepdims=True)
    h = x * lax.rsqrt(ms + NORM_EPS) * g_ref[...]
    h = (h * (1.0 + sc_ref[0]) + sh_ref[0]).astype(bf16)
    for j in range(MAIN_W // 1024):
        o_ref[0, :, j * 1024:(j + 1) * 1024] = jnp.dot(
            h, w_ref[:, j * 1024:(j + 1) * 1024], preferred_element_type=f32).astype(bf16)
    og_ref[0] = jnp.dot(h, wg_ref[...], preferred_element_type=f32)


def in_projection(x, g, scale, shift, w_main, w_gate, tm):
    b, l, d = x.shape
    per_b = scale.shape[0] == b
    mod_map = (lambda bi, i: (bi, 0, 0)) if per_b else (lambda bi, i: (0, 0, 0))
    return pl.pallas_call(
        _inproj_kernel,
        grid=(b, l // tm),
        in_specs=[pl.BlockSpec((1, tm, d), lambda bi, i: (bi, i, 0)),
                  pl.BlockSpec((1, d), lambda bi, i: (0, 0)),
                  pl.BlockSpec((1, 1, d), mod_map),
                  pl.BlockSpec((1, 1, d), mod_map),
                  pl.BlockSpec((d, MAIN_W), lambda bi, i: (0, 0)),
                  pl.BlockSpec((d, GATE_W), lambda bi, i: (0, 0))],
        out_specs=[pl.BlockSpec((1, tm, MAIN_W), lambda bi, i: (bi, i, 0)),
                   pl.BlockSpec((1, tm, GATE_W), lambda bi, i: (bi, i, 0))],
        out_shape=[jax.ShapeDtypeStruct((b, l, MAIN_W), bf16),
                   jax.ShapeDtypeStruct((b, l, GATE_W), f32)],
        compiler_params=pltpu.CompilerParams(dimension_semantics=("arbitrary", "arbitrary"),
                                             vmem_limit_bytes=VMEM_LIMIT),
        name="in_projection",
    )(x, g.reshape(1, d), scale, shift, w_main, w_gate)


def _outproj_kernel(m_ref, w_ref, g_ref, gate_ref, x_ref, o_ref):
    y = jnp.dot(m_ref[0], w_ref[...], preferred_element_type=f32)
    ms = jnp.mean(y * y, axis=-1, keepdims=True)
    yn = y * lax.rsqrt(ms + NORM_EPS) * g_ref[...]
    o_ref[0] = x_ref[0] + gate_ref[0] * yn


def out_projection(mix, w_out, g_post, gate, x, tm):
    b, l, d = x.shape
    per_b = gate.shape[0] == b
    mod_map = (lambda bi, i: (bi, 0, 0)) if per_b else (lambda bi, i: (0, 0, 0))
    return pl.pallas_call(
        _outproj_kernel,
        grid=(b, l // tm),
        in_specs=[pl.BlockSpec((1, tm, d), lambda bi, i: (bi, i, 0)),
                  pl.BlockSpec((d, d), lambda bi, i: (0, 0)),
                  pl.BlockSpec((1, d), lambda bi, i: (0, 0)),
                  pl.BlockSpec((1, 1, d), mod_map),
                  pl.BlockSpec((1, tm, d), lambda bi, i: (bi, i, 0))],
        out_specs=pl.BlockSpec((1, tm, d), lambda bi, i: (bi, i, 0)),
        out_shape=jax.ShapeDtypeStruct((b, l, d), f32),
        compiler_params=pltpu.CompilerParams(dimension_semantics=("arbitrary", "arbitrary"),
                                             vmem_limit_bytes=VMEM_LIMIT),
        name="out_projection",
    )(mix, w_out, g_post.reshape(1, d), gate, x)


NA_TQ_ROWS = 4
NA_TQ = NA_TQ_ROWS * GRID_W
NA_KV_ROWS = NA_TQ_ROWS + NA_WIN_ROWS
NA_KV = NA_KV_ROWS * GRID_W
MASK_VALUE = -1e30
NT_DIMS = (((1,), (1,)), ((), ()))


def _na_window_start(t, rows):
    return jnp.clip(t * NA_TQ_ROWS - NA_WIN_ROWS // 2, 0, rows - NA_KV_ROWS)


def na_bias_tables(rpb, rows):
    n_tiles = rows // NA_TQ_ROWS
    tabs = []
    for t in (0, 1, n_tiles - 1):
        r0 = t * NA_TQ_ROWS
        ws = int(np.clip(r0 - NA_WIN_ROWS // 2, 0, rows - NA_KV_ROWS))
        qr = r0 + np.arange(NA_TQ) // GRID_W
        qc = np.arange(NA_TQ) % GRID_W
        kr = ws + np.arange(NA_KV) // GRID_W
        kc = np.arange(NA_KV) % GRID_W
        band0 = np.clip(qr - NA_WIN_ROWS // 2, 0, rows - NA_WIN_ROWS)
        c0 = np.clip(qc - NA_WIN_COLS // 2, 0, GRID_W - NA_WIN_COLS)
        valid = ((kr[None, :] >= band0[:, None]) & (kr[None, :] < band0[:, None] + NA_WIN_ROWS)
                 & (kc[None, :] >= c0[:, None]) & (kc[None, :] < c0[:, None] + NA_WIN_COLS))
        dr = np.clip(kr[None, :] - qr[:, None] + (NA_WIN_ROWS - 1), 0, 2 * NA_WIN_ROWS - 2)
        dc = np.clip(kc[None, :] - qc[:, None], -(NA_WIN_COLS - 1), NA_WIN_COLS - 1) + (NA_WIN_COLS - 1)
        tab = rpb.astype(f32)[:, dr, dc]
        tabs.append(jnp.where(valid[None], tab, MASK_VALUE))
    return jnp.stack(tabs)


def _na_kernel(q_ref, k_ref, v_ref, z_ref, kc_ref, vc_ref, bias_ref, o_ref, *, rows):
    t = pl.program_id(1)
    ws = pl.multiple_of(_na_window_start(t, rows) * GRID_W, GRID_W)
    q = q_ref[0]
    kw = k_ref[0, pl.ds(ws, NA_KV), :]
    vw = v_ref[0, pl.ds(ws, NA_KV), :]
    kc = kc_ref[0]
    vc = vc_ref[0]
    outs = []
    for h in range(NA_HEADS):
        sl = slice(h * NA_HEAD_DIM, (h + 1) * NA_HEAD_DIM)
        qh = q[:, sl] * (NA_HEAD_DIM ** -0.5)
        s1 = lax.dot_general(qh, kw[:, sl], NT_DIMS, preferred_element_type=f32) + bias_ref[0, h]
        s2 = lax.dot_general(qh, kc[:, sl], NT_DIMS, preferred_element_type=f32)
        m = jnp.maximum(jnp.max(s1, axis=-1, keepdims=True), jnp.max(s2, axis=-1, keepdims=True))
        p1 = jnp.exp(s1 - m)
        p2 = jnp.exp(s2 - m)
        den = jnp.sum(p1, axis=-1, keepdims=True) + jnp.sum(p2, axis=-1, keepdims=True)
        o = (jnp.dot(p1.astype(bf16), vw[:, sl], preferred_element_type=f32)
             + jnp.dot(p2.astype(bf16), vc[:, sl], preferred_element_type=f32))
        outs.append(o / den)
    z = z_ref[0].astype(f32)
    o = jnp.concatenate(outs, axis=-1) * (z * jax.nn.sigmoid(z))
    o_ref[0] = o.astype(o_ref.dtype)


def na_attention(px, pc, rpb):
    b, l, _ = px.shape
    lc = pc.shape[1]
    rows = l // GRID_W
    n_tiles = rows // NA_TQ_ROWS
    bias = na_bias_tables(rpb, rows)
    w = NA_WIDTH
    cq, ck, cv, cz = [(C_NA_QKV + j * w) // w for j in range(3)] + [C_NA_Z // w]

    def bias_map(bi, t):
        return (jnp.where(t == 0, 0, jnp.where(t == n_tiles - 1, 2, 1)), 0, 0, 0)

    return pl.pallas_call(
        functools.partial(_na_kernel, rows=rows),
        grid=(b, n_tiles),
        in_specs=[pl.BlockSpec((1, NA_TQ, w), lambda bi, t: (bi, t, cq)),
                  pl.BlockSpec((1, l, w), lambda bi, t: (bi, 0, ck)),
                  pl.BlockSpec((1, l, w), lambda bi, t: (bi, 0, cv)),
                  pl.BlockSpec((1, NA_TQ, w), lambda bi, t: (bi, t, cz)),
                  pl.BlockSpec((1, lc, w), lambda bi, t: (bi, 0, ck)),
                  pl.BlockSpec((1, lc, w), lambda bi, t: (bi, 0, cv)),
                  pl.BlockSpec((1, NA_HEADS, NA_TQ, NA_KV), bias_map)],
        out_specs=pl.BlockSpec((1, NA_TQ, w), lambda bi, t: (bi, t, 0)),
        out_shape=jax.ShapeDtypeStruct((b, l, w), bf16),
        compiler_params=pltpu.CompilerParams(dimension_semantics=("arbitrary", "arbitrary"),
                                             vmem_limit_bytes=VMEM_LIMIT),
        name="na_attention",
    )(px, px, px, px, pc, pc, bias)


def _ctx_attn_kernel(q_ref, k_ref, v_ref, z_ref, o_ref):
    q = q_ref[0]
    k = k_ref[0]
    v = v_ref[0]
    outs = []
    for h in range(NA_HEADS):
        sl = slice(h * NA_HEAD_DIM, (h + 1) * NA_HEAD_DIM)
        qh = q[:, sl] * (NA_HEAD_DIM ** -0.5)
        s = lax.dot_general(qh, k[:, sl], NT_DIMS, preferred_element_type=f32)
        p = jnp.exp(s - jnp.max(s, axis=-1, keepdims=True))
        den = jnp.sum(p, axis=-1, keepdims=True)
        outs.append(jnp.dot(p.astype(bf16), v[:, sl], preferred_element_type=f32) / den)
    z = z_ref[0].astype(f32)
    o_ref[0] = (jnp.concatenate(outs, axis=-1) * (z * jax.nn.sigmoid(z))).astype(o_ref.dtype)


def ctx_attention(pc):
    b, lc, _ = pc.shape
    w = NA_WIDTH
    cq, ck, cv, cz = [(C_NA_QKV + j * w) // w for j in range(3)] + [C_NA_Z // w]
    spec = lambda cidx: pl.BlockSpec((1, lc, w), lambda bi: (bi, 0, cidx))
    return pl.pallas_call(
        _ctx_attn_kernel,
        grid=(b,),
        in_specs=[spec(cq), spec(ck), spec(cv), spec(cz)],
        out_specs=pl.BlockSpec((1, lc, w), lambda bi: (bi, 0, 0)),
        out_shape=jax.ShapeDtypeStruct((b, lc, w), bf16),
        compiler_params=pltpu.CompilerParams(dimension_semantics=("arbitrary",), vmem_limit_bytes=VMEM_LIMIT),
        name="ctx_attention",
    )(pc, pc, pc, pc)


HIGHEST = lax.Precision.HIGHEST
HY_CH = 128
FFT_N2 = 128
ROW_CHUNK = 256
STRIDE_PAD = 8
HY_FILT_ROWS = 512
HY_BANDS = (HY_EMB - 1) // 2


def _hy_filter_kernel(fr_ref, w1t_ref, w1c_ref, w1s_ref, b1_ref, w2_ref, b2_ref, w3_ref, b3_ref, freq_ref, wo_ref,
                      dl_ref, o_ref, *, l):
    i = pl.program_id(0)
    n = i * HY_FILT_ROWS + lax.broadcasted_iota(jnp.int32, (HY_FILT_ROWS, 1), 0)
    lag = jnp.where(n < l, n, 2 * l - n).astype(f32)
    t = lag * (1.0 / (l - 1))
    ang = (lag * (2.0 * math.pi / l)) * fr_ref[...]
    dot = functools.partial(jnp.dot, preferred_element_type=f32, precision=HIGHEST)
    pre = t * w1t_ref[...] + dot(jnp.cos(ang), w1c_ref[...]) - dot(jnp.sin(ang), w1s_ref[...]) + b1_ref[...]
    hid = jnp.sin(freq_ref[0:1] * pre)
    hid = jnp.sin(freq_ref[1:2] * (dot(hid, w2_ref[...]) + b2_ref[...]))
    hid = jnp.sin(freq_ref[2:3] * (dot(hid, w3_ref[...]) + b3_ref[...]))
    filt = dot(hid, wo_ref[...])
    window = jnp.exp(-t * dl_ref[...])
    live = jnp.where(n == l, 0.0, 1.0)
    fwd = n < l
    for o in range(HY_ORDER):
        a = filt[:, (2 * o) * HY_WIDTH:(2 * o + 1) * HY_WIDTH]
        b = filt[:, (2 * o + 1) * HY_WIDTH:(2 * o + 2) * HY_WIDTH]
        o_ref[:, o * HY_WIDTH:(o + 1) * HY_WIDTH] = jnp.where(fwd, a, b) * window * live


def hyena_filter_circular(l, w1, b1, w2, b2, w3, b3, freq, wout):
    n = 2 * l
    assert n % HY_FILT_ROWS == 0
    hid = w1.shape[1]
    fr = np.zeros((1, LANE), np.float32)
    fr[0, :HY_BANDS] = np.linspace(1e-4, HY_BANDS - 1, HY_BANDS, dtype=np.float32)
    padrows = lambda w: jnp.concatenate([w, jnp.zeros((LANE - w.shape[0], hid), f32)], axis=0)
    w1t, w1c, w1s = w1[0:1], padrows(w1[1:1 + HY_BANDS]), padrows(w1[1 + HY_BANDS:])
    max_decay = math.log(HY_DECAY_TARGET) / HY_FAST_DECAY
    min_decay = math.log(HY_DECAY_TARGET) / HY_SLOW_DECAY
    deltas = np.abs(np.linspace(min_decay, max_decay, HY_WIDTH, dtype=np.float32))[None]
    full = lambda a: pl.BlockSpec(a.shape, lambda i: (0,) * a.ndim)
    args = [jnp.asarray(fr), w1t, w1c, w1s, b1[None], w2, b2[None], w3, b3[None], freq, wout, jnp.asarray(deltas)]
    return pl.pallas_call(
        functools.partial(_hy_filter_kernel, l=l),
        grid=(n // HY_FILT_ROWS,),
        in_specs=[full(a) for a in args],
        out_specs=pl.BlockSpec((HY_FILT_ROWS, HY_ORDER * HY_WIDTH), lambda i: (i, 0)),
        out_shape=jax.ShapeDtypeStruct((n, HY_ORDER * HY_WIDTH), f32),
        compiler_params=pltpu.CompilerParams(dimension_semantics=("arbitrary",), vmem_limit_bytes=VMEM_LIMIT),
        name="hyena_filter",
    )(*args)


@functools.lru_cache(maxsize=None)
def _fft_consts(n, k_in, k_out):
    n1 = n // FFT_N2
    k1 = np.arange(n1)
    n2 = np.arange(FFT_N2)
    tt = FFT_N2 * np.arange(k_in)[None, None, :] + n2[:, None, None]
    ang = -2.0 * np.pi * (k1[None, :, None] * tt) / n
    f1 = np.concatenate([np.cos(ang), np.sin(ang)], axis=1)
    a2 = -2.0 * np.pi * np.outer(n2, n2) / FFT_N2
    cr, ci = np.cos(a2), np.sin(a2)
    fblk = np.block([[cr, -ci], [ci, cr]])
    fiblk = np.block([[cr, ci], [-ci, cr]])
    to = FFT_N2 * np.arange(k_out)[None, :, None] + n2[:, None, None]
    ango = 2.0 * np.pi * (k1[None, None, :] * to) / n
    hinv = np.concatenate([np.cos(ango), -np.sin(ango)], axis=2) / n
    return f1, fblk, fiblk, hinv


def _fft_fwd_stage1(x_ref, f1_ref, ar_ref, ai_ref, k_in, n1, sa, dt, prec):
    def body(n2, c):
        xs = x_ref[pl.ds(n2, k_in, stride=FFT_N2), :].astype(dt)
        a = jnp.dot(f1_ref[n2], xs, preferred_element_type=f32, precision=prec)
        off = pl.multiple_of(n2 * sa, 8)
        ar_ref[pl.ds(off, n1), :] = a[:n1]
        ai_ref[pl.ds(off, n1), :] = a[n1:]
        return c
    lax.fori_loop(0, FFT_N2, body, 0)


def _filter_fft_kernel(k_ref, f1_ref, fblk_ref, hr_ref, hi_ref, ar_ref, ai_ref, *, n):
    n1 = n // FFT_N2
    sa = n1 + STRIDE_PAD
    _fft_fwd_stage1(k_ref, f1_ref, ar_ref, ai_ref, n1, n1, sa, f32, HIGHEST)

    def body(k1, c):
        z = jnp.concatenate([ar_ref[pl.ds(k1, FFT_N2, stride=sa), :], ai_ref[pl.ds(k1, FFT_N2, stride=sa), :]], axis=0)
        xx = jnp.dot(fblk_ref[...], z, preferred_element_type=f32, precision=HIGHEST)
        off = pl.multiple_of(k1 * FFT_N2, FFT_N2)
        hr_ref[pl.ds(off, FFT_N2), :] = xx[:FFT_N2]
        hi_ref[pl.ds(off, FFT_N2), :] = xx[FFT_N2:]
        return c
    lax.fori_loop(0, n1, body, 0)


def filter_spectrum(kern):
    n, c = kern.shape
    n1 = n // FFT_N2
    f1, fblk, _, _ = _fft_consts(n, n1, 1)
    f1 = jnp.asarray(f1, f32)
    fblk = jnp.asarray(fblk, f32)
    sa = n1 + STRIDE_PAD
    return pl.pallas_call(
        functools.partial(_filter_fft_kernel, n=n),
        grid=(c // HY_CH,),
        in_specs=[pl.BlockSpec((n, HY_CH), lambda j: (0, j)),
                  pl.BlockSpec(f1.shape, lambda j: (0, 0, 0)),
                  pl.BlockSpec(fblk.shape, lambda j: (0, 0))],
        out_specs=[pl.BlockSpec((n, HY_CH), lambda j: (0, j))] * 2,
        out_shape=[jax.ShapeDtypeStruct((n, c), f32)] * 2,
        scratch_shapes=[pltpu.VMEM((FFT_N2 * sa, HY_CH), f32)] * 2,
        compiler_params=pltpu.CompilerParams(dimension_semantics=("arbitrary",), vmem_limit_bytes=VMEM_LIMIT),
        name="filter_spectrum",
    )(kern, f1, fblk)


def _conv3_rows(src_ref, w_ref, pad_ref, dst_ref, l):
    zeros = jnp.zeros((8, src_ref.shape[-1]), f32)
    pad_ref[0:8, :] = zeros
    pad_ref[l + 8:l + 16, :] = zeros

    def cp(i, c):
        r = pl.multiple_of(i * ROW_CHUNK, ROW_CHUNK)
        pad_ref[pl.ds(8 + r, ROW_CHUNK), :] = src_ref[0, pl.ds(r, ROW_CHUNK), :].astype(f32)
        return c
    lax.fori_loop(0, l // ROW_CHUNK, cp, 0)
    w = w_ref[...].astype(f32)

    def cv(i, c):
        r = pl.multiple_of(i * ROW_CHUNK, ROW_CHUNK)
        blk = pad_ref[pl.ds(r, ROW_CHUNK + 16), :]
        dst_ref[pl.ds(r, ROW_CHUNK), :] = (w[0:1] * blk[7:7 + ROW_CHUNK] + w[1:2] * blk[8:8 + ROW_CHUNK]
                                           + w[2:3] * blk[9:9 + ROW_CHUNK])
        return c
    lax.fori_loop(0, l // ROW_CHUNK, cv, 0)


def _hy_order_kernel(yin_ref, graw_ref, z_ref, cwy_ref, cwg_ref, skip_ref, hr_ref, hi_ref, f1_ref, fblk_ref, fiblk_ref,
                     hinv_ref, o_ref, pad_ref, y_ref, gate_ref, c_ref, ar_ref, ai_ref, br_ref, bi_ref,
                     *, l, first, last):
    n = 2 * l
    n1 = n // FFT_N2
    k1n = l // FFT_N2
    sa = n1 + STRIDE_PAD
    sb = FFT_N2 + STRIDE_PAD
    if first:
        _conv3_rows(yin_ref, cwy_ref, pad_ref, y_ref, l)
        src_ref = y_ref
    else:
        src_ref = yin_ref.at[0]
    _conv3_rows(graw_ref, cwg_ref, pad_ref, gate_ref, l)

    _fft_fwd_stage1(src_ref, f1_ref, ar_ref, ai_ref, k1n, n1, sa, bf16, None)

    def mid(k1, c):
        z = jnp.concatenate([ar_ref[pl.ds(k1, FFT_N2, stride=sa), :], ai_ref[pl.ds(k1, FFT_N2, stride=sa), :]], axis=0)
        xx = jnp.dot(fblk_ref[...], z.astype(bf16), preferred_element_type=f32)
        xr, xi = xx[:FFT_N2], xx[FFT_N2:]
        off = pl.multiple_of(k1 * FFT_N2, FFT_N2)
        hr = hr_ref[pl.ds(off, FFT_N2), :]
        hi = hi_ref[pl.ds(off, FFT_N2), :]
        yy = jnp.concatenate([xr * hr - xi * hi, xr * hi + xi * hr], axis=0).astype(bf16)
        bb = jnp.dot(fiblk_ref[...], yy, preferred_element_type=f32)
        boff = pl.multiple_of(k1 * sb, 8)
        br_ref[pl.ds(boff, FFT_N2), :] = bb[:FFT_N2]
        bi_ref[pl.ds(boff, FFT_N2), :] = bb[FFT_N2:]
        return c
    lax.fori_loop(0, n1, mid, 0)

    def inv2(n2, c):
        bb = jnp.concatenate([br_ref[pl.ds(n2, n1, stride=sb), :], bi_ref[pl.ds(n2, n1, stride=sb), :]], axis=0)
        o = jnp.dot(hinv_ref[n2], bb.astype(bf16), preferred_element_type=f32)
        c_ref[pl.ds(n2, k1n, stride=FFT_N2), :] = o
        return c
    lax.fori_loop(0, FFT_N2, inv2, 0)

    skip = skip_ref[0].astype(f32)

    def fin(i, c):
        r = pl.multiple_of(i * ROW_CHUNK, ROW_CHUNK)
        y = src_ref[pl.ds(r, ROW_CHUNK), :]
        y = gate_ref[pl.ds(r, ROW_CHUNK), :] * (c_ref[pl.ds(r, ROW_CHUNK), :] + y * skip)
        if last:
            z = z_ref[0, pl.ds(r, ROW_CHUNK), :].astype(f32)
            y = y * (z * jax.nn.sigmoid(z))
        o_ref[0, pl.ds(r, ROW_CHUNK), :] = y.astype(o_ref.dtype)
        return c
    lax.fori_loop(0, l // ROW_CHUNK, fin, 0)


def hyena_order(yin, px, hr, hi, conv_w, skip, order, first, last):
    b, l, _ = px.shape
    n = 2 * l
    n1 = n // FFT_N2
    k1n = l // FFT_N2
    sa, sb = n1 + STRIDE_PAD, FFT_N2 + STRIDE_PAD
    nh = HY_WIDTH // HY_CH
    f1, fblk, fiblk, hinv = [jnp.asarray(a, bf16) for a in _fft_consts(n, k1n, k1n)]
    cb = lambda col: col // HY_CH
    c_v, c_g, c_z = cb(C_HY_VX), cb(C_HY_VX + (order + 1) * HY_WIDTH), cb(C_HY_Z)
    once = pl.Buffered(1)
    yin_spec = (pl.BlockSpec((1, l, HY_CH), lambda h, bi: (bi, 0, c_v + h)) if first
                else pl.BlockSpec((1, l, HY_CH), lambda h, bi: (bi, 0, h)))
    return pl.pallas_call(
        functools.partial(_hy_order_kernel, l=l, first=first, last=last),
        grid=(nh, b),
        in_specs=[yin_spec,
                  pl.BlockSpec((1, l, HY_CH), lambda h, bi: (bi, 0, c_g + h)),
                  pl.BlockSpec((1, l, HY_CH), lambda h, bi: (bi, 0, c_z + h)),
                  pl.BlockSpec((3, HY_CH), lambda h, bi: (0, h)),
                  pl.BlockSpec((3, HY_CH), lambda h, bi: (0, (order + 1) * nh + h)),
                  pl.BlockSpec((1, 1, HY_CH), lambda h, bi: (order, 0, h)),
                  pl.BlockSpec((n, HY_CH), lambda h, bi: (0, order * nh + h), pipeline_mode=once),
                  pl.BlockSpec((n, HY_CH), lambda h, bi: (0, order * nh + h), pipeline_mode=once),
                  pl.BlockSpec(f1.shape, lambda h, bi: (0, 0, 0), pipeline_mode=once),
                  pl.BlockSpec(fblk.shape, lambda h, bi: (0, 0), pipeline_mode=once),
                  pl.BlockSpec(fiblk.shape, lambda h, bi: (0, 0), pipeline_mode=once),
                  pl.BlockSpec(hinv.shape, lambda h, bi: (0, 0, 0), pipeline_mode=once)],
        out_specs=pl.BlockSpec((1, l, HY_CH), lambda h, bi: (bi, 0, h)),
        out_shape=jax.ShapeDtypeStruct((b, l, HY_WIDTH), bf16 if last else f32),
        scratch_shapes=[pltpu.VMEM((l + 16, HY_CH), f32), pltpu.VMEM((l, HY_CH), f32), pltpu.VMEM((l, HY_CH), f32),
                        pltpu.VMEM((l, HY_CH), f32),
                        pltpu.VMEM((FFT_N2 * sa, HY_CH), f32), pltpu.VMEM((FFT_N2 * sa, HY_CH), f32),
                        pltpu.VMEM((n1 * sb, HY_CH), f32), pltpu.VMEM((n1 * sb, HY_CH), f32)],
        compiler_params=pltpu.CompilerParams(dimension_semantics=("arbitrary", "arbitrary"),
                                             vmem_limit_bytes=56 * 1024 * 1024),
        name=f"hyena_order{order}",
    )(yin, px, px, conv_w, conv_w, skip.reshape(HY_ORDER, 1, HY_WIDTH), hr, hi, f1, fblk, fiblk, hinv)


def hyena_latent(px, conv_w, skip, w1, b1, w2, b2, w3, b3, freq, wout):
    l = px.shape[1]
    kern = hyena_filter_circular(l, w1, b1, w2, b2, w3, b3, freq, wout)
    hr, hi = filter_spectrum(kern)
    y1 = hyena_order(px, px, hr, hi, conv_w, skip, 0, True, False)
    return hyena_order(y1, px, hr, hi, conv_w, skip, 1, False, True)


@functools.lru_cache(maxsize=None)
def _dense_dft_consts(l):
    n = 2 * l
    k = np.arange(n)
    ang = -2.0 * np.pi * np.outer(k, np.arange(n)) / n
    fwd = np.concatenate([np.cos(ang), np.sin(ang)], axis=0)
    angi = 2.0 * np.pi * np.outer(np.arange(l), k) / n
    inv = np.concatenate([np.cos(angi), -np.sin(angi)], axis=1) / n
    return fwd, inv


def _hy_ctx_kernel(p_ref, kern_ref, cw_ref, skip_ref, fwd_ref, inv_ref, o_ref, pad_ref, t_ref, *, l):
    n = 2 * l
    dot = functools.partial(jnp.dot, preferred_element_type=f32, precision=HIGHEST)
    w = HY_WIDTH

    def conv3(col):
        pad_ref[0:8, :] = jnp.zeros((8, w), f32)
        pad_ref[l + 8:l + 16, :] = jnp.zeros((8, w), f32)
        pad_ref[8:8 + l, :] = p_ref[0, :, col:col + w].astype(f32)
        cw = cw_ref[:, col - C_HY_VX:col - C_HY_VX + w].astype(f32)
        return cw[0:1] * pad_ref[7:7 + l, :] + cw[1:2] * pad_ref[8:8 + l, :] + cw[2:3] * pad_ref[9:9 + l, :]

    y = conv3(C_HY_VX)
    for o in range(HY_ORDER):
        gate = conv3(C_HY_VX + (o + 1) * w)
        hh = dot(fwd_ref[...], kern_ref[:, o * w:(o + 1) * w])
        xx = dot(fwd_ref[:, :l], y)
        xr, xi, hr, hi = xx[:n], xx[n:], hh[:n], hh[n:]
        t_ref[0:n, :] = xr * hr - xi * hi
        t_ref[n:2 * n, :] = xr * hi + xi * hr
        conv = dot(inv_ref[...], t_ref[...])
        y = gate * (conv + y * skip_ref[o:o + 1, :].astype(f32))
    z = p_ref[0, :, C_HY_Z:C_HY_Z + w].astype(f32)
    o_ref[0] = (y * (z * jax.nn.sigmoid(z))).astype(o_ref.dtype)


def hyena_ctx(pc, conv_w, skip, w1, b1, w2, b2, w3, b3, freq, wout):
    b, lc, _ = pc.shape
    n = 2 * lc
    kern = hyena_filter_circular(lc, w1, b1, w2, b2, w3, b3, freq, wout)
    fwd, inv = [jnp.asarray(a, f32) for a in _dense_dft_consts(lc)]
    full = lambda a: pl.BlockSpec(a.shape, lambda bi: (0,) * a.ndim)
    return pl.pallas_call(
        functools.partial(_hy_ctx_kernel, l=lc),
        grid=(b,),
        in_specs=[pl.BlockSpec((1, lc, MAIN_W), lambda bi: (bi, 0, 0)), full(kern), full(conv_w), full(skip),
                  full(fwd), full(inv)],
        out_specs=pl.BlockSpec((1, lc, HY_WIDTH), lambda bi: (bi, 0, 0)),
        out_shape=jax.ShapeDtypeStruct((b, lc, HY_WIDTH), bf16),
        scratch_shapes=[pltpu.VMEM((lc + 16, HY_WIDTH), f32), pltpu.VMEM((2 * n, HY_WIDTH), f32)],
        compiler_params=pltpu.CompilerParams(dimension_semantics=("arbitrary",), vmem_limit_bytes=VMEM_LIMIT),
        name="hyena_ctx",
    )(pc, kern, conv_w, skip, fwd, inv)


def rms_norm(x, g):
    xf = x.astype(f32)
    y = xf * lax.rsqrt(jnp.mean(jnp.square(xf), axis=-1, keepdims=True) + NORM_EPS)
    return (y * g.astype(f32)).astype(x.dtype)


def l2_normalize(x):
    xf = x.astype(f32)
    return xf * lax.rsqrt(jnp.sum(jnp.square(xf), axis=-1, keepdims=True) + NORM_EPS)


def dwconv_centred(x, w):
    k, ch = w.shape
    return lax.conv_general_dilated(x, w.astype(x.dtype)[:, None, :], (1,), [(k // 2, k // 2)],
                                    dimension_numbers=('NWC', 'WIO', 'NWC'), feature_group_count=ch)


def heads(t, n_heads, head_dim):
    return t.reshape(t.shape[0], t.shape[1], n_heads, head_dim)


def axial_rope(x):
    l, dh = x.shape[1], x.shape[-1]
    half = dh // 2
    nf = half // 2
    t = jnp.arange(l)
    inv = ROPE_BASE ** (-jnp.arange(nf, dtype=f32) / nf)
    xf = x.astype(f32)

    def rot(xp, pos):
        ang = pos.astype(f32)[:, None] * inv[None, :]
        cos = jnp.cos(ang)[None, :, None, :]
        sin = jnp.sin(ang)[None, :, None, :]
        a, b = xp[..., :nf], xp[..., nf:]
        return jnp.concatenate([a * cos - b * sin, b * cos + a * sin], axis=-1)

    out = jnp.concatenate([rot(xf[..., :half], t // GRID_W), rot(xf[..., half:], t % GRID_W)], axis=-1)
    return out.astype(x.dtype)


def na_latent(q, k, v, k_ctx, v_ctx, rpb):
    b, l, h, dh = q.shape
    rows = l // GRID_W
    kr = min(NA_WIN_ROWS, rows)
    kc = NA_WIN_COLS
    r = jnp.arange(rows)
    col = jnp.arange(GRID_W)
    band = jnp.clip(r - kr // 2, 0, rows - kr)[:, None] + jnp.arange(kr)[None, :]
    c0 = jnp.clip(col - kc // 2, 0, GRID_W - kc)
    in_win = (col[None, :] >= c0[:, None]) & (col[None, :] < c0[:, None] + kc)
    qg = q.reshape(b, rows, GRID_W, h, dh)
    kg = k.reshape(b, rows, GRID_W, h, dh)[:, band]
    vg = v.reshape(b, rows, GRID_W, h, dh)[:, band]
    scale = dh ** -0.5
    s_loc = jnp.einsum('brqhd,brkwhd->brhqkw', qg, kg).astype(f32) * scale
    dr = band - r[:, None] + (NA_WIN_ROWS - 1)
    dc = jnp.clip(col[None, :] - col[:, None], -(kc - 1), kc - 1) + (kc - 1)
    bias = rpb.astype(f32)[:, dr][:, :, :, dc]
    bias = jnp.transpose(bias, (1, 0, 3, 2, 4))
    s_loc = jnp.where(in_win[:, None, :], s_loc + bias, -jnp.inf)
    s_ctx = jnp.einsum('brqhd,bchd->brhqc', qg, k_ctx).astype(f32) * scale
    n_loc = kr * GRID_W
    s = jnp.concatenate([s_loc.reshape(b, rows, h, GRID_W, n_loc), s_ctx], axis=-1)
    p = jax.nn.softmax(s, axis=-1).astype(v.dtype)
    p_loc = p[..., :n_loc].reshape(b, rows, h, GRID_W, kr, GRID_W)
    o = (jnp.einsum('brhqkw,brkwhd->brqhd', p_loc, vg)
         + jnp.einsum('brhqc,bchd->brqhd', p[..., n_loc:], v_ctx))
    return o.reshape(b, l, h * dh)


def dense_ctx_attention(q, k, v):
    b, l, h, dh = q.shape
    s = jnp.einsum('bqhd,bkhd->bhqk', q, k).astype(f32) * dh ** -0.5
    p = jax.nn.softmax(s, axis=-1).astype(v.dtype)
    return jnp.einsum('bhqk,bkhd->bqhd', p, v).reshape(b, l, h * dh)


def dn_inputs(qkv_raw, b_raw, a_raw, conv_w, a_log, dt_bias, rope):
    bsz, l, _ = qkv_raw.shape
    qkv = jax.nn.silu(dwconv_centred(qkv_raw, conv_w))
    q, k, v = [heads(t, DN_HEADS, DN_HEAD_DIM) for t in jnp.split(qkv, 3, axis=-1)]
    if rope:
        q, k = axial_rope(q), axial_rope(k)
    q = l2_normalize(q) * (DN_HEAD_DIM ** -0.5)
    k = l2_normalize(k)
    v = v.astype(f32)
    beta = jax.nn.sigmoid(b_raw.astype(f32)).reshape(bsz, l, 2, DN_HEADS)
    g = -jnp.exp(a_log.astype(f32)) * jax.nn.softplus(
        a_raw.astype(f32).reshape(bsz, l, 2, DN_HEADS) + dt_bias.astype(f32))
    beta = jnp.transpose(beta, (2, 0, 3, 1))
    g = jnp.transpose(g, (2, 0, 3, 1))
    to_bhl = lambda t: jnp.transpose(t, (0, 2, 1, 3))
    return to_bhl(q), to_bhl(k), to_bhl(v), beta, g


def gated_delta_chunked(q, k, v, beta, g, s0):
    b, h, l, _ = q.shape
    dv = v.shape[-1]
    n = l // DN_CHUNK
    q, k, v = [t.reshape(b, h, n, DN_CHUNK, t.shape[-1]) for t in (q, k, v)]
    beta = beta.reshape(b, h, n, DN_CHUNK)
    gc = jnp.cumsum(g.reshape(b, h, n, DN_CHUNK), axis=-1)
    pos = jnp.arange(DN_CHUNK)
    tri = pos[:, None] >= pos[None, :]
    strict = pos[:, None] > pos[None, :]
    diff = gc[..., :, None] - gc[..., None, :]
    decay = jnp.where(tri, jnp.exp(jnp.where(tri, diff, 0.0)), 0.0)
    kb = k * beta[..., None]
    a_kk = jnp.where(strict, jnp.einsum('bhnid,bhnjd->bhnij', kb, k) * decay, 0.0)
    rhs = jnp.concatenate([v * beta[..., None], kb * jnp.exp(gc)[..., None]], axis=-1)
    sol = lax.linalg.triangular_solve(a_kk + jnp.eye(DN_CHUNK, dtype=f32), rhs,
                                      left_side=True, lower=True, unit_diagonal=True)
    u, w = sol[..., :dv], sol[..., dv:]
    a_qk = jnp.where(tri, jnp.einsum('bhnid,bhnjd->bhnij', q, k) * decay, 0.0)

    def step(s, xs):
        q_i, k_i, u_i, w_i, g_i, aqk_i = xs
        v_new = u_i - jnp.einsum('bhcd,bhde->bhce', w_i, s)
        o_i = (jnp.einsum('bhcd,bhde->bhce', q_i * jnp.exp(g_i)[..., None], s)
               + jnp.einsum('bhij,bhje->bhie', aqk_i, v_new))
        g_last = g_i[..., -1]
        s = (s * jnp.exp(g_last)[..., None, None]
             + jnp.einsum('bhcd,bhce->bhde', k_i * jnp.exp(g_last[..., None] - g_i)[..., None], v_new))
        return s, o_i

    xs = tuple(jnp.moveaxis(t, 2, 0) for t in (q, k, u, w, gc, a_qk))
    s_fin, o = lax.scan(step, s0, xs)
    return jnp.moveaxis(o, 0, 2).reshape(b, h, l, dv), s_fin


def dn_bidirectional(q, k, v, beta, g, s0):
    flip = lambda t: jnp.flip(t, axis=2)
    o_f, s_f = gated_delta_chunked(q, k, v, beta[0], g[0], s0[0])
    o_b, s_b = gated_delta_chunked(flip(q), flip(k), flip(v), flip(beta[1]), flip(g[1]), s0[1])
    return o_f + flip(o_b), jnp.stack([s_f, s_b])


def dn_output(o, z, norm_w):
    b, h, l, dv = o.shape
    o = jnp.transpose(o, (0, 2, 1, 3))
    y = rms_norm(o, norm_w) * jax.nn.silu(z.reshape(b, l, h, dv).astype(f32))
    return y.reshape(b, l, h * dv).astype(z.dtype)


def hyena_filters(l, w1, b1, w2, b2, w3, b3, freq, wout):
    t = jnp.linspace(0.0, 1.0, l, dtype=f32)[:, None]
    bands = (HY_EMB - 1) // 2
    fr = jnp.linspace(1e-4, bands - 1, bands, dtype=f32)
    ang = (2.0 * math.pi / l) * jnp.arange(l, dtype=f32)[:, None] * fr[None, :]
    z = jnp.concatenate([t, jnp.cos(ang), -jnp.sin(ang)], axis=-1)
    freq = freq.astype(f32)
    hid = jnp.sin(freq[0] * (z @ w1.astype(f32) + b1.astype(f32)))
    hid = jnp.sin(freq[1] * (hid @ w2.astype(f32) + b2.astype(f32)))
    hid = jnp.sin(freq[2] * (hid @ w3.astype(f32) + b3.astype(f32)))
    filt = (hid @ wout.astype(f32)).reshape(l, HY_ORDER, 2, HY_WIDTH)
    max_decay = math.log(HY_DECAY_TARGET) / HY_FAST_DECAY
    min_decay = math.log(HY_DECAY_TARGET) / HY_SLOW_DECAY
    deltas = jnp.abs(jnp.linspace(min_decay, max_decay, HY_WIDTH, dtype=f32))
    window = jnp.exp(-t * deltas[None, :])
    return filt * window[:, None, None, :]


def bidir_long_conv(u, h_fwd, h_bwd):
    b, l, ch = u.shape
    n = 2 * l
    kern = jnp.concatenate([h_fwd, jnp.zeros((1, ch), f32), h_bwd[:0:-1]], axis=0)
    y = jnp.fft.irfft(jnp.fft.rfft(u.astype(f32), n=n, axis=1) * jnp.fft.rfft(kern, n=n, axis=0)[None],
                      n=n, axis=1)
    return y[:, :l]


def hyena_mix(vx_raw, z, conv_w, filt, skip):
    vx = dwconv_centred(vx_raw, conv_w).astype(f32)
    v, x1, x2 = jnp.split(vx, 3, axis=-1)
    y = v
    for n, gate in enumerate((x1, x2)):
        y = gate * (bidir_long_conv(y, filt[:, n, 0], filt[:, n, 1]) + y * skip[n].astype(f32))
    return (y * jax.nn.silu(z.astype(f32))).astype(z.dtype)


def _pack_w_in(w_in):
    depth, d, _ = w_in.shape
    main = jnp.concatenate([w_in[:, :, :3072], w_in[:, :, 3088:4112]], axis=-1).astype(bf16)
    gate = jnp.concatenate([w_in[:, :, 3072:3088], jnp.zeros((depth, d, GATE_W - 16), w_in.dtype)],
                           axis=-1).astype(bf16)
    return main, gate


def kernel(x, c, ctx, c_ctx, w_ada, b_ada, g_pre, g_post, w_in, w_out, na_rpb, dn_conv, dn_a_log, dn_dt_bias,
           dn_norm, hy_conv, hy_w1, hy_b1, hy_w2, hy_b2, hy_w3, hy_b3, hy_freq, hy_wout, hy_skip):
    bsz, l, d = x.shape
    lc = ctx.shape[1]
    depth = w_in.shape[0]
    cond = jnp.concatenate([c, c_ctx[None], jnp.zeros((7, d), f32)], axis=0)
    mod = modulation_all(cond, w_ada, b_ada)
    w_main, w_gate = _pack_w_in(w_in)
    w_out_b = w_out.astype(bf16)

    for i in range(depth):
        last = i == depth - 1
        shift_x, scale_x, gate_x = [mod[i, :bsz, j * d:(j + 1) * d][:, None] for j in range(3)]
        shift_c, scale_c, gate_c = [mod[i, bsz:bsz + 1, j * d:(j + 1) * d][:, None] for j in range(3)]
        px, gx = in_projection(x, g_pre[i], scale_x, shift_x, w_main[i], w_gate[i], 512)
        pc, gc = in_projection(ctx, g_pre[i], scale_c, shift_c, w_main[i], w_gate[i], 256)

        sl = lambda t, a, n: t[..., a:a + n].astype(f32)
        dn_qkv_x, dn_z_x = sl(px, C_DN_QKV, 1536), sl(px, C_DN_Z, 512)
        dn_b_x, dn_a_x = gx[..., :8], gx[..., 8:16]
        dn_qkv_c, dn_z_c = sl(pc, C_DN_QKV, 1536), sl(pc, C_DN_Z, 512)
        dn_b_c, dn_a_c = gc[..., :8], gc[..., 8:16]
        hy_args = (hy_conv[i], hy_skip[i], hy_w1[i], hy_b1[i], hy_w2[i], hy_b2[i], hy_w3[i], hy_b3[i], hy_freq[i],
                   hy_wout[i])

        out_a_x = na_attention(px, pc, na_rpb[i])

        dq_c, dk_c, dv_c, beta_c, g_c = dn_inputs(dn_qkv_c, dn_b_c, dn_a_c, dn_conv[i], dn_a_log[i], dn_dt_bias[i], False)
        s0 = jnp.zeros((2, bsz, DN_HEADS, DN_HEAD_DIM, DN_HEAD_DIM), f32)
        do_c, s_ctx = dn_bidirectional(dq_c, dk_c, dv_c, beta_c, g_c, s0)
        dq_x, dk_x, dv_x, beta_x, g_x = dn_inputs(dn_qkv_x, dn_b_x, dn_a_x, dn_conv[i], dn_a_log[i], dn_dt_bias[i], True)
        do_x, _ = dn_bidirectional(dq_x, dk_x, dv_x, beta_x, g_x, s_ctx)
        out_b_x = dn_output(do_x, dn_z_x, dn_norm[i])

        out_c_x = hyena_latent(px, *hy_args)

        mix_x = jnp.concatenate([out_a_x, out_b_x.astype(bf16), out_c_x], axis=-1)
        new_x = out_projection(mix_x, w_out_b[i], g_post[i], gate_x, x, 512)

        if not last:
            out_a_c = ctx_attention(pc)
            out_b_c = dn_output(do_c, dn_z_c, dn_norm[i])
            out_c_c = hyena_ctx(pc, *hy_args)
            mix_c = jnp.concatenate([out_a_c, out_b_c.astype(bf16), out_c_c], axis=-1)
            ctx = out_projection(mix_c, w_out_b[i], g_post[i], gate_c, ctx, 256)
        x = new_x
    return x
```

```python
import functools
import math

import numpy as np
import jax
import jax.numpy as jnp
from jax import lax
from jax.experimental import pallas as pl
from jax.experimental.pallas import tpu as pltpu

D_MODEL = 1024
GRID_W = 64
NA_HEAD_DIM = 64
NA_WIDTH = 256
NA_HEADS = 4
NA_WIN_ROWS = 8
NA_WIN_COLS = 16
DN_HEAD_DIM = 128
DN_WIDTH = 512
DN_HEADS = 4
DN_CONV_W = 5
DN_CHUNK = 64
HY_WIDTH = 256
HY_ORDER = 2
HY_EMB = 33
HY_DECAY_TARGET = 1e-2
HY_FAST_DECAY = 0.3
HY_SLOW_DECAY = 1.5
ROPE_BASE = 10000.0
NORM_EPS = 1e-6

C_NA_QKV, C_NA_Z, C_DN_QKV, C_DN_Z, C_HY_VX, C_HY_Z = 0, 768, 1024, 2560, 3072, 3840
MAIN_W = 4096
GATE_W = 128
LANE = 128
VMEM_LIMIT = 52 * 1024 * 1024

f32 = jnp.float32
bf16 = jnp.bfloat16
HIGHEST = lax.Precision.HIGHEST
NT_DIMS = (((1,), (1,)), ((), ()))
TN_DIMS = (((0,), (0,)), ((), ()))


def _mod_kernel(c_ref, w_ref, b_ref, o_ref):
    c = c_ref[...]
    a = c * jax.nn.sigmoid(c)
    o_ref[0] = jnp.dot(a, w_ref[0], preferred_element_type=f32, precision=lax.Precision.HIGHEST) + b_ref[0]


def modulation_all(cond, w_ada, b_ada):
    depth, d, d3 = w_ada.shape
    r = cond.shape[0]
    tn = 512
    return pl.pallas_call(
        _mod_kernel,
        grid=(depth, d3 // tn),
        in_specs=[pl.BlockSpec((r, d), lambda i, j: (0, 0)),
                  pl.BlockSpec((1, d, tn), lambda i, j: (i, 0, j)),
                  pl.BlockSpec((1, 1, tn), lambda i, j: (i, 0, j))],
        out_specs=pl.BlockSpec((1, r, tn), lambda i, j: (i, 0, j)),
        out_shape=jax.ShapeDtypeStruct((depth, r, d3), f32),
        compiler_params=pltpu.CompilerParams(dimension_semantics=("arbitrary", "arbitrary"),
                                             vmem_limit_bytes=VMEM_LIMIT),
        name="modulation",
    )(cond, w_ada, b_ada.reshape(depth, 1, d3))


def _inproj_kernel(x_ref, g_ref, sc_ref, sh_ref, w_ref, wg_ref, o_ref, og_ref):
    x = x_ref[0]
    ms = jnp.mean(x * x, axis=-1, keepdims=True)
    h = x * lax.rsqrt(ms + NORM_EPS) * g_ref[...]
    h = (h * (1.0 + sc_ref[0]) + sh_ref[0]).astype(bf16)
    for j in range(MAIN_W // 1024):
        o_ref[0, :, j * 1024:(j + 1) * 1024] = jnp.dot(
            h, w_ref[:, j * 1024:(j + 1) * 1024], preferred_element_type=f32).astype(bf16)
    og_ref[0] = jnp.dot(h, wg_ref[...], preferred_element_type=f32)


def in_projection(x, g, scale, shift, w_main, w_gate, tm):
    b, l, d = x.shape
    per_b = scale.shape[0] == b
    mod_map = (lambda bi, i: (bi, 0, 0)) if per_b else (lambda bi, i: (0, 0, 0))
    return pl.pallas_call(
        _inproj_kernel,
        grid=(b, l // tm),
        in_specs=[pl.BlockSpec((1, tm, d), lambda bi, i: (bi, i, 0)),
                  pl.BlockSpec((1, d), lambda bi, i: (0, 0)),
                  pl.BlockSpec((1, 1, d), mod_map),
                  pl.BlockSpec((1, 1, d), mod_map),
                  pl.BlockSpec((d, MAIN_W), lambda bi, i: (0, 0)),
                  pl.BlockSpec((d, GATE_W), lambda bi, i: (0, 0))],
        out_specs=[pl.BlockSpec((1, tm, MAIN_W), lambda bi, i: (bi, i, 0)),
                   pl.BlockSpec((1, tm, GATE_W), lambda bi, i: (bi, i, 0))],
        out_shape=[jax.ShapeDtypeStruct((b, l, MAIN_W), bf16),
                   jax.ShapeDtypeStruct((b, l, GATE_W), f32)],
        compiler_params=pltpu.CompilerParams(dimension_semantics=("arbitrary", "arbitrary"),
                                             vmem_limit_bytes=VMEM_LIMIT),
        name="in_projection",
    )(x, g.reshape(1, d), scale, shift, w_main, w_gate)


def _outproj_kernel(a_ref, b_ref, c_ref, w_ref, g_ref, gate_ref, x_ref, o_ref):
    y = (jnp.dot(a_ref[0], w_ref[0:NA_WIDTH], preferred_element_type=f32)
         + jnp.dot(b_ref[0], w_ref[NA_WIDTH:NA_WIDTH + DN_WIDTH], preferred_element_type=f32)
         + jnp.dot(c_ref[0], w_ref[NA_WIDTH + DN_WIDTH:], preferred_element_type=f32))
    ms = jnp.mean(y * y, axis=-1, keepdims=True)
    yn = y * lax.rsqrt(ms + NORM_EPS) * g_ref[...]
    o_ref[0] = x_ref[0] + gate_ref[0] * yn


def out_projection(out_a, out_b, out_c, w_out, g_post, gate, x, tm):
    b, l, d = x.shape
    per_b = gate.shape[0] == b
    mod_map = (lambda bi, i: (bi, 0, 0)) if per_b else (lambda bi, i: (0, 0, 0))
    part = lambda a: pl.BlockSpec((1, tm, a.shape[-1]), lambda bi, i: (bi, i, 0))
    return pl.pallas_call(
        _outproj_kernel,
        grid=(b, l // tm),
        in_specs=[part(out_a), part(out_b), part(out_c),
                  pl.BlockSpec((d, d), lambda bi, i: (0, 0)),
                  pl.BlockSpec((1, d), lambda bi, i: (0, 0)),
                  pl.BlockSpec((1, 1, d), mod_map),
                  pl.BlockSpec((1, tm, d), lambda bi, i: (bi, i, 0))],
        out_specs=pl.BlockSpec((1, tm, d), lambda bi, i: (bi, i, 0)),
        out_shape=jax.ShapeDtypeStruct((b, l, d), f32),
        compiler_params=pltpu.CompilerParams(dimension_semantics=("arbitrary", "arbitrary"),
                                             vmem_limit_bytes=VMEM_LIMIT),
        name="out_projection",
    )(out_a, out_b, out_c, w_out, g_post.reshape(1, d), gate, x)


NA_TQ_ROWS = 4
NA_TQ = NA_TQ_ROWS * GRID_W
NA_KV_ROWS = NA_TQ_ROWS + NA_WIN_ROWS
NA_KV = NA_KV_ROWS * GRID_W
MASK_VALUE = -1e30


def _na_window_start(t, rows):
    return jnp.clip(t * NA_TQ_ROWS - NA_WIN_ROWS // 2, 0, rows - NA_KV_ROWS)


def _na_bias_kernel(rpb_ref, sel_ref, mask_ref, o_ref):
    o_ref[...] = jnp.dot(rpb_ref[...], sel_ref[...], preferred_element_type=f32, precision=HIGHEST) + mask_ref[...]


def na_bias_tables(rpb, rows):
    nh, ndr, ndc = rpb.shape
    col = np.arange(GRID_W)
    dc = np.clip(col[None, :] - col[:, None], -(NA_WIN_COLS - 1), NA_WIN_COLS - 1) + (NA_WIN_COLS - 1)
    c0 = np.clip(col - NA_WIN_COLS // 2, 0, GRID_W - NA_WIN_COLS)
    in_win = (col[None, :] >= c0[:, None]) & (col[None, :] < c0[:, None] + NA_WIN_COLS)
    sel = np.zeros((LANE, GRID_W * GRID_W), np.float32)
    sel[dc.reshape(-1), np.arange(GRID_W * GRID_W)] = 1.0
    cmask = np.where(in_win, 0.0, MASK_VALUE).astype(np.float32).reshape(1, -1)
    rpb2 = jnp.pad(rpb.astype(f32).reshape(nh * ndr, ndc), ((0, 0), (0, LANE - ndc)))
    tab = pl.pallas_call(
        _na_bias_kernel,
        out_shape=jax.ShapeDtypeStruct((nh * ndr, GRID_W * GRID_W), f32),
        name="na_bias",
    )(rpb2, jnp.asarray(sel), jnp.asarray(cmask)).reshape(nh, ndr, GRID_W, GRID_W)
    masked = jnp.full((nh, GRID_W, GRID_W), MASK_VALUE, f32)
    n_tiles = rows // NA_TQ_ROWS
    tabs = []
    for t in (0, 1, n_tiles - 1):
        r0 = t * NA_TQ_ROWS
        ws = int(np.clip(r0 - NA_WIN_ROWS // 2, 0, rows - NA_KV_ROWS))
        blocks = []
        for qr in range(r0, r0 + NA_TQ_ROWS):
            band0 = int(np.clip(qr - NA_WIN_ROWS // 2, 0, rows - NA_WIN_ROWS))
            blocks.append(jnp.concatenate(
                [tab[:, kr - qr + NA_WIN_ROWS - 1] if band0 <= kr < band0 + NA_WIN_ROWS else masked
                 for kr in range(ws, ws + NA_KV_ROWS)], axis=-1))
        tabs.append(jnp.concatenate(blocks, axis=-2))
    return jnp.stack(tabs)


def _na_kernel(q_ref, k_ref, v_ref, z_ref, kc_ref, vc_ref, bias_ref, o_ref, *, rows):
    t = pl.program_id(1)
    ws = pl.multiple_of(_na_window_start(t, rows) * GRID_W, GRID_W)
    q = q_ref[0]
    kw = k_ref[0, pl.ds(ws, NA_KV), :]
    vw = v_ref[0, pl.ds(ws, NA_KV), :]
    kc = kc_ref[0]
    vc = vc_ref[0]
    outs = []
    for h in range(NA_HEADS):
        sl = slice(h * NA_HEAD_DIM, (h + 1) * NA_HEAD_DIM)
        qh = q[:, sl] * (NA_HEAD_DIM ** -0.5)
        s1 = lax.dot_general(qh, kw[:, sl], NT_DIMS, preferred_element_type=f32) + bias_ref[0, h]
        s2 = lax.dot_general(qh, kc[:, sl], NT_DIMS, preferred_element_type=f32)
        m = jnp.maximum(jnp.max(s1, axis=-1, keepdims=True), jnp.max(s2, axis=-1, keepdims=True))
        p1 = jnp.exp(s1 - m)
        p2 = jnp.exp(s2 - m)
        den = jnp.sum(p1, axis=-1, keepdims=True) + jnp.sum(p2, axis=-1, keepdims=True)
        o = (jnp.dot(p1.astype(bf16), vw[:, sl], preferred_element_type=f32)
             + jnp.dot(p2.astype(bf16), vc[:, sl], preferred_element_type=f32))
        outs.append(o / den)
    z = z_ref[0].astype(f32)
    o = jnp.concatenate(outs, axis=-1) * (z * jax.nn.sigmoid(z))
    o_ref[0] = o.astype(o_ref.dtype)


def na_attention(px, pc, rpb):
    b, l, _ = px.shape
    lc = pc.shape[1]
    rows = l // GRID_W
    n_tiles = rows // NA_TQ_ROWS
    bias = na_bias_tables(rpb, rows)
    w = NA_WIDTH
    cq, ck, cv, cz = [(C_NA_QKV + j * w) // w for j in range(3)] + [C_NA_Z // w]

    def bias_map(bi, t):
        return (jnp.where(t == 0, 0, jnp.where(t == n_tiles - 1, 2, 1)), 0, 0, 0)

    return pl.pallas_call(
        functools.partial(_na_kernel, rows=rows),
        grid=(b, n_tiles),
        in_specs=[pl.BlockSpec((1, NA_TQ, w), lambda bi, t: (bi, t, cq)),
                  pl.BlockSpec((1, l, w), lambda bi, t: (bi, 0, ck)),
                  pl.BlockSpec((1, l, w), lambda bi, t: (bi, 0, cv)),
                  pl.BlockSpec((1, NA_TQ, w), lambda bi, t: (bi, t, cz)),
                  pl.BlockSpec((1, lc, w), lambda bi, t: (bi, 0, ck)),
                  pl.BlockSpec((1, lc, w), lambda bi, t: (bi, 0, cv)),
                  pl.BlockSpec((1, NA_HEADS, NA_TQ, NA_KV), bias_map)],
        out_specs=pl.BlockSpec((1, NA_TQ, w), lambda bi, t: (bi, t, 0)),
        out_shape=jax.ShapeDtypeStruct((b, l, w), bf16),
        compiler_params=pltpu.CompilerParams(dimension_semantics=("arbitrary", "arbitrary"),
                                             vmem_limit_bytes=VMEM_LIMIT),
        name="na_attention",
    )(px, px, px, px, pc, pc, bias)


def _ctx_attn_kernel(q_ref, k_ref, v_ref, z_ref, o_ref):
    q = q_ref[0]
    k = k_ref[0]
    v = v_ref[0]
    outs = []
    for h in range(NA_HEADS):
        sl = slice(h * NA_HEAD_DIM, (h + 1) * NA_HEAD_DIM)
        qh = q[:, sl] * (NA_HEAD_DIM ** -0.5)
        s = lax.dot_general(qh, k[:, sl], NT_DIMS, preferred_element_type=f32)
        p = jnp.exp(s - jnp.max(s, axis=-1, keepdims=True))
        den = jnp.sum(p, axis=-1, keepdims=True)
        outs.append(jnp.dot(p.astype(bf16), v[:, sl], preferred_element_type=f32) / den)
    z = z_ref[0].astype(f32)
    o_ref[0] = (jnp.concatenate(outs, axis=-1) * (z * jax.nn.sigmoid(z))).astype(o_ref.dtype)


def ctx_attention(pc):
    b, lc, _ = pc.shape
    w = NA_WIDTH
    cq, ck, cv, cz = [(C_NA_QKV + j * w) // w for j in range(3)] + [C_NA_Z // w]
    spec = lambda cidx: pl.BlockSpec((1, lc, w), lambda bi: (bi, 0, cidx))
    return pl.pallas_call(
        _ctx_attn_kernel,
        grid=(b,),
        in_specs=[spec(cq), spec(ck), spec(cv), spec(cz)],
        out_specs=pl.BlockSpec((1, lc, w), lambda bi: (bi, 0, 0)),
        out_shape=jax.ShapeDtypeStruct((b, lc, w), bf16),
        compiler_params=pltpu.CompilerParams(dimension_semantics=("arbitrary",), vmem_limit_bytes=VMEM_LIMIT),
        name="ctx_attention",
    )(pc, pc, pc, pc)


HY_CH = 128
FFT_N2 = 128
ROW_CHUNK = 256
STRIDE_PAD = 8
HY_FILT_ROWS = 512
HY_BANDS = (HY_EMB - 1) // 2


def _hy_filter_kernel(fr_ref, w1t_ref, w1c_ref, w1s_ref, b1_ref, w2_ref, b2_ref, w3_ref, b3_ref, freq_ref, wo_ref,
                      dl_ref, o_ref, *, l):
    i = pl.program_id(0)
    n = i * HY_FILT_ROWS + lax.broadcasted_iota(jnp.int32, (HY_FILT_ROWS, 1), 0)
    lag = jnp.where(n < l, n, 2 * l - n).astype(f32)
    t = lag * (1.0 / (l - 1))
    ang = (lag * (2.0 * math.pi / l)) * fr_ref[...]
    dot = functools.partial(jnp.dot, preferred_element_type=f32, precision=HIGHEST)
    pre = t * w1t_ref[...] + dot(jnp.cos(ang), w1c_ref[...]) - dot(jnp.sin(ang), w1s_ref[...]) + b1_ref[...]
    hid = jnp.sin(freq_ref[0:1] * pre)
    hid = jnp.sin(freq_ref[1:2] * (dot(hid, w2_ref[...]) + b2_ref[...]))
    hid = jnp.sin(freq_ref[2:3] * (dot(hid, w3_ref[...]) + b3_ref[...]))
    filt = dot(hid, wo_ref[...])
    window = jnp.exp(-t * dl_ref[...])
    live = jnp.where(n == l, 0.0, 1.0)
    fwd = n < l
    for o in range(HY_ORDER):
        a = filt[:, (2 * o) * HY_WIDTH:(2 * o + 1) * HY_WIDTH]
        b = filt[:, (2 * o + 1) * HY_WIDTH:(2 * o + 2) * HY_WIDTH]
        o_ref[:, o * HY_WIDTH:(o + 1) * HY_WIDTH] = jnp.where(fwd, a, b) * window * live


def hyena_filter_circular(l, w1, b1, w2, b2, w3, b3, freq, wout):
    n = 2 * l
    assert n % HY_FILT_ROWS == 0
    hid = w1.shape[1]
    fr = np.zeros((1, LANE), np.float32)
    fr[0, :HY_BANDS] = np.linspace(1e-4, HY_BANDS - 1, HY_BANDS, dtype=np.float32)
    padrows = lambda w: jnp.concatenate([w, jnp.zeros((LANE - w.shape[0], hid), f32)], axis=0)
    w1t, w1c, w1s = w1[0:1], padrows(w1[1:1 + HY_BANDS]), padrows(w1[1 + HY_BANDS:])
    max_decay = math.log(HY_DECAY_TARGET) / HY_FAST_DECAY
    min_decay = math.log(HY_DECAY_TARGET) / HY_SLOW_DECAY
    deltas = np.abs(np.linspace(min_decay, max_decay, HY_WIDTH, dtype=np.float32))[None]
    full = lambda a: pl.BlockSpec(a.shape, lambda i: (0,) * a.ndim)
    args = [jnp.asarray(fr), w1t, w1c, w1s, b1[None], w2, b2[None], w3, b3[None], freq, wout, jnp.asarray(deltas)]
    return pl.pallas_call(
        functools.partial(_hy_filter_kernel, l=l),
        grid=(n // HY_FILT_ROWS,),
        in_specs=[full(a) for a in args],
        out_specs=pl.BlockSpec((HY_FILT_ROWS, HY_ORDER * HY_WIDTH), lambda i: (i, 0)),
        out_shape=jax.ShapeDtypeStruct((n, HY_ORDER * HY_WIDTH), f32),
        compiler_params=pltpu.CompilerParams(dimension_semantics=("arbitrary",), vmem_limit_bytes=VMEM_LIMIT),
        name="hyena_filter",
    )(*args)


@functools.lru_cache(maxsize=None)
def _fft_consts(n, k_in, k_out):
    n1 = n // FFT_N2
    k1 = np.arange(n1)
    n2 = np.arange(FFT_N2)
    tt = FFT_N2 * np.arange(k_in)[None, None, :] + n2[:, None, None]
    ang = -2.0 * np.pi * (k1[None, :, None] * tt) / n
    f1 = np.concatenate([np.cos(ang), np.sin(ang)], axis=1)
    a2 = -2.0 * np.pi * np.outer(n2, n2) / FFT_N2
    cr, ci = np.cos(a2), np.sin(a2)
    fblk = np.block([[cr, -ci], [ci, cr]])
    fiblk = np.block([[cr, ci], [-ci, cr]])
    to = FFT_N2 * np.arange(k_out)[None, :, None] + n2[:, None, None]
    ango = 2.0 * np.pi * (k1[None, None, :] * to) / n
    hinv = np.concatenate([np.cos(ango), -np.sin(ango)], axis=2) / n
    return f1, fblk, fiblk, hinv


def _fft_fwd_stage1(x_ref, f1_ref, ar_ref, ai_ref, k_in, n1, sa, dt, prec):
    def body(n2, c):
        xs = x_ref[pl.ds(n2, k_in, stride=FFT_N2), :].astype(dt)
        a = jnp.dot(f1_ref[n2], xs, preferred_element_type=f32, precision=prec)
        off = pl.multiple_of(n2 * sa, 8)
        ar_ref[pl.ds(off, n1), :] = a[:n1]
        ai_ref[pl.ds(off, n1), :] = a[n1:]
        return c
    lax.fori_loop(0, FFT_N2, body, 0)


def _filter_fft_kernel(k_ref, f1_ref, fblk_ref, hr_ref, hi_ref, ar_ref, ai_ref, *, n):
    n1 = n // FFT_N2
    sa = n1 + STRIDE_PAD
    _fft_fwd_stage1(k_ref, f1_ref, ar_ref, ai_ref, n1, n1, sa, f32, HIGHEST)

    def body(k1, c):
        z = jnp.concatenate([ar_ref[pl.ds(k1, FFT_N2, stride=sa), :], ai_ref[pl.ds(k1, FFT_N2, stride=sa), :]], axis=0)
        xx = jnp.dot(fblk_ref[...], z, preferred_element_type=f32, precision=HIGHEST)
        off = pl.multiple_of(k1 * FFT_N2, FFT_N2)
        hr_ref[pl.ds(off, FFT_N2), :] = xx[:FFT_N2]
        hi_ref[pl.ds(off, FFT_N2), :] = xx[FFT_N2:]
        return c
    lax.fori_loop(0, n1, body, 0)


def filter_spectrum(kern):
    n, c = kern.shape
    n1 = n // FFT_N2
    f1, fblk, _, _ = _fft_consts(n, n1, 1)
    f1 = jnp.asarray(f1, f32)
    fblk = jnp.asarray(fblk, f32)
    sa = n1 + STRIDE_PAD
    return pl.pallas_call(
        functools.partial(_filter_fft_kernel, n=n),
        grid=(c // HY_CH,),
        in_specs=[pl.BlockSpec((n, HY_CH), lambda j: (0, j)),
                  pl.BlockSpec(f1.shape, lambda j: (0, 0, 0)),
                  pl.BlockSpec(fblk.shape, lambda j: (0, 0))],
        out_specs=[pl.BlockSpec((n, HY_CH), lambda j: (0, j))] * 2,
        out_shape=[jax.ShapeDtypeStruct((n, c), f32)] * 2,
        scratch_shapes=[pltpu.VMEM((FFT_N2 * sa, HY_CH), f32)] * 2,
        compiler_params=pltpu.CompilerParams(dimension_semantics=("arbitrary",), vmem_limit_bytes=VMEM_LIMIT),
        name="filter_spectrum",
    )(kern, f1, fblk)


def _conv3_rows(src_ref, w_ref, pad_ref, dst_ref, l):
    zeros = jnp.zeros((8, src_ref.shape[-1]), f32)
    pad_ref[0:8, :] = zeros
    pad_ref[l + 8:l + 16, :] = zeros

    def cp(i, c):
        r = pl.multiple_of(i * ROW_CHUNK, ROW_CHUNK)
        pad_ref[pl.ds(8 + r, ROW_CHUNK), :] = src_ref[0, pl.ds(r, ROW_CHUNK), :].astype(f32)
        return c
    lax.fori_loop(0, l // ROW_CHUNK, cp, 0)
    w = w_ref[...].astype(f32)

    def cv(i, c):
        r = pl.multiple_of(i * ROW_CHUNK, ROW_CHUNK)
        blk = pad_ref[pl.ds(r, ROW_CHUNK + 16), :]
        dst_ref[pl.ds(r, ROW_CHUNK), :] = (w[0:1] * blk[7:7 + ROW_CHUNK] + w[1:2] * blk[8:8 + ROW_CHUNK]
                                           + w[2:3] * blk[9:9 + ROW_CHUNK])
        return c
    lax.fori_loop(0, l // ROW_CHUNK, cv, 0)


def _hy_order_kernel(yin_ref, graw_ref, z_ref, cwy_ref, cwg_ref, skip_ref, hr_ref, hi_ref, f1_ref, fblk_ref, fiblk_ref,
                     hinv_ref, o_ref, pad_ref, y_ref, gate_ref, c_ref, ar_ref, ai_ref, br_ref, bi_ref,
                     *, l, first, last):
    n = 2 * l
    n1 = n // FFT_N2
    k1n = l // FFT_N2
    sa = n1 + STRIDE_PAD
    sb = FFT_N2 + STRIDE_PAD
    if first:
        _conv3_rows(yin_ref, cwy_ref, pad_ref, y_ref, l)
        src_ref = y_ref
    else:
        src_ref = yin_ref.at[0]
    _conv3_rows(graw_ref, cwg_ref, pad_ref, gate_ref, l)

    _fft_fwd_stage1(src_ref, f1_ref, ar_ref, ai_ref, k1n, n1, sa, bf16, None)

    def mid(k1, c):
        z = jnp.concatenate([ar_ref[pl.ds(k1, FFT_N2, stride=sa), :], ai_ref[pl.ds(k1, FFT_N2, stride=sa), :]], axis=0)
        xx = jnp.dot(fblk_ref[...], z.astype(bf16), preferred_element_type=f32)
        xr, xi = xx[:FFT_N2], xx[FFT_N2:]
        off = pl.multiple_of(k1 * FFT_N2, FFT_N2)
        hr = hr_ref[pl.ds(off, FFT_N2), :]
        hi = hi_ref[pl.ds(off, FFT_N2), :]
        yy = jnp.concatenate([xr * hr - xi * hi, xr * hi + xi * hr], axis=0).astype(bf16)
        bb = jnp.dot(fiblk_ref[...], yy, preferred_element_type=f32)
        boff = pl.multiple_of(k1 * sb, 8)
        br_ref[pl.ds(boff, FFT_N2), :] = bb[:FFT_N2]
        bi_ref[pl.ds(boff, FFT_N2), :] = bb[FFT_N2:]
        return c
    lax.fori_loop(0, n1, mid, 0)

    def inv2(n2, c):
        bb = jnp.concatenate([br_ref[pl.ds(n2, n1, stride=sb), :], bi_ref[pl.ds(n2, n1, stride=sb), :]], axis=0)
        o = jnp.dot(hinv_ref[n2], bb.astype(bf16), preferred_element_type=f32)
        c_ref[pl.ds(n2, k1n, stride=FFT_N2), :] = o
        return c
    lax.fori_loop(0, FFT_N2, inv2, 0)

    skip = skip_ref[0].astype(f32)

    def fin(i, c):
        r = pl.multiple_of(i * ROW_CHUNK, ROW_CHUNK)
        y = src_ref[pl.ds(r, ROW_CHUNK), :]
        y = gate_ref[pl.ds(r, ROW_CHUNK), :] * (c_ref[pl.ds(r, ROW_CHUNK), :] + y * skip)
        if last:
            z = z_ref[0, pl.ds(r, ROW_CHUNK), :].astype(f32)
            y = y * (z * jax.nn.sigmoid(z))
        o_ref[0, pl.ds(r, ROW_CHUNK), :] = y.astype(o_ref.dtype)
        return c
    lax.fori_loop(0, l // ROW_CHUNK, fin, 0)


def hyena_order(yin, px, hr, hi, conv_w, skip, order, first, last):
    b, l, _ = px.shape
    n = 2 * l
    n1 = n // FFT_N2
    k1n = l // FFT_N2
    sa, sb = n1 + STRIDE_PAD, FFT_N2 + STRIDE_PAD
    nh = HY_WIDTH // HY_CH
    f1, fblk, fiblk, hinv = [jnp.asarray(a, bf16) for a in _fft_consts(n, k1n, k1n)]
    cb = lambda col: col // HY_CH
    c_v, c_g, c_z = cb(C_HY_VX), cb(C_HY_VX + (order + 1) * HY_WIDTH), cb(C_HY_Z)
    once = pl.Buffered(1)
    yin_spec = (pl.BlockSpec((1, l, HY_CH), lambda h, bi: (bi, 0, c_v + h)) if first
                else pl.BlockSpec((1, l, HY_CH), lambda h, bi: (bi, 0, h)))
    return pl.pallas_call(
        functools.partial(_hy_order_kernel, l=l, first=first, last=last),
        grid=(nh, b),
        in_specs=[yin_spec,
                  pl.BlockSpec((1, l, HY_CH), lambda h, bi: (bi, 0, c_g + h)),
                  pl.BlockSpec((1, l, HY_CH), lambda h, bi: (bi, 0, c_z + h)),
                  pl.BlockSpec((3, HY_CH), lambda h, bi: (0, h)),
                  pl.BlockSpec((3, HY_CH), lambda h, bi: (0, (order + 1) * nh + h)),
                  pl.BlockSpec((1, 1, HY_CH), lambda h, bi: (order, 0, h)),
                  pl.BlockSpec((n, HY_CH), lambda h, bi: (0, order * nh + h), pipeline_mode=once),
                  pl.BlockSpec((n, HY_CH), lambda h, bi: (0, order * nh + h), pipeline_mode=once),
                  pl.BlockSpec(f1.shape, lambda h, bi: (0, 0, 0), pipeline_mode=once),
                  pl.BlockSpec(fblk.shape, lambda h, bi: (0, 0), pipeline_mode=once),
                  pl.BlockSpec(fiblk.shape, lambda h, bi: (0, 0), pipeline_mode=once),
                  pl.BlockSpec(hinv.shape, lambda h, bi: (0, 0, 0), pipeline_mode=once)],
        out_specs=pl.BlockSpec((1, l, HY_CH), lambda h, bi: (bi, 0, h)),
        out_shape=jax.ShapeDtypeStruct((b, l, HY_WIDTH), bf16 if last else f32),
        scratch_shapes=[pltpu.VMEM((l + 16, HY_CH), f32), pltpu.VMEM((l, HY_CH), f32), pltpu.VMEM((l, HY_CH), f32),
                        pltpu.VMEM((l, HY_CH), f32),
                        pltpu.VMEM((FFT_N2 * sa, HY_CH), f32), pltpu.VMEM((FFT_N2 * sa, HY_CH), f32),
                        pltpu.VMEM((n1 * sb, HY_CH), f32), pltpu.VMEM((n1 * sb, HY_CH), f32)],
        compiler_params=pltpu.CompilerParams(dimension_semantics=("arbitrary", "arbitrary"),
                                             vmem_limit_bytes=56 * 1024 * 1024),
        name=f"hyena_order{order}",
    )(yin, px, px, conv_w, conv_w, skip.reshape(HY_ORDER, 1, HY_WIDTH), hr, hi, f1, fblk, fiblk, hinv)


def hyena_latent(px, conv_w, skip, w1, b1, w2, b2, w3, b3, freq, wout):
    l = px.shape[1]
    kern = hyena_filter_circular(l, w1, b1, w2, b2, w3, b3, freq, wout)
    hr, hi = filter_spectrum(kern)
    y1 = hyena_order(px, px, hr, hi, conv_w, skip, 0, True, False)
    return hyena_order(y1, px, hr, hi, conv_w, skip, 1, False, True)


@functools.lru_cache(maxsize=None)
def _dense_dft_consts(l):
    n = 2 * l
    k = np.arange(n)
    ang = -2.0 * np.pi * np.outer(k, np.arange(n)) / n
    fwd = np.concatenate([np.cos(ang), np.sin(ang)], axis=0)
    angi = 2.0 * np.pi * np.outer(np.arange(l), k) / n
    inv = np.concatenate([np.cos(angi), -np.sin(angi)], axis=1) / n
    return fwd, inv


def _hy_ctx_kernel(p_ref, kern_ref, cw_ref, skip_ref, fwd_ref, inv_ref, o_ref, pad_ref, t_ref, *, l):
    n = 2 * l
    dot = functools.partial(jnp.dot, preferred_element_type=f32, precision=HIGHEST)
    w = HY_WIDTH

    def conv3(col):
        pad_ref[0:8, :] = jnp.zeros((8, w), f32)
        pad_ref[l + 8:l + 16, :] = jnp.zeros((8, w), f32)
        pad_ref[8:8 + l, :] = p_ref[0, :, col:col + w].astype(f32)
        cw = cw_ref[:, col - C_HY_VX:col - C_HY_VX + w].astype(f32)
        return cw[0:1] * pad_ref[7:7 + l, :] + cw[1:2] * pad_ref[8:8 + l, :] + cw[2:3] * pad_ref[9:9 + l, :]

    y = conv3(C_HY_VX)
    for o in range(HY_ORDER):
        gate = conv3(C_HY_VX + (o + 1) * w)
        hh = dot(fwd_ref[...], kern_ref[:, o * w:(o + 1) * w])
        xx = dot(fwd_ref[:, :l], y)
        xr, xi, hr, hi = xx[:n], xx[n:], hh[:n], hh[n:]
        t_ref[0:n, :] = xr * hr - xi * hi
        t_ref[n:2 * n, :] = xr * hi + xi * hr
        conv = dot(inv_ref[...], t_ref[...])
        y = gate * (conv + y * skip_ref[o:o + 1, :].astype(f32))
    z = p_ref[0, :, C_HY_Z:C_HY_Z + w].astype(f32)
    o_ref[0] = (y * (z * jax.nn.sigmoid(z))).astype(o_ref.dtype)


def hyena_ctx(pc, conv_w, skip, w1, b1, w2, b2, w3, b3, freq, wout):
    b, lc, _ = pc.shape
    n = 2 * lc
    kern = hyena_filter_circular(lc, w1, b1, w2, b2, w3, b3, freq, wout)
    fwd, inv = [jnp.asarray(a, f32) for a in _dense_dft_consts(lc)]
    full = lambda a: pl.BlockSpec(a.shape, lambda bi: (0,) * a.ndim)
    return pl.pallas_call(
        functools.partial(_hy_ctx_kernel, l=lc),
        grid=(b,),
        in_specs=[pl.BlockSpec((1, lc, MAIN_W), lambda bi: (bi, 0, 0)), full(kern), full(conv_w), full(skip),
                  full(fwd), full(inv)],
        out_specs=pl.BlockSpec((1, lc, HY_WIDTH), lambda bi: (bi, 0, 0)),
        out_shape=jax.ShapeDtypeStruct((b, lc, HY_WIDTH), bf16),
        scratch_shapes=[pltpu.VMEM((lc + 16, HY_WIDTH), f32), pltpu.VMEM((2 * n, HY_WIDTH), f32)],
        compiler_params=pltpu.CompilerParams(dimension_semantics=("arbitrary",), vmem_limit_bytes=VMEM_LIMIT),
        name="hyena_ctx",
    )(pc, kern, conv_w, skip, fwd, inv)


DN_PREP_ROWS = 256
DN_SUB_ROWS = 64
DN_HALO = 8


@functools.lru_cache(maxsize=None)
def _rope_tables(l):
    half = DN_HEAD_DIM // 2
    nf = half // 2
    t = np.arange(l)
    inv = (np.float32(ROPE_BASE) ** (-np.arange(nf, dtype=np.float32) / nf)).astype(np.float32)
    ang_r = ((t // GRID_W).astype(np.float32)[:, None] * inv[None, :]).astype(np.float32)
    ang_c = ((t % GRID_W).astype(np.float32)[:, None] * inv[None, :]).astype(np.float32)
    cos = np.concatenate([np.cos(ang_r), np.cos(ang_r), np.cos(ang_c), np.cos(ang_c)], axis=1)
    sin = np.concatenate([-np.sin(ang_r), np.sin(ang_r), -np.sin(ang_c), np.sin(ang_c)], axis=1)
    return cos.astype(np.float32), sin.astype(np.float32)


def _dn_prep_kernel(*refs, rope, n_tiles):
    (q_ref, qp_ref, qn_ref, k_ref, kp_ref, kn_ref, v_ref, vp_ref, vn_ref, g_ref, cw_ref, cos_ref, sin_ref, al_ref,
     dt_ref, qo_ref, ko_ref, vo_ref, go_ref, pad_ref) = refs
    i = pl.program_id(1)
    t = DN_PREP_ROWS
    w = DN_WIDTH
    lane = lax.broadcasted_iota(jnp.int32, (DN_SUB_ROWS, w), 1)
    first_half = (lane % (DN_HEAD_DIM // 2)) < (DN_HEAD_DIM // 4)
    for idx, (m_ref, p_ref, n_ref, o_ref) in enumerate(((q_ref, qp_ref, qn_ref, qo_ref), (k_ref, kp_ref, kn_ref, ko_ref),
                                                        (v_ref, vp_ref, vn_ref, vo_ref))):
        pad_ref[0:DN_HALO, :] = jnp.where(i == 0, 0.0, p_ref[0].astype(f32))
        pad_ref[DN_HALO:DN_HALO + t, :] = m_ref[0].astype(f32)
        pad_ref[DN_HALO + t:2 * DN_HALO + t, :] = jnp.where(i == n_tiles - 1, 0.0, n_ref[0].astype(f32))
        cw = cw_ref[:, idx * w:(idx + 1) * w].astype(f32)

        def sub(s, c):
            r = pl.multiple_of(s * DN_SUB_ROWS, DN_SUB_ROWS)
            blk = pad_ref[pl.ds(r, DN_SUB_ROWS + 2 * DN_HALO), :]
            base = DN_HALO - DN_CONV_W // 2
            y = cw[0:1] * blk[base:base + DN_SUB_ROWS]
            for j in range(1, DN_CONV_W):
                y = y + cw[j:j + 1] * blk[base + j:base + j + DN_SUB_ROWS]
            y = y * jax.nn.sigmoid(y)
            if idx < 2:
                if rope:
                    cos = jnp.concatenate([cos_ref[pl.ds(r, DN_SUB_ROWS), :]] * DN_HEADS, axis=1)
                    sin = jnp.concatenate([sin_ref[pl.ds(r, DN_SUB_ROWS), :]] * DN_HEADS, axis=1)
                    q4 = DN_HEAD_DIM // 4
                    swapped = jnp.where(first_half, pltpu.roll(y, w - q4, 1), pltpu.roll(y, q4, 1))
                    y = y * cos + swapped * sin
                outs = []
                for h in range(DN_HEADS):
                    yh = y[:, h * DN_HEAD_DIM:(h + 1) * DN_HEAD_DIM]
                    nrm = lax.rsqrt(jnp.sum(yh * yh, axis=-1, keepdims=True) + NORM_EPS)
                    if idx == 0:
                        nrm = nrm * (DN_HEAD_DIM ** -0.5)
                    outs.append(yh * nrm)
                y = jnp.concatenate(outs, axis=1)
            o_ref[0, pl.ds(r, DN_SUB_ROWS), :] = y.astype(o_ref.dtype)
            return c
        lax.fori_loop(0, t // DN_SUB_ROWS, sub, 0)

    g = g_ref[0]
    glane = lax.broadcasted_iota(jnp.int32, g.shape, 1)
    xa = g + dt_ref[...]
    softplus = jnp.maximum(xa, 0.0) + jnp.log1p(jnp.exp(-jnp.abs(xa)))
    go_ref[0] = jnp.where(glane < 2 * DN_HEADS, jax.nn.sigmoid(g), -jnp.exp(al_ref[...]) * softplus)


def dn_prep(p, gates_raw, conv_w, a_log, dt_bias, rope):
    b, l, _ = p.shape
    t = DN_PREP_ROWS
    w = DN_WIDTH
    n_tiles = l // t
    hb = t // DN_HALO
    cblk = C_DN_QKV // w
    cos, sin = [jnp.asarray(a) for a in _rope_tables(l)]
    pad = jnp.zeros((2 * DN_HEADS,), f32)
    rest = jnp.zeros((GATE_W - 4 * DN_HEADS,), f32)
    al = jnp.concatenate([pad, a_log.reshape(-1), rest])[None]
    dt = jnp.concatenate([pad, dt_bias.reshape(-1), rest])[None]

    def slab(j):
        return [pl.BlockSpec((1, t, w), lambda bi, i: (bi, i, cblk + j)),
                pl.BlockSpec((1, DN_HALO, w), lambda bi, i: (bi, jnp.maximum(i * hb - 1, 0), cblk + j)),
                pl.BlockSpec((1, DN_HALO, w), lambda bi, i: (bi, jnp.minimum((i + 1) * hb, l // DN_HALO - 1), cblk + j))]

    row = lambda a: pl.BlockSpec(a.shape, lambda bi, i: (0, 0))
    tab = pl.BlockSpec((t, DN_HEAD_DIM), lambda bi, i: (i, 0))
    out_blk = pl.BlockSpec((1, t, w), lambda bi, i: (bi, i, 0))
    return pl.pallas_call(
        functools.partial(_dn_prep_kernel, rope=rope, n_tiles=n_tiles),
        grid=(b, n_tiles),
        in_specs=slab(0) + slab(1) + slab(2) + [
            pl.BlockSpec((1, t, GATE_W), lambda bi, i: (bi, i, 0)), row(conv_w), tab, tab, row(al), row(dt)],
        out_specs=[out_blk, out_blk, out_blk, pl.BlockSpec((1, t, GATE_W), lambda bi, i: (bi, i, 0))],
        out_shape=[jax.ShapeDtypeStruct((b, l, w), bf16)] * 3 + [jax.ShapeDtypeStruct((b, l, GATE_W), f32)],
        scratch_shapes=[pltpu.VMEM((t + 2 * DN_HALO, w), f32)],
        compiler_params=pltpu.CompilerParams(dimension_semantics=("arbitrary", "arbitrary"),
                                             vmem_limit_bytes=VMEM_LIMIT),
        name="dn_prep",
    )(p, p, p, p, p, p, p, p, p, gates_raw, conv_w, cos, sin, al, dt)


def _chunk_order(s, d, nc, nl):
    is_ctx = s < nc
    c_idx = jnp.where(d == 0, jnp.minimum(s, nc - 1), jnp.maximum(nc - 1 - s, 0))
    x_idx = jnp.where(d == 0, jnp.maximum(s - nc, 0), jnp.minimum(nl - 1 - (s - nc), nl - 1))
    return is_ctx, c_idx, x_idx


def _dn_scan_kernel(qc_ref, kc_ref, vc_ref, gc_ref, qx_ref, kx_ref, vx_ref, gx_ref, oc_ref, ox_ref, s_ref, *, nc, nl):
    d = pl.program_id(1)
    s = pl.program_id(2)
    c = DN_CHUNK
    dh = DN_HEAD_DIM
    is_ctx = s < nc

    @pl.when(s == 0)
    def _():
        s_ref[...] = jnp.zeros(s_ref.shape, f32)

    pick = lambda a_ref, b_ref: jnp.where(is_ctx, a_ref[0], b_ref[0])
    q, k, v, gates = pick(qc_ref, qx_ref), pick(kc_ref, kx_ref), pick(vc_ref, vx_ref), pick(gc_ref, gx_ref)

    ri = lax.broadcasted_iota(jnp.int32, (c, c), 0)
    ci = lax.broadcasted_iota(jnp.int32, (c, c), 1)
    rev = d == 1
    ahead = (ri - ci) * jnp.where(rev, -1, 1)
    incl = ahead >= 0
    strict = ahead > 0
    eye = (ri == ci).astype(f32)
    hdot = functools.partial(jnp.dot, preferred_element_type=f32, precision=HIGHEST)
    bdot = lambda a, b: jnp.dot(a.astype(bf16), b.astype(bf16), preferred_element_type=f32)

    gcs = hdot(incl.astype(f32), gates)
    gtot = jnp.sum(gates, axis=0, keepdims=True)
    er = lax.broadcasted_iota(jnp.int32, (GATE_W, GATE_W), 0)
    ec = lax.broadcasted_iota(jnp.int32, (GATE_W, GATE_W), 1)
    gcs_t = lax.dot_general((er == ec).astype(f32), gcs, NT_DIMS, preferred_element_type=f32, precision=HIGHEST)
    lane = lax.broadcasted_iota(jnp.int32, gates.shape, 1)
    sub = lax.broadcasted_iota(jnp.int32, gcs_t.shape, 0)

    outs = []
    for h in range(DN_HEADS):
        hs = slice(h * dh, (h + 1) * dh)
        qh, kh, vh = q[:, hs].astype(f32), k[:, hs].astype(f32), v[:, hs].astype(f32)
        bcol = jnp.where(rev, DN_HEADS + h, h)
        gl = bcol + 2 * DN_HEADS
        beta = jnp.sum(jnp.where(lane == bcol, gates, 0.0), axis=1, keepdims=True)
        gcol = jnp.sum(jnp.where(lane == gl, gcs, 0.0), axis=1, keepdims=True)
        glast = jnp.sum(jnp.where(lane[0:1] == gl, gtot, 0.0), axis=1, keepdims=True)
        grow = jnp.sum(jnp.where(sub == gl, gcs_t, 0.0), axis=0, keepdims=True)
        decay = jnp.where(incl, jnp.exp(jnp.minimum(gcol - grow, 0.0)), 0.0)
        kb = kh * beta
        kk = lax.dot_general(kb, kh, NT_DIMS, preferred_element_type=f32, precision=HIGHEST)
        nmat = -jnp.where(strict, kk * decay, 0.0)
        tinv = eye + nmat
        pw = nmat
        for _ in range(int(math.log2(c)) - 1):
            pw = hdot(pw, pw)
            tinv = tinv + hdot(tinv, pw)
        eg = jnp.exp(gcol)
        sol = hdot(tinv, jnp.concatenate([vh * beta, kb * eg], axis=1))
        u, wmat = sol[:, :dh], sol[:, dh:]
        qk = lax.dot_general(qh.astype(bf16), kh.astype(bf16), NT_DIMS, preferred_element_type=f32)
        aqk = jnp.where(incl, qk * decay, 0.0)
        state = s_ref[h]
        v_new = u - bdot(wmat, state)
        outs.append(bdot(qh * eg, state) + bdot(aqk, v_new))
        kdec = kh * jnp.exp(glast - gcol)
        s_ref[h] = state * jnp.exp(glast) + lax.dot_general(kdec.astype(bf16), v_new.astype(bf16), TN_DIMS,
                                                           preferred_element_type=f32)
    o_all = jnp.concatenate(outs, axis=1)

    @pl.when(is_ctx)
    def _():
        oc_ref[0, 0] = o_all

    @pl.when(jnp.logical_not(is_ctx))
    def _():
        ox_ref[0, 0] = o_all


def dn_scan(qc, kc, vc, gc, qx, kx, vx, gx):
    b, lc, w = qc.shape
    l = qx.shape[1]
    c = DN_CHUNK
    nc, nl = lc // c, l // c
    cmap = lambda bi, d, s: (bi, _chunk_order(s, d, nc, nl)[1], 0)
    xmap = lambda bi, d, s: (bi, _chunk_order(s, d, nc, nl)[2], 0)
    ocmap = lambda bi, d, s: (d, bi, _chunk_order(s, d, nc, nl)[1], 0)
    oxmap = lambda bi, d, s: (d, bi, _chunk_order(s, d, nc, nl)[2], 0)
    blk = lambda width, m: pl.BlockSpec((1, c, width), m)
    return pl.pallas_call(
        functools.partial(_dn_scan_kernel, nc=nc, nl=nl),
        grid=(b, 2, nc + nl),
        in_specs=[blk(w, cmap), blk(w, cmap), blk(w, cmap), blk(GATE_W, cmap),
                  blk(w, xmap), blk(w, xmap), blk(w, xmap), blk(GATE_W, xmap)],
        out_specs=[pl.BlockSpec((1, 1, c, w), ocmap), pl.BlockSpec((1, 1, c, w), oxmap)],
        out_shape=[jax.ShapeDtypeStruct((2, b, lc, w), f32), jax.ShapeDtypeStruct((2, b, l, w), f32)],
        scratch_shapes=[pltpu.VMEM((DN_HEADS, DN_HEAD_DIM, DN_HEAD_DIM), f32)],
        compiler_params=pltpu.CompilerParams(dimension_semantics=("arbitrary", "arbitrary", "arbitrary"),
                                             vmem_limit_bytes=VMEM_LIMIT),
        name="dn_scan",
    )(qc, kc, vc, gc, qx, kx, vx, gx)


def _dn_out_kernel(o_ref, z_ref, nw_ref, y_ref):
    o = o_ref[0, 0] + o_ref[1, 0]
    z = z_ref[0].astype(f32)
    nw = nw_ref[...].astype(f32)
    outs = []
    for h in range(DN_HEADS):
        oh = o[:, h * DN_HEAD_DIM:(h + 1) * DN_HEAD_DIM]
        outs.append(oh * lax.rsqrt(jnp.mean(oh * oh, axis=-1, keepdims=True) + NORM_EPS) * nw)
    y_ref[0] = (jnp.concatenate(outs, axis=1) * (z * jax.nn.sigmoid(z))).astype(y_ref.dtype)


def dn_output(o, p, norm_w, tm):
    _, b, l, w = o.shape
    return pl.pallas_call(
        _dn_out_kernel,
        grid=(b, l // tm),
        in_specs=[pl.BlockSpec((2, 1, tm, w), lambda bi, i: (0, bi, i, 0)),
                  pl.BlockSpec((1, tm, w), lambda bi, i: (bi, i, C_DN_Z // w)),
                  pl.BlockSpec((1, DN_HEAD_DIM), lambda bi, i: (0, 0))],
        out_specs=pl.BlockSpec((1, tm, w), lambda bi, i: (bi, i, 0)),
        out_shape=jax.ShapeDtypeStruct((b, l, w), bf16),
        compiler_params=pltpu.CompilerParams(dimension_semantics=("arbitrary", "arbitrary"),
                                             vmem_limit_bytes=VMEM_LIMIT),
        name="dn_output",
    )(o, p, norm_w.reshape(1, DN_HEAD_DIM))


def _pack_w_in(w_in):
    depth, d, _ = w_in.shape
    main = jnp.concatenate([w_in[:, :, :3072], w_in[:, :, 3088:4112]], axis=-1).astype(bf16)
    gate = jnp.concatenate([w_in[:, :, 3072:3088], jnp.zeros((depth, d, GATE_W - 16), w_in.dtype)],
                           axis=-1).astype(bf16)
    return main, gate


def kernel(x, c, ctx, c_ctx, w_ada, b_ada, g_pre, g_post, w_in, w_out, na_rpb, dn_conv, dn_a_log, dn_dt_bias,
           dn_norm, hy_conv, hy_w1, hy_b1, hy_w2, hy_b2, hy_w3, hy_b3, hy_freq, hy_wout, hy_skip):
    bsz, l, d = x.shape
    lc = ctx.shape[1]
    depth = w_in.shape[0]
    cond = jnp.concatenate([c, c_ctx[None], jnp.zeros((7, d), f32)], axis=0)
    mod = modulation_all(cond, w_ada, b_ada)
    w_main, w_gate = _pack_w_in(w_in)
    w_out_b = w_out.astype(bf16)

    for i in range(depth):
        last = i == depth - 1
        shift_x, scale_x, gate_x = [mod[i, :bsz, j * d:(j + 1) * d][:, None] for j in range(3)]
        shift_c, scale_c, gate_c = [mod[i, bsz:bsz + 1, j * d:(j + 1) * d][:, None] for j in range(3)]
        px, gx = in_projection(x, g_pre[i], scale_x, shift_x, w_main[i], w_gate[i], 512)
        pc, gc = in_projection(ctx, g_pre[i], scale_c, shift_c, w_main[i], w_gate[i], 256)

        hy_args = (hy_conv[i], hy_skip[i], hy_w1[i], hy_b1[i], hy_w2[i], hy_b2[i], hy_w3[i], hy_b3[i], hy_freq[i],
                   hy_wout[i])
        dn_args = (dn_conv[i], dn_a_log[i], dn_dt_bias[i])

        out_a_x = na_attention(px, pc, na_rpb[i])

        dn_c = dn_prep(pc, gc, *dn_args, False)
        dn_x = dn_prep(px, gx, *dn_args, True)
        do_c, do_x = dn_scan(*dn_c, *dn_x)
        out_b_x = dn_output(do_x, px, dn_norm[i], 512)

        out_c_x = hyena_latent(px, *hy_args)

        new_x = out_projection(out_a_x, out_b_x, out_c_x, w_out_b[i], g_post[i], gate_x, x, 512)

        if not last:
            out_a_c = ctx_attention(pc)
            out_b_c = dn_output(do_c, pc, dn_norm[i], lc)
            out_c_c = hyena_ctx(pc, *hy_args)
            ctx = out_projection(out_a_c, out_b_c, out_c_c, w_out_b[i], g_post[i], gate_c, ctx, lc)
        x = new_x
    return x
```

```python
import functools
import math

import numpy as np
import jax
import jax.numpy as jnp
from jax import lax
from jax.experimental import pallas as pl
from jax.experimental.pallas import tpu as pltpu

D_MODEL = 1024
GRID_W = 64
NA_HEAD_DIM = 64
NA_WIDTH = 256
NA_HEADS = 4
NA_WIN_ROWS = 8
NA_WIN_COLS = 16
DN_HEAD_DIM = 128
DN_WIDTH = 512
DN_HEADS = 4
DN_CONV_W = 5
DN_CHUNK = 64
HY_WIDTH = 256
HY_ORDER = 2
HY_EMB = 33
HY_DECAY_TARGET = 1e-2
HY_FAST_DECAY = 0.3
HY_SLOW_DECAY = 1.5
ROPE_BASE = 10000.0
NORM_EPS = 1e-6

C_NA_QKV, C_NA_Z, C_DN_QKV, C_DN_Z, C_HY_VX, C_HY_Z = 0, 768, 1024, 2560, 3072, 3840
MAIN_W = 4096
GATE_W = 128
LANE = 128
VMEM_LIMIT = 52 * 1024 * 1024

f32 = jnp.float32
bf16 = jnp.bfloat16
HIGHEST = lax.Precision.HIGHEST
NT_DIMS = (((1,), (1,)), ((), ()))
TN_DIMS = (((0,), (0,)), ((), ()))


def _mod_kernel(c_ref, w_ref, b_ref, o_ref):
    c = c_ref[...]
    a = c * jax.nn.sigmoid(c)
    o_ref[0] = jnp.dot(a, w_ref[0], preferred_element_type=f32, precision=lax.Precision.HIGHEST) + b_ref[0]


def modulation_all(cond, w_ada, b_ada):
    depth, d, d3 = w_ada.shape
    r = cond.shape[0]
    tn = 512
    return pl.pallas_call(
        _mod_kernel,
        grid=(depth, d3 // tn),
        in_specs=[pl.BlockSpec((r, d), lambda i, j: (0, 0)),
                  pl.BlockSpec((1, d, tn), lambda i, j: (i, 0, j)),
                  pl.BlockSpec((1, 1, tn), lambda i, j: (i, 0, j))],
        out_specs=pl.BlockSpec((1, r, tn), lambda i, j: (i, 0, j)),
        out_shape=jax.ShapeDtypeStruct((depth, r, d3), f32),
        compiler_params=pltpu.CompilerParams(dimension_semantics=("arbitrary", "arbitrary"),
                                             vmem_limit_bytes=VMEM_LIMIT),
        name="modulation",
    )(cond, w_ada, b_ada.reshape(depth, 1, d3))


def _inproj_kernel(x_ref, g_ref, sc_ref, sh_ref, w_ref, wg_ref, o_ref, og_ref):
    x = x_ref[0]
    ms = jnp.mean(x * x, axis=-1, keepdims=True)
    h = x * lax.rsqrt(ms + NORM_EPS) * g_ref[...]
    h = (h * (1.0 + sc_ref[0]) + sh_ref[0]).astype(bf16)
    for j in range(MAIN_W // 1024):
        o_ref[0, :, j * 1024:(j + 1) * 1024] = jnp.dot(
            h, w_ref[:, j * 1024:(j + 1) * 1024], preferred_element_type=f32).astype(bf16)
    og_ref[0] = jnp.dot(h, wg_ref[...], preferred_element_type=f32)


def in_projection(x, g, scale, shift, w_main, w_gate, tm):
    b, l, d = x.shape
    per_b = scale.shape[0] == b
    mod_map = (lambda bi, i: (bi, 0, 0)) if per_b else (lambda bi, i: (0, 0, 0))
    return pl.pallas_call(
        _inproj_kernel,
        grid=(b, l // tm),
        in_specs=[pl.BlockSpec((1, tm, d), lambda bi, i: (bi, i, 0)),
                  pl.BlockSpec((1, d), lambda bi, i: (0, 0)),
                  pl.BlockSpec((1, 1, d), mod_map),
                  pl.BlockSpec((1, 1, d), mod_map),
                  pl.BlockSpec((d, MAIN_W), lambda bi, i: (0, 0)),
                  pl.BlockSpec((d, GATE_W), lambda bi, i: (0, 0))],
        out_specs=[pl.BlockSpec((1, tm, MAIN_W), lambda bi, i: (bi, i, 0)),
                   pl.BlockSpec((1, tm, GATE_W), lambda bi, i: (bi, i, 0))],
        out_shape=[jax.ShapeDtypeStruct((b, l, MAIN_W), bf16),
                   jax.ShapeDtypeStruct((b, l, GATE_W), f32)],
        compiler_params=pltpu.CompilerParams(dimension_semantics=("arbitrary", "arbitrary"),
                                             vmem_limit_bytes=VMEM_LIMIT),
        name="in_projection",
    )(x, g.reshape(1, d), scale, shift, w_main, w_gate)


def _outproj_kernel(a_ref, b_ref, c_ref, w_ref, g_ref, gate_ref, x_ref, o_ref):
    y = (jnp.dot(a_ref[0], w_ref[0:NA_WIDTH], preferred_element_type=f32)
         + jnp.dot(b_ref[0], w_ref[NA_WIDTH:NA_WIDTH + DN_WIDTH], preferred_element_type=f32)
         + jnp.dot(c_ref[0], w_ref[NA_WIDTH + DN_WIDTH:], preferred_element_type=f32))
    ms = jnp.mean(y * y, axis=-1, keepdims=True)
    yn = y * lax.rsqrt(ms + NORM_EPS) * g_ref[...]
    o_ref[0] = x_ref[0] + gate_ref[0] * yn


def out_projection(out_a, out_b, out_c, w_out, g_post, gate, x, tm):
    b, l, d = x.shape
    per_b = gate.shape[0] == b
    mod_map = (lambda bi, i: (bi, 0, 0)) if per_b else (lambda bi, i: (0, 0, 0))
    part = lambda a: pl.BlockSpec((1, tm, a.shape[-1]), lambda bi, i: (bi, i, 0))
    return pl.pallas_call(
        _outproj_kernel,
        grid=(b, l // tm),
        in_specs=[part(out_a), part(out_b), part(out_c),
                  pl.BlockSpec((d, d), lambda bi, i: (0, 0)),
                  pl.BlockSpec((1, d), lambda bi, i: (0, 0)),
                  pl.BlockSpec((1, 1, d), mod_map),
                  pl.BlockSpec((1, tm, d), lambda bi, i: (bi, i, 0))],
        out_specs=pl.BlockSpec((1, tm, d), lambda bi, i: (bi, i, 0)),
        out_shape=jax.ShapeDtypeStruct((b, l, d), f32),
        compiler_params=pltpu.CompilerParams(dimension_semantics=("arbitrary", "arbitrary"),
                                             vmem_limit_bytes=VMEM_LIMIT),
        name="out_projection",
    )(out_a, out_b, out_c, w_out, g_post.reshape(1, d), gate, x)


NA_TQ_ROWS = 4
NA_TQ = NA_TQ_ROWS * GRID_W
NA_KV_ROWS = NA_TQ_ROWS + NA_WIN_ROWS
NA_KV = NA_KV_ROWS * GRID_W
MASK_VALUE = -1e30


def _na_window_start(t, rows):
    return jnp.clip(t * NA_TQ_ROWS - NA_WIN_ROWS // 2, 0, rows - NA_KV_ROWS)


def _na_bias_kernel(rpb_ref, sel_ref, mask_ref, o_ref):
    o_ref[...] = jnp.dot(rpb_ref[...], sel_ref[...], preferred_element_type=f32, precision=HIGHEST) + mask_ref[...]


def na_bias_tables(rpb, rows):
    nh, ndr, ndc = rpb.shape
    col = np.arange(GRID_W)
    dc = np.clip(col[None, :] - col[:, None], -(NA_WIN_COLS - 1), NA_WIN_COLS - 1) + (NA_WIN_COLS - 1)
    c0 = np.clip(col - NA_WIN_COLS // 2, 0, GRID_W - NA_WIN_COLS)
    in_win = (col[None, :] >= c0[:, None]) & (col[None, :] < c0[:, None] + NA_WIN_COLS)
    sel = np.zeros((LANE, GRID_W * GRID_W), np.float32)
    sel[dc.reshape(-1), np.arange(GRID_W * GRID_W)] = 1.0
    cmask = np.where(in_win, 0.0, MASK_VALUE).astype(np.float32).reshape(1, -1)
    rpb2 = jnp.pad(rpb.astype(f32).reshape(nh * ndr, ndc), ((0, 0), (0, LANE - ndc)))
    tab = pl.pallas_call(
        _na_bias_kernel,
        out_shape=jax.ShapeDtypeStruct((nh * ndr, GRID_W * GRID_W), f32),
        name="na_bias",
    )(rpb2, jnp.asarray(sel), jnp.asarray(cmask)).reshape(nh, ndr, GRID_W, GRID_W)
    masked = jnp.full((nh, GRID_W, GRID_W), MASK_VALUE, f32)
    n_tiles = rows // NA_TQ_ROWS
    tabs = []
    for t in (0, 1, n_tiles - 1):
        r0 = t * NA_TQ_ROWS
        ws = int(np.clip(r0 - NA_WIN_ROWS // 2, 0, rows - NA_KV_ROWS))
        blocks = []
        for qr in range(r0, r0 + NA_TQ_ROWS):
            band0 = int(np.clip(qr - NA_WIN_ROWS // 2, 0, rows - NA_WIN_ROWS))
            blocks.append(jnp.concatenate(
                [tab[:, kr - qr + NA_WIN_ROWS - 1] if band0 <= kr < band0 + NA_WIN_ROWS else masked
                 for kr in range(ws, ws + NA_KV_ROWS)], axis=-1))
        tabs.append(jnp.concatenate(blocks, axis=-2))
    return jnp.stack(tabs)


def _na_kernel(q_ref, k_ref, v_ref, z_ref, kc_ref, vc_ref, bias_ref, o_ref, *, rows):
    t = pl.program_id(1)
    ws = pl.multiple_of(_na_window_start(t, rows) * GRID_W, GRID_W)
    q = q_ref[0]
    kw = k_ref[0, pl.ds(ws, NA_KV), :]
    vw = v_ref[0, pl.ds(ws, NA_KV), :]
    kc = kc_ref[0]
    vc = vc_ref[0]
    outs = []
    for h in range(NA_HEADS):
        sl = slice(h * NA_HEAD_DIM, (h + 1) * NA_HEAD_DIM)
        qh = q[:, sl] * (NA_HEAD_DIM ** -0.5)
        s1 = lax.dot_general(qh, kw[:, sl], NT_DIMS, preferred_element_type=f32) + bias_ref[0, h]
        s2 = lax.dot_general(qh, kc[:, sl], NT_DIMS, preferred_element_type=f32)
        m = jnp.maximum(jnp.max(s1, axis=-1, keepdims=True), jnp.max(s2, axis=-1, keepdims=True))
        p1 = jnp.exp(s1 - m)
        p2 = jnp.exp(s2 - m)
        den = jnp.sum(p1, axis=-1, keepdims=True) + jnp.sum(p2, axis=-1, keepdims=True)
        o = (jnp.dot(p1.astype(bf16), vw[:, sl], preferred_element_type=f32)
             + jnp.dot(p2.astype(bf16), vc[:, sl], preferred_element_type=f32))
        outs.append(o / den)
    z = z_ref[0].astype(f32)
    o = jnp.concatenate(outs, axis=-1) * (z * jax.nn.sigmoid(z))
    o_ref[0] = o.astype(o_ref.dtype)


def na_attention(px, pc, rpb):
    b, l, _ = px.shape
    lc = pc.shape[1]
    rows = l // GRID_W
    n_tiles = rows // NA_TQ_ROWS
    bias = na_bias_tables(rpb, rows)
    w = NA_WIDTH
    cq, ck, cv, cz = [(C_NA_QKV + j * w) // w for j in range(3)] + [C_NA_Z // w]

    def bias_map(bi, t):
        return (jnp.where(t == 0, 0, jnp.where(t == n_tiles - 1, 2, 1)), 0, 0, 0)

    return pl.pallas_call(
        functools.partial(_na_kernel, rows=rows),
        grid=(b, n_tiles),
        in_specs=[pl.BlockSpec((1, NA_TQ, w), lambda bi, t: (bi, t, cq)),
                  pl.BlockSpec((1, l, w), lambda bi, t: (bi, 0, ck)),
                  pl.BlockSpec((1, l, w), lambda bi, t: (bi, 0, cv)),
                  pl.BlockSpec((1, NA_TQ, w), lambda bi, t: (bi, t, cz)),
                  pl.BlockSpec((1, lc, w), lambda bi, t: (bi, 0, ck)),
                  pl.BlockSpec((1, lc, w), lambda bi, t: (bi, 0, cv)),
                  pl.BlockSpec((1, NA_HEADS, NA_TQ, NA_KV), bias_map)],
        out_specs=pl.BlockSpec((1, NA_TQ, w), lambda bi, t: (bi, t, 0)),
        out_shape=jax.ShapeDtypeStruct((b, l, w), bf16),
        compiler_params=pltpu.CompilerParams(dimension_semantics=("arbitrary", "arbitrary"),
                                             vmem_limit_bytes=VMEM_LIMIT),
        name="na_attention",
    )(px, px, px, px, pc, pc, bias)


def _ctx_attn_kernel(q_ref, k_ref, v_ref, z_ref, o_ref):
    q = q_ref[0]
    k = k_ref[0]
    v = v_ref[0]
    outs = []
    for h in range(NA_HEADS):
        sl = slice(h * NA_HEAD_DIM, (h + 1) * NA_HEAD_DIM)
        qh = q[:, sl] * (NA_HEAD_DIM ** -0.5)
        s = lax.dot_general(qh, k[:, sl], NT_DIMS, preferred_element_type=f32)
        p = jnp.exp(s - jnp.max(s, axis=-1, keepdims=True))
        den = jnp.sum(p, axis=-1, keepdims=True)
        outs.append(jnp.dot(p.astype(bf16), v[:, sl], preferred_element_type=f32) / den)
    z = z_ref[0].astype(f32)
    o_ref[0] = (jnp.concatenate(outs, axis=-1) * (z * jax.nn.sigmoid(z))).astype(o_ref.dtype)


def ctx_attention(pc):
    b, lc, _ = pc.shape
    w = NA_WIDTH
    cq, ck, cv, cz = [(C_NA_QKV + j * w) // w for j in range(3)] + [C_NA_Z // w]
    spec = lambda cidx: pl.BlockSpec((1, lc, w), lambda bi: (bi, 0, cidx))
    return pl.pallas_call(
        _ctx_attn_kernel,
        grid=(b,),
        in_specs=[spec(cq), spec(ck), spec(cv), spec(cz)],
        out_specs=pl.BlockSpec((1, lc, w), lambda bi: (bi, 0, 0)),
        out_shape=jax.ShapeDtypeStruct((b, lc, w), bf16),
        compiler_params=pltpu.CompilerParams(dimension_semantics=("arbitrary",), vmem_limit_bytes=VMEM_LIMIT),
        name="ctx_attention",
    )(pc, pc, pc, pc)


HY_CH = 128
FFT_N2 = 128
ROW_CHUNK = 256
STRIDE_PAD = 8
HY_FILT_ROWS = 512
HY_BANDS = (HY_EMB - 1) // 2


def _hy_filter_kernel(fr_ref, w1t_ref, w1c_ref, w1s_ref, b1_ref, w2_ref, b2_ref, w3_ref, b3_ref, freq_ref, wo_ref,
                      dl_ref, o_ref, *, l):
    i = pl.program_id(0)
    n = i * HY_FILT_ROWS + lax.broadcasted_iota(jnp.int32, (HY_FILT_ROWS, 1), 0)
    lag = jnp.where(n < l, n, 2 * l - n).astype(f32)
    t = lag * (1.0 / (l - 1))
    ang = (lag * (2.0 * math.pi / l)) * fr_ref[...]
    dot = functools.partial(jnp.dot, preferred_element_type=f32, precision=HIGHEST)
    pre = t * w1t_ref[...] + dot(jnp.cos(ang), w1c_ref[...]) - dot(jnp.sin(ang), w1s_ref[...]) + b1_ref[...]
    hid = jnp.sin(freq_ref[0:1] * pre)
    hid = jnp.sin(freq_ref[1:2] * (dot(hid, w2_ref[...]) + b2_ref[...]))
    hid = jnp.sin(freq_ref[2:3] * (dot(hid, w3_ref[...]) + b3_ref[...]))
    filt = dot(hid, wo_ref[...])
    window = jnp.exp(-t * dl_ref[...])
    live = jnp.where(n == l, 0.0, 1.0)
    fwd = n < l
    for o in range(HY_ORDER):
        a = filt[:, (2 * o) * HY_WIDTH:(2 * o + 1) * HY_WIDTH]
        b = filt[:, (2 * o + 1) * HY_WIDTH:(2 * o + 2) * HY_WIDTH]
        o_ref[:, o * HY_WIDTH:(o + 1) * HY_WIDTH] = jnp.where(fwd, a, b) * window * live


def hyena_filter_circular(l, w1, b1, w2, b2, w3, b3, freq, wout):
    n = 2 * l
    assert n % HY_FILT_ROWS == 0
    hid = w1.shape[1]
    fr = np.zeros((1, LANE), np.float32)
    fr[0, :HY_BANDS] = np.linspace(1e-4, HY_BANDS - 1, HY_BANDS, dtype=np.float32)
    padrows = lambda w: jnp.concatenate([w, jnp.zeros((LANE - w.shape[0], hid), f32)], axis=0)
    w1t, w1c, w1s = w1[0:1], padrows(w1[1:1 + HY_BANDS]), padrows(w1[1 + HY_BANDS:])
    max_decay = math.log(HY_DECAY_TARGET) / HY_FAST_DECAY
    min_decay = math.log(HY_DECAY_TARGET) / HY_SLOW_DECAY
    deltas = np.abs(np.linspace(min_decay, max_decay, HY_WIDTH, dtype=np.float32))[None]
    full = lambda a: pl.BlockSpec(a.shape, lambda i: (0,) * a.ndim)
    args = [jnp.asarray(fr), w1t, w1c, w1s, b1[None], w2, b2[None], w3, b3[None], freq, wout, jnp.asarray(deltas)]
    return pl.pallas_call(
        functools.partial(_hy_filter_kernel, l=l),
        grid=(n // HY_FILT_ROWS,),
        in_specs=[full(a) for a in args],
        out_specs=pl.BlockSpec((HY_FILT_ROWS, HY_ORDER * HY_WIDTH), lambda i: (i, 0)),
        out_shape=jax.ShapeDtypeStruct((n, HY_ORDER * HY_WIDTH), f32),
        compiler_params=pltpu.CompilerParams(dimension_semantics=("arbitrary",), vmem_limit_bytes=VMEM_LIMIT),
        name="hyena_filter",
    )(*args)


@functools.lru_cache(maxsize=None)
def _fft_consts(n, k_in, k_out):
    n1 = n // FFT_N2
    k1 = np.arange(n1)
    n2 = np.arange(FFT_N2)
    tt = FFT_N2 * np.arange(k_in)[None, None, :] + n2[:, None, None]
    ang = -2.0 * np.pi * (k1[None, :, None] * tt) / n
    f1 = np.concatenate([np.cos(ang), np.sin(ang)], axis=1)
    a2 = -2.0 * np.pi * np.outer(n2, n2) / FFT_N2
    cr, ci = np.cos(a2), np.sin(a2)
    fblk = np.block([[cr, -ci], [ci, cr]])
    fiblk = np.block([[cr, ci], [-ci, cr]])
    to = FFT_N2 * np.arange(k_out)[None, :, None] + n2[:, None, None]
    ango = 2.0 * np.pi * (k1[None, None, :] * to) / n
    hinv = np.concatenate([np.cos(ango), -np.sin(ango)], axis=2) / n
    return f1, fblk, fiblk, hinv


def _fft_fwd_stage1(x_ref, f1_ref, ar_ref, ai_ref, k_in, n1, sa, dt, prec):
    def body(n2, c):
        xs = x_ref[pl.ds(n2, k_in, stride=FFT_N2), :].astype(dt)
        a = jnp.dot(f1_ref[n2], xs, preferred_element_type=f32, precision=prec)
        off = pl.multiple_of(n2 * sa, 8)
        ar_ref[pl.ds(off, n1), :] = a[:n1]
        ai_ref[pl.ds(off, n1), :] = a[n1:]
        return c
    lax.fori_loop(0, FFT_N2, body, 0)


def _filter_fft_kernel(k_ref, f1_ref, fblk_ref, hr_ref, hi_ref, ar_ref, ai_ref, *, n):
    n1 = n // FFT_N2
    sa = n1 + STRIDE_PAD
    _fft_fwd_stage1(k_ref, f1_ref, ar_ref, ai_ref, n1, n1, sa, f32, HIGHEST)

    def body(k1, c):
        z = jnp.concatenate([ar_ref[pl.ds(k1, FFT_N2, stride=sa), :], ai_ref[pl.ds(k1, FFT_N2, stride=sa), :]], axis=0)
        xx = jnp.dot(fblk_ref[...], z, preferred_element_type=f32, precision=HIGHEST)
        off = pl.multiple_of(k1 * FFT_N2, FFT_N2)
        hr_ref[pl.ds(off, FFT_N2), :] = xx[:FFT_N2]
        hi_ref[pl.ds(off, FFT_N2), :] = xx[FFT_N2:]
        return c
    lax.fori_loop(0, n1, body, 0)


def filter_spectrum(kern):
    n, c = kern.shape
    n1 = n // FFT_N2
    f1, fblk, _, _ = _fft_consts(n, n1, 1)
    f1 = jnp.asarray(f1, f32)
    fblk = jnp.asarray(fblk, f32)
    sa = n1 + STRIDE_PAD
    return pl.pallas_call(
        functools.partial(_filter_fft_kernel, n=n),
        grid=(c // HY_CH,),
        in_specs=[pl.BlockSpec((n, HY_CH), lambda j: (0, j)),
                  pl.BlockSpec(f1.shape, lambda j: (0, 0, 0)),
                  pl.BlockSpec(fblk.shape, lambda j: (0, 0))],
        out_specs=[pl.BlockSpec((n, HY_CH), lambda j: (0, j))] * 2,
        out_shape=[jax.ShapeDtypeStruct((n, c), f32)] * 2,
        scratch_shapes=[pltpu.VMEM((FFT_N2 * sa, HY_CH), f32)] * 2,
        compiler_params=pltpu.CompilerParams(dimension_semantics=("arbitrary",), vmem_limit_bytes=VMEM_LIMIT),
        name="filter_spectrum",
    )(kern, f1, fblk)


def _conv3_rows(src_ref, w_ref, pad_ref, dst_ref, l):
    zeros = jnp.zeros((8, src_ref.shape[-1]), f32)
    pad_ref[0:8, :] = zeros
    pad_ref[l + 8:l + 16, :] = zeros

    def cp(i, c):
        r = pl.multiple_of(i * ROW_CHUNK, ROW_CHUNK)
        pad_ref[pl.ds(8 + r, ROW_CHUNK), :] = src_ref[0, pl.ds(r, ROW_CHUNK), :].astype(f32)
        return c
    lax.fori_loop(0, l // ROW_CHUNK, cp, 0)
    w = w_ref[...].astype(f32)

    def cv(i, c):
        r = pl.multiple_of(i * ROW_CHUNK, ROW_CHUNK)
        blk = pad_ref[pl.ds(r, ROW_CHUNK + 16), :]
        dst_ref[pl.ds(r, ROW_CHUNK), :] = (w[0:1] * blk[7:7 + ROW_CHUNK] + w[1:2] * blk[8:8 + ROW_CHUNK]
                                           + w[2:3] * blk[9:9 + ROW_CHUNK])
        return c
    lax.fori_loop(0, l // ROW_CHUNK, cv, 0)


def _hy_order_kernel(yin_ref, graw_ref, z_ref, cwy_ref, cwg_ref, skip_ref, hr_ref, hi_ref, f1_ref, fblk_ref, fiblk_ref,
                     hinv_ref, o_ref, pad_ref, y_ref, gate_ref, c_ref, ar_ref, ai_ref, br_ref, bi_ref,
                     *, l, first, last):
    n = 2 * l
    n1 = n // FFT_N2
    k1n = l // FFT_N2
    sa = n1 + STRIDE_PAD
    sb = FFT_N2 + STRIDE_PAD
    if first:
        _conv3_rows(yin_ref, cwy_ref, pad_ref, y_ref, l)
        src_ref = y_ref
    else:
        src_ref = yin_ref.at[0]
    _conv3_rows(graw_ref, cwg_ref, pad_ref, gate_ref, l)

    _fft_fwd_stage1(src_ref, f1_ref, ar_ref, ai_ref, k1n, n1, sa, bf16, None)

    def mid(k1, c):
        z = jnp.concatenate([ar_ref[pl.ds(k1, FFT_N2, stride=sa), :], ai_ref[pl.ds(k1, FFT_N2, stride=sa), :]], axis=0)
        xx = jnp.dot(fblk_ref[...], z.astype(bf16), preferred_element_type=f32)
        xr, xi = xx[:FFT_N2], xx[FFT_N2:]
        off = pl.multiple_of(k1 * FFT_N2, FFT_N2)
        hr = hr_ref[pl.ds(off, FFT_N2), :]
        hi = hi_ref[pl.ds(off, FFT_N2), :]
        yy = jnp.concatenate([xr * hr - xi * hi, xr * hi + xi * hr], axis=0).astype(bf16)
        bb = jnp.dot(fiblk_ref[...], yy, preferred_element_type=f32)
        boff = pl.multiple_of(k1 * sb, 8)
        br_ref[pl.ds(boff, FFT_N2), :] = bb[:FFT_N2]
        bi_ref[pl.ds(boff, FFT_N2), :] = bb[FFT_N2:]
        return c
    lax.fori_loop(0, n1, mid, 0)

    def inv2(n2, c):
        bb = jnp.concatenate([br_ref[pl.ds(n2, n1, stride=sb), :], bi_ref[pl.ds(n2, n1, stride=sb), :]], axis=0)
        o = jnp.dot(hinv_ref[n2], bb.astype(bf16), preferred_element_type=f32)
        c_ref[pl.ds(n2, k1n, stride=FFT_N2), :] = o
        return c
    lax.fori_loop(0, FFT_N2, inv2, 0)

    skip = skip_ref[0].astype(f32)

    def fin(i, c):
        r = pl.multiple_of(i * ROW_CHUNK, ROW_CHUNK)
        y = src_ref[pl.ds(r, ROW_CHUNK), :]
        y = gate_ref[pl.ds(r, ROW_CHUNK), :] * (c_ref[pl.ds(r, ROW_CHUNK), :] + y * skip)
        if last:
            z = z_ref[0, pl.ds(r, ROW_CHUNK), :].astype(f32)
            y = y * (z * jax.nn.sigmoid(z))
        o_ref[0, pl.ds(r, ROW_CHUNK), :] = y.astype(o_ref.dtype)
        return c
    lax.fori_loop(0, l // ROW_CHUNK, fin, 0)


def hyena_order(yin, px, hr, hi, conv_w, skip, order, first, last):
    b, l, _ = px.shape
    n = 2 * l
    n1 = n // FFT_N2
    k1n = l // FFT_N2
    sa, sb = n1 + STRIDE_PAD, FFT_N2 + STRIDE_PAD
    nh = HY_WIDTH // HY_CH
    f1, fblk, fiblk, hinv = [jnp.asarray(a, bf16) for a in _fft_consts(n, k1n, k1n)]
    cb = lambda col: col // HY_CH
    c_v, c_g, c_z = cb(C_HY_VX), cb(C_HY_VX + (order + 1) * HY_WIDTH), cb(C_HY_Z)
    once = pl.Buffered(1)
    yin_spec = (pl.BlockSpec((1, l, HY_CH), lambda h, bi: (bi, 0, c_v + h)) if first
                else pl.BlockSpec((1, l, HY_CH), lambda h, bi: (bi, 0, h)))
    return pl.pallas_call(
        functools.partial(_hy_order_kernel, l=l, first=first, last=last),
        grid=(nh, b),
        in_specs=[yin_spec,
                  pl.BlockSpec((1, l, HY_CH), lambda h, bi: (bi, 0, c_g + h)),
                  pl.BlockSpec((1, l, HY_CH), lambda h, bi: (bi, 0, c_z + h)),
                  pl.BlockSpec((3, HY_CH), lambda h, bi: (0, h)),
                  pl.BlockSpec((3, HY_CH), lambda h, bi: (0, (order + 1) * nh + h)),
                  pl.BlockSpec((1, 1, HY_CH), lambda h, bi: (order, 0, h)),
                  pl.BlockSpec((n, HY_CH), lambda h, bi: (0, order * nh + h), pipeline_mode=once),
                  pl.BlockSpec((n, HY_CH), lambda h, bi: (0, order * nh + h), pipeline_mode=once),
                  pl.BlockSpec(f1.shape, lambda h, bi: (0, 0, 0), pipeline_mode=once),
                  pl.BlockSpec(fblk.shape, lambda h, bi: (0, 0), pipeline_mode=once),
                  pl.BlockSpec(fiblk.shape, lambda h, bi: (0, 0), pipeline_mode=once),
                  pl.BlockSpec(hinv.shape, lambda h, bi: (0, 0, 0), pipeline_mode=once)],
        out_specs=pl.BlockSpec((1, l, HY_CH), lambda h, bi: (bi, 0, h)),
        out_shape=jax.ShapeDtypeStruct((b, l, HY_WIDTH), bf16 if last else f32),
        scratch_shapes=[pltpu.VMEM((l + 16, HY_CH), f32), pltpu.VMEM((l, HY_CH), f32), pltpu.VMEM((l, HY_CH), f32),
                        pltpu.VMEM((l, HY_CH), f32),
                        pltpu.VMEM((FFT_N2 * sa, HY_CH), f32), pltpu.VMEM((FFT_N2 * sa, HY_CH), f32),
                        pltpu.VMEM((n1 * sb, HY_CH), f32), pltpu.VMEM((n1 * sb, HY_CH), f32)],
        compiler_params=pltpu.CompilerParams(dimension_semantics=("arbitrary", "arbitrary"),
                                             vmem_limit_bytes=56 * 1024 * 1024),
        name=f"hyena_order{order}",
    )(yin, px, px, conv_w, conv_w, skip.reshape(HY_ORDER, 1, HY_WIDTH), hr, hi, f1, fblk, fiblk, hinv)


def hyena_latent(px, conv_w, skip, w1, b1, w2, b2, w3, b3, freq, wout):
    l = px.shape[1]
    kern = hyena_filter_circular(l, w1, b1, w2, b2, w3, b3, freq, wout)
    hr, hi = filter_spectrum(kern)
    y1 = hyena_order(px, px, hr, hi, conv_w, skip, 0, True, False)
    return hyena_order(y1, px, hr, hi, conv_w, skip, 1, False, True)


@functools.lru_cache(maxsize=None)
def _dense_dft_consts(l):
    n = 2 * l
    k = np.arange(n)
    ang = -2.0 * np.pi * np.outer(k, np.arange(n)) / n
    fwd = np.concatenate([np.cos(ang), np.sin(ang)], axis=0)
    angi = 2.0 * np.pi * np.outer(np.arange(l), k) / n
    inv = np.concatenate([np.cos(angi), -np.sin(angi)], axis=1) / n
    return fwd, inv


def _hy_ctx_kernel(p_ref, kern_ref, cw_ref, skip_ref, fwd_ref, inv_ref, o_ref, pad_ref, t_ref, *, l):
    n = 2 * l
    dot = functools.partial(jnp.dot, preferred_element_type=f32, precision=HIGHEST)
    w = HY_WIDTH

    def conv3(col):
        pad_ref[0:8, :] = jnp.zeros((8, w), f32)
        pad_ref[l + 8:l + 16, :] = jnp.zeros((8, w), f32)
        pad_ref[8:8 + l, :] = p_ref[0, :, col:col + w].astype(f32)
        cw = cw_ref[:, col - C_HY_VX:col - C_HY_VX + w].astype(f32)
        return cw[0:1] * pad_ref[7:7 + l, :] + cw[1:2] * pad_ref[8:8 + l, :] + cw[2:3] * pad_ref[9:9 + l, :]

    y = conv3(C_HY_VX)
    for o in range(HY_ORDER):
        gate = conv3(C_HY_VX + (o + 1) * w)
        hh = dot(fwd_ref[...], kern_ref[:, o * w:(o + 1) * w])
        xx = dot(fwd_ref[:, :l], y)
        xr, xi, hr, hi = xx[:n], xx[n:], hh[:n], hh[n:]
        t_ref[0:n, :] = xr * hr - xi * hi
        t_ref[n:2 * n, :] = xr * hi + xi * hr
        conv = dot(inv_ref[...], t_ref[...])
        y = gate * (conv + y * skip_ref[o:o + 1, :].astype(f32))
    z = p_ref[0, :, C_HY_Z:C_HY_Z + w].astype(f32)
    o_ref[0] = (y * (z * jax.nn.sigmoid(z))).astype(o_ref.dtype)


def hyena_ctx(pc, conv_w, skip, w1, b1, w2, b2, w3, b3, freq, wout):
    b, lc, _ = pc.shape
    n = 2 * lc
    kern = hyena_filter_circular(lc, w1, b1, w2, b2, w3, b3, freq, wout)
    fwd, inv = [jnp.asarray(a, f32) for a in _dense_dft_consts(lc)]
    full = lambda a: pl.BlockSpec(a.shape, lambda bi: (0,) * a.ndim)
    return pl.pallas_call(
        functools.partial(_hy_ctx_kernel, l=lc),
        grid=(b,),
        in_specs=[pl.BlockSpec((1, lc, MAIN_W), lambda bi: (bi, 0, 0)), full(kern), full(conv_w), full(skip),
                  full(fwd), full(inv)],
        out_specs=pl.BlockSpec((1, lc, HY_WIDTH), lambda bi: (bi, 0, 0)),
        out_shape=jax.ShapeDtypeStruct((b, lc, HY_WIDTH), bf16),
        scratch_shapes=[pltpu.VMEM((lc + 16, HY_WIDTH), f32), pltpu.VMEM((2 * n, HY_WIDTH), f32)],
        compiler_params=pltpu.CompilerParams(dimension_semantics=("arbitrary",), vmem_limit_bytes=VMEM_LIMIT),
        name="hyena_ctx",
    )(pc, kern, conv_w, skip, fwd, inv)


DN_PREP_ROWS = 256
DN_SUB_ROWS = 64
DN_HALO = 8


@functools.lru_cache(maxsize=None)
def _rope_tables(l):
    half = DN_HEAD_DIM // 2
    nf = half // 2
    t = np.arange(l)
    inv = (np.float32(ROPE_BASE) ** (-np.arange(nf, dtype=np.float32) / nf)).astype(np.float32)
    ang_r = ((t // GRID_W).astype(np.float32)[:, None] * inv[None, :]).astype(np.float32)
    ang_c = ((t % GRID_W).astype(np.float32)[:, None] * inv[None, :]).astype(np.float32)
    cos = np.concatenate([np.cos(ang_r), np.cos(ang_r), np.cos(ang_c), np.cos(ang_c)], axis=1)
    sin = np.concatenate([-np.sin(ang_r), np.sin(ang_r), -np.sin(ang_c), np.sin(ang_c)], axis=1)
    return cos.astype(np.float32), sin.astype(np.float32)


def _dn_prep_kernel(*refs, rope, n_tiles):
    (q_ref, qp_ref, qn_ref, k_ref, kp_ref, kn_ref, v_ref, vp_ref, vn_ref, g_ref, cw_ref, cos_ref, sin_ref, al_ref,
     dt_ref, qo_ref, ko_ref, vo_ref, go_ref, pad_ref) = refs
    i = pl.program_id(1)
    t = DN_PREP_ROWS
    w = DN_WIDTH
    lane = lax.broadcasted_iota(jnp.int32, (DN_SUB_ROWS, w), 1)
    first_half = (lane % (DN_HEAD_DIM // 2)) < (DN_HEAD_DIM // 4)
    for idx, (m_ref, p_ref, n_ref, o_ref) in enumerate(((q_ref, qp_ref, qn_ref, qo_ref), (k_ref, kp_ref, kn_ref, ko_ref),
                                                        (v_ref, vp_ref, vn_ref, vo_ref))):
        pad_ref[0:DN_HALO, :] = jnp.where(i == 0, 0.0, p_ref[0].astype(f32))
        pad_ref[DN_HALO:DN_HALO + t, :] = m_ref[0].astype(f32)
        pad_ref[DN_HALO + t:2 * DN_HALO + t, :] = jnp.where(i == n_tiles - 1, 0.0, n_ref[0].astype(f32))
        cw = cw_ref[:, idx * w:(idx + 1) * w].astype(f32)

        def sub(s, c):
            r = pl.multiple_of(s * DN_SUB_ROWS, DN_SUB_ROWS)
            blk = pad_ref[pl.ds(r, DN_SUB_ROWS + 2 * DN_HALO), :]
            base = DN_HALO - DN_CONV_W // 2
            y = cw[0:1] * blk[base:base + DN_SUB_ROWS]
            for j in range(1, DN_CONV_W):
                y = y + cw[j:j + 1] * blk[base + j:base + j + DN_SUB_ROWS]
            y = y * jax.nn.sigmoid(y)
            if idx < 2:
                if rope:
                    cos = jnp.concatenate([cos_ref[pl.ds(r, DN_SUB_ROWS), :]] * DN_HEADS, axis=1)
                    sin = jnp.concatenate([sin_ref[pl.ds(r, DN_SUB_ROWS), :]] * DN_HEADS, axis=1)
                    q4 = DN_HEAD_DIM // 4
                    swapped = jnp.where(first_half, pltpu.roll(y, w - q4, 1), pltpu.roll(y, q4, 1))
                    y = y * cos + swapped * sin
                outs = []
                for h in range(DN_HEADS):
                    yh = y[:, h * DN_HEAD_DIM:(h + 1) * DN_HEAD_DIM]
                    nrm = lax.rsqrt(jnp.sum(yh * yh, axis=-1, keepdims=True) + NORM_EPS)
                    if idx == 0:
                        nrm = nrm * (DN_HEAD_DIM ** -0.5)
                    outs.append(yh * nrm)
                y = jnp.concatenate(outs, axis=1)
            o_ref[0, pl.ds(r, DN_SUB_ROWS), :] = y.astype(o_ref.dtype)
            return c
        lax.fori_loop(0, t // DN_SUB_ROWS, sub, 0)

    g = g_ref[0]
    glane = lax.broadcasted_iota(jnp.int32, g.shape, 1)
    xa = g + dt_ref[...]
    softplus = jnp.maximum(xa, 0.0) + jnp.log1p(jnp.exp(-jnp.abs(xa)))
    go_ref[0] = jnp.where(glane < 2 * DN_HEADS, jax.nn.sigmoid(g), -jnp.exp(al_ref[...]) * softplus)


def dn_prep(p, gates_raw, conv_w, a_log, dt_bias, rope):
    b, l, _ = p.shape
    t = DN_PREP_ROWS
    w = DN_WIDTH
    n_tiles = l // t
    hb = t // DN_HALO
    cblk = C_DN_QKV // w
    cos, sin = [jnp.asarray(a) for a in _rope_tables(l)]
    pad = jnp.zeros((2 * DN_HEADS,), f32)
    rest = jnp.zeros((GATE_W - 4 * DN_HEADS,), f32)
    al = jnp.concatenate([pad, a_log.reshape(-1), rest])[None]
    dt = jnp.concatenate([pad, dt_bias.reshape(-1), rest])[None]

    def slab(j):
        return [pl.BlockSpec((1, t, w), lambda bi, i: (bi, i, cblk + j)),
                pl.BlockSpec((1, DN_HALO, w), lambda bi, i: (bi, jnp.maximum(i * hb - 1, 0), cblk + j)),
                pl.BlockSpec((1, DN_HALO, w), lambda bi, i: (bi, jnp.minimum((i + 1) * hb, l // DN_HALO - 1), cblk + j))]

    row = lambda a: pl.BlockSpec(a.shape, lambda bi, i: (0, 0))
    tab = pl.BlockSpec((t, DN_HEAD_DIM), lambda bi, i: (i, 0))
    out_blk = pl.BlockSpec((1, t, w), lambda bi, i: (bi, i, 0))
    return pl.pallas_call(
        functools.partial(_dn_prep_kernel, rope=rope, n_tiles=n_tiles),
        grid=(b, n_tiles),
        in_specs=slab(0) + slab(1) + slab(2) + [
            pl.BlockSpec((1, t, GATE_W), lambda bi, i: (bi, i, 0)), row(conv_w), tab, tab, row(al), row(dt)],
        out_specs=[out_blk, out_blk, out_blk, pl.BlockSpec((1, t, GATE_W), lambda bi, i: (bi, i, 0))],
        out_shape=[jax.ShapeDtypeStruct((b, l, w), bf16)] * 3 + [jax.ShapeDtypeStruct((b, l, GATE_W), f32)],
        scratch_shapes=[pltpu.VMEM((t + 2 * DN_HALO, w), f32)],
        compiler_params=pltpu.CompilerParams(dimension_semantics=("arbitrary", "arbitrary"),
                                             vmem_limit_bytes=VMEM_LIMIT),
        name="dn_prep",
    )(p, p, p, p, p, p, p, p, p, gates_raw, conv_w, cos, sin, al, dt)


DN_GROUP = 4
DN_GROUP_ROWS = DN_GROUP * DN_CHUNK
DN_UNROLL = 2


def _bdot(a, b):
    return jnp.dot(a.astype(bf16), b.astype(bf16), preferred_element_type=f32)


def _dn_chunk_kernel(qc_ref, kc_ref, vc_ref, gc_ref, qx_ref, kx_ref, vx_ref, gx_ref, u_ref, wq_ref, ak_ref, eg_ref):
    g = pl.program_id(1)
    c = DN_CHUNK
    dh = DN_HEAD_DIM
    is_ctx = g == 0
    ri = lax.broadcasted_iota(jnp.int32, (c, c), 0)
    ci = lax.broadcasted_iota(jnp.int32, (c, c), 1)
    eye = (ri == ci).astype(f32)
    er = lax.broadcasted_iota(jnp.int32, (GATE_W, GATE_W), 0)
    ec = lax.broadcasted_iota(jnp.int32, (GATE_W, GATE_W), 1)
    eye_g = (er == ec).astype(f32)
    eye_b = eye_g.astype(bf16)
    hdot = functools.partial(jnp.dot, preferred_element_type=f32, precision=HIGHEST)

    masks = [((ri <= ci) if d else (ri >= ci), (ri < ci) if d else (ri > ci)) for d in range(2)]

    def body(it, carry):
        units = []
        for jj in range(DN_UNROLL):
            j = it * DN_UNROLL + jj
            r = pl.multiple_of(j * c, c)
            pick = lambda a_ref, b_ref: jnp.where(is_ctx, a_ref[0, pl.ds(r, c), :], b_ref[0, pl.ds(r, c), :])
            q, k, v, gates = pick(qc_ref, qx_ref), pick(kc_ref, kx_ref), pick(vc_ref, vx_ref), pick(gc_ref, gx_ref)
            gtot = jnp.sum(gates, axis=0, keepdims=True)
            qks = [lax.dot_general(q[:, h * dh:(h + 1) * dh], k[:, h * dh:(h + 1) * dh], NT_DIMS,
                                   preferred_element_type=f32) for h in range(DN_HEADS)]
            for d in range(2):
                incl, strict = masks[d]
                gcs = hdot(incl.astype(f32), gates)
                gcs_t = lax.dot_general(eye_g, gcs, NT_DIMS, preferred_element_type=f32, precision=HIGHEST)
                egs = []
                for h in range(DN_HEADS):
                    hs = slice(h * dh, (h + 1) * dh)
                    col = d * DN_HEADS + h
                    gl = 2 * DN_HEADS + col
                    kh, vh, qh = k[:, hs].astype(f32), v[:, hs].astype(f32), q[:, hs].astype(f32)
                    beta = gates[:, col:col + 1]
                    gcol = gcs[:, gl:gl + 1]
                    glast = gtot[:, gl:gl + 1]
                    decay = jnp.where(incl, jnp.exp(jnp.minimum(gcol - gcs_t[gl:gl + 1, :], 0.0)), 0.0)
                    kb = kh * beta
                    kk = lax.dot_general(kb.astype(bf16), k[:, hs], NT_DIMS, preferred_element_type=f32)
                    eg = jnp.exp(gcol)
                    wq_ref[d, 0, j, h, c:2 * c, :] = (qh * eg).astype(bf16)
                    ak_ref[d, 0, j, h, 0:c, :] = jnp.where(incl, qks[h] * decay, 0.0).astype(bf16)
                    kdec = (kh * jnp.exp(glast - gcol)).astype(bf16)
                    ak_ref[d, 0, j, h, c:c + dh, :] = lax.dot_general(eye_b, kdec, NT_DIMS,
                                                                      preferred_element_type=f32).astype(bf16)
                    egs.append(jnp.broadcast_to(jnp.exp(glast), (1, GATE_W)))
                    units.append(dict(j=j, d=d, h=h, n=-jnp.where(strict, kk * decay, 0.0),
                                      rhs=jnp.concatenate([vh * beta, kb * eg], axis=1).astype(bf16)))
                eg_ref[d, 0, j] = jnp.concatenate(egs + [jnp.zeros((8 - DN_HEADS, GATE_W), f32)], axis=0)
        for un in units:
            un["t"] = eye + un["n"]
        for _ in range(int(math.log2(c)) - 1):
            for un in units:
                un["n"] = _bdot(un["n"], un["n"])
            for un in units:
                un["t"] = un["t"] + _bdot(un["t"], un["n"])
        for un in units:
            sol = _bdot(un["t"], un["rhs"])
            u_ref[un["d"], 0, un["j"], un["h"]] = sol[:, :dh].astype(bf16)
            wq_ref[un["d"], 0, un["j"], un["h"], 0:c, :] = sol[:, dh:].astype(bf16)
        return carry
    lax.fori_loop(0, DN_GROUP // DN_UNROLL, body, 0)


def dn_chunks(qc, kc, vc, gc, qx, kx, vx, gx):
    b, lc, w = qc.shape
    l = qx.shape[1]
    assert lc == DN_GROUP_ROWS and l % DN_GROUP_ROWS == 0
    ng = 1 + l // DN_GROUP_ROWS
    nch = ng * DN_GROUP
    c, dh, nhd = DN_CHUNK, DN_HEAD_DIM, DN_HEADS
    cmap = lambda bi, g: (bi, 0, 0)
    xmap = lambda bi, g: (bi, jnp.maximum(g - 1, 0), 0)
    blk = lambda width, m: pl.BlockSpec((1, DN_GROUP_ROWS, width), m)
    omap = lambda bi, g: (0, bi, g, 0, 0, 0)
    return pl.pallas_call(
        _dn_chunk_kernel,
        grid=(b, ng),
        in_specs=[blk(w, cmap), blk(w, cmap), blk(w, cmap), blk(GATE_W, cmap),
                  blk(w, xmap), blk(w, xmap), blk(w, xmap), blk(GATE_W, xmap)],
        out_specs=[pl.BlockSpec((2, 1, DN_GROUP, nhd, c, dh), omap),
                   pl.BlockSpec((2, 1, DN_GROUP, nhd, 2 * c, dh), omap),
                   pl.BlockSpec((2, 1, DN_GROUP, nhd, c + dh, c), omap),
                   pl.BlockSpec((2, 1, DN_GROUP, 8, GATE_W), lambda bi, g: (0, bi, g, 0, 0))],
        out_shape=[jax.ShapeDtypeStruct((2, b, nch, nhd, c, dh), bf16),
                   jax.ShapeDtypeStruct((2, b, nch, nhd, 2 * c, dh), bf16),
                   jax.ShapeDtypeStruct((2, b, nch, nhd, c + dh, c), bf16),
                   jax.ShapeDtypeStruct((2, b, nch, 8, GATE_W), f32)],
        compiler_params=pltpu.CompilerParams(dimension_semantics=("arbitrary", "arbitrary"),
                                             vmem_limit_bytes=VMEM_LIMIT),
        name="dn_chunks",
    )(qc, kc, vc, gc, qx, kx, vx, gx)


def _bwd_group(s, ng):
    return jnp.where(s == 0, 0, ng - s)


def _dn_serial_kernel(uf_ref, wqf_ref, akf_ref, egf_ref, ub_ref, wqb_ref, akb_ref, egb_ref, of_ref, ob_ref, s_ref):
    c = DN_CHUNK
    dh = DN_HEAD_DIM

    @pl.when(pl.program_id(1) == 0)
    def _():
        s_ref[...] = jnp.zeros(s_ref.shape, f32)

    dirs = ((uf_ref, wqf_ref, akf_ref, egf_ref, of_ref), (ub_ref, wqb_ref, akb_ref, egb_ref, ob_ref))
    states = [[s_ref[d, h] for h in range(DN_HEADS)] for d in range(2)]
    units = [(d, h) for d in range(2) for h in range(DN_HEADS)]
    for jj in range(DN_GROUP):
        js = (jj, DN_GROUP - 1 - jj)
        r1 = {(d, h): jnp.dot(dirs[d][1][0, 0, js[d], h], states[d][h].astype(bf16), preferred_element_type=f32)
              for d, h in units}
        r2 = {}
        for d, h in units:
            v_new = dirs[d][0][0, 0, js[d], h].astype(f32) - r1[d, h][:c]
            r2[d, h] = jnp.dot(dirs[d][2][0, 0, js[d], h], v_new.astype(bf16), preferred_element_type=f32)
        for d, h in units:
            j = js[d]
            dirs[d][4][0, j * c:(j + 1) * c, h * dh:(h + 1) * dh] = r1[d, h][c:] + r2[d, h][:c]
            states[d][h] = states[d][h] * dirs[d][3][0, 0, j, h:h + 1, :] + r2[d, h][c:]
    for d in range(2):
        for h in range(DN_HEADS):
            s_ref[d, h] = states[d][h]


def dn_serial(u, wq, ak, eg):
    _, b, nch, nhd, c, dh = u.shape
    ng = nch // DN_GROUP
    w = nhd * dh
    fmap6 = lambda bi, s: (0, bi, s, 0, 0, 0)
    bmap6 = lambda bi, s: (1, bi, _bwd_group(s, ng), 0, 0, 0)
    fmap5 = lambda bi, s: (0, bi, s, 0, 0)
    bmap5 = lambda bi, s: (1, bi, _bwd_group(s, ng), 0, 0)
    blk6 = lambda a, m: pl.BlockSpec((1, 1, DN_GROUP) + a.shape[3:], m)
    egblk = lambda m: pl.BlockSpec((1, 1, DN_GROUP, 8, GATE_W), m)
    return pl.pallas_call(
        _dn_serial_kernel,
        grid=(b, ng),
        in_specs=[blk6(u, fmap6), blk6(wq, fmap6), blk6(ak, fmap6), egblk(fmap5),
                  blk6(u, bmap6), blk6(wq, bmap6), blk6(ak, bmap6), egblk(bmap5)],
        out_specs=[pl.BlockSpec((1, DN_GROUP_ROWS, w), lambda bi, s: (bi, s, 0)),
                   pl.BlockSpec((1, DN_GROUP_ROWS, w), lambda bi, s: (bi, _bwd_group(s, ng), 0))],
        out_shape=[jax.ShapeDtypeStruct((b, nch * c, w), f32)] * 2,
        scratch_shapes=[pltpu.VMEM((2, nhd, dh, dh), f32)],
        compiler_params=pltpu.CompilerParams(dimension_semantics=("arbitrary", "arbitrary"),
                                             vmem_limit_bytes=VMEM_LIMIT),
        name="dn_serial",
    )(u, wq, ak, eg, u, wq, ak, eg)


def _dn_out_kernel(of_ref, ob_ref, z_ref, nw_ref, y_ref):
    o = of_ref[0] + ob_ref[0]
    z = z_ref[0].astype(f32)
    nw = nw_ref[...].astype(f32)
    outs = []
    for h in range(DN_HEADS):
        oh = o[:, h * DN_HEAD_DIM:(h + 1) * DN_HEAD_DIM]
        outs.append(oh * lax.rsqrt(jnp.mean(oh * oh, axis=-1, keepdims=True) + NORM_EPS) * nw)
    y_ref[0] = (jnp.concatenate(outs, axis=1) * (z * jax.nn.sigmoid(z))).astype(y_ref.dtype)


def dn_output(o_f, o_b, p, norm_w, first_block):
    b, l, _ = p.shape
    w = DN_WIDTH
    tm = DN_GROUP_ROWS
    omap = lambda bi, i: (bi, first_block + i, 0)
    return pl.pallas_call(
        _dn_out_kernel,
        grid=(b, l // tm),
        in_specs=[pl.BlockSpec((1, tm, w), omap), pl.BlockSpec((1, tm, w), omap),
                  pl.BlockSpec((1, tm, w), lambda bi, i: (bi, i, C_DN_Z // w)),
                  pl.BlockSpec((1, DN_HEAD_DIM), lambda bi, i: (0, 0))],
        out_specs=pl.BlockSpec((1, tm, w), lambda bi, i: (bi, i, 0)),
        out_shape=jax.ShapeDtypeStruct((b, l, w), bf16),
        compiler_params=pltpu.CompilerParams(dimension_semantics=("arbitrary", "arbitrary"),
                                             vmem_limit_bytes=VMEM_LIMIT),
        name="dn_output",
    )(o_f, o_b, p, norm_w.reshape(1, DN_HEAD_DIM))


def _pack_w_in(w_in):
    depth, d, _ = w_in.shape
    main = jnp.concatenate([w_in[:, :, :3072], w_in[:, :, 3088:4112]], axis=-1).astype(bf16)
    gate = jnp.concatenate([w_in[:, :, 3072:3088], jnp.zeros((depth, d, GATE_W - 16), w_in.dtype)],
                           axis=-1).astype(bf16)
    return main, gate


def kernel(x, c, ctx, c_ctx, w_ada, b_ada, g_pre, g_post, w_in, w_out, na_rpb, dn_conv, dn_a_log, dn_dt_bias,
           dn_norm, hy_conv, hy_w1, hy_b1, hy_w2, hy_b2, hy_w3, hy_b3, hy_freq, hy_wout, hy_skip):
    bsz, l, d = x.shape
    lc = ctx.shape[1]
    depth = w_in.shape[0]
    cond = jnp.concatenate([c, c_ctx[None], jnp.zeros((7, d), f32)], axis=0)
    mod = modulation_all(cond, w_ada, b_ada)
    w_main, w_gate = _pack_w_in(w_in)
    w_out_b = w_out.astype(bf16)

    for i in range(depth):
        last = i == depth - 1
        shift_x, scale_x, gate_x = [mod[i, :bsz, j * d:(j + 1) * d][:, None] for j in range(3)]
        shift_c, scale_c, gate_c = [mod[i, bsz:bsz + 1, j * d:(j + 1) * d][:, None] for j in range(3)]
        px, gx = in_projection(x, g_pre[i], scale_x, shift_x, w_main[i], w_gate[i], 512)
        pc, gc = in_projection(ctx, g_pre[i], scale_c, shift_c, w_main[i], w_gate[i], 256)

        hy_args = (hy_conv[i], hy_skip[i], hy_w1[i], hy_b1[i], hy_w2[i], hy_b2[i], hy_w3[i], hy_b3[i], hy_freq[i],
                   hy_wout[i])
        dn_args = (dn_conv[i], dn_a_log[i], dn_dt_bias[i])

        out_a_x = na_attention(px, pc, na_rpb[i])

        dn_c = dn_prep(pc, gc, *dn_args, False)
        dn_x = dn_prep(px, gx, *dn_args, True)
        do_f, do_b = dn_serial(*dn_chunks(*dn_c, *dn_x))
        out_b_x = dn_output(do_f, do_b, px, dn_norm[i], lc // DN_GROUP_ROWS)

        out_c_x = hyena_latent(px, *hy_args)

        new_x = out_projection(out_a_x, out_b_x, out_c_x, w_out_b[i], g_post[i], gate_x, x, 512)

        if not last:
            out_a_c = ctx_attention(pc)
            out_b_c = dn_output(do_f, do_b, pc, dn_norm[i], 0)
            out_c_c = hyena_ctx(pc, *hy_args)
            ctx = out_projection(out_a_c, out_b_c, out_c_c, w_out_b[i], g_post[i], gate_c, ctx, lc)
        x = new_x
    return x
```

```python
import functools
import math

import numpy as np
import jax
import jax.numpy as jnp
from jax import lax
from jax.experimental import pallas as pl
from jax.experimental.pallas import tpu as pltpu

D_MODEL = 1024
GRID_W = 64
NA_HEAD_DIM = 64
NA_WIDTH = 256
NA_HEADS = 4
NA_WIN_ROWS = 8
NA_WIN_COLS = 16
DN_HEAD_DIM = 128
DN_WIDTH = 512
DN_HEADS = 4
DN_CONV_W = 5
DN_CHUNK = 64
HY_WIDTH = 256
HY_ORDER = 2
HY_EMB = 33
HY_DECAY_TARGET = 1e-2
HY_FAST_DECAY = 0.3
HY_SLOW_DECAY = 1.5
ROPE_BASE = 10000.0
NORM_EPS = 1e-6

C_NA_QKV, C_NA_Z, C_DN_QKV, C_DN_Z, C_HY_VX, C_HY_Z = 0, 768, 1024, 2560, 3072, 3840
MAIN_W = 4096
GATE_W = 128
LANE = 128
VMEM_LIMIT = 52 * 1024 * 1024

f32 = jnp.float32
bf16 = jnp.bfloat16
HIGHEST = lax.Precision.HIGHEST
NT_DIMS = (((1,), (1,)), ((), ()))
TN_DIMS = (((0,), (0,)), ((), ()))


def _mod_kernel(c_ref, w_ref, b_ref, o_ref):
    c = c_ref[...]
    a = c * jax.nn.sigmoid(c)
    o_ref[0] = jnp.dot(a, w_ref[0], preferred_element_type=f32, precision=lax.Precision.HIGHEST) + b_ref[0]


def modulation_all(cond, w_ada, b_ada):
    depth, d, d3 = w_ada.shape
    r = cond.shape[0]
    tn = 512
    return pl.pallas_call(
        _mod_kernel,
        grid=(depth, d3 // tn),
        in_specs=[pl.BlockSpec((r, d), lambda i, j: (0, 0)),
                  pl.BlockSpec((1, d, tn), lambda i, j: (i, 0, j)),
                  pl.BlockSpec((1, 1, tn), lambda i, j: (i, 0, j))],
        out_specs=pl.BlockSpec((1, r, tn), lambda i, j: (i, 0, j)),
        out_shape=jax.ShapeDtypeStruct((depth, r, d3), f32),
        compiler_params=pltpu.CompilerParams(dimension_semantics=("arbitrary", "arbitrary"),
                                             vmem_limit_bytes=VMEM_LIMIT),
        name="modulation",
    )(cond, w_ada, b_ada.reshape(depth, 1, d3))


def _inproj_kernel(x_ref, g_ref, sc_ref, sh_ref, w_ref, wg_ref, o_ref, og_ref):
    x = x_ref[0]
    ms = jnp.mean(x * x, axis=-1, keepdims=True)
    h = x * lax.rsqrt(ms + NORM_EPS) * g_ref[...]
    h = (h * (1.0 + sc_ref[0]) + sh_ref[0]).astype(bf16)
    for j in range(MAIN_W // 1024):
        o_ref[0, :, j * 1024:(j + 1) * 1024] = jnp.dot(
            h, w_ref[:, j * 1024:(j + 1) * 1024], preferred_element_type=f32).astype(bf16)
    og_ref[0] = jnp.dot(h, wg_ref[...], preferred_element_type=f32)


def in_projection(x, g, scale, shift, w_main, w_gate, tm):
    b, l, d = x.shape
    per_b = scale.shape[0] == b
    mod_map = (lambda bi, i: (bi, 0, 0)) if per_b else (lambda bi, i: (0, 0, 0))
    return pl.pallas_call(
        _inproj_kernel,
        grid=(b, l // tm),
        in_specs=[pl.BlockSpec((1, tm, d), lambda bi, i: (bi, i, 0)),
                  pl.BlockSpec((1, d), lambda bi, i: (0, 0)),
                  pl.BlockSpec((1, 1, d), mod_map),
                  pl.BlockSpec((1, 1, d), mod_map),
                  pl.BlockSpec((d, MAIN_W), lambda bi, i: (0, 0)),
                  pl.BlockSpec((d, GATE_W), lambda bi, i: (0, 0))],
        out_specs=[pl.BlockSpec((1, tm, MAIN_W), lambda bi, i: (bi, i, 0)),
                   pl.BlockSpec((1, tm, GATE_W), lambda bi, i: (bi, i, 0))],
        out_shape=[jax.ShapeDtypeStruct((b, l, MAIN_W), bf16),
                   jax.ShapeDtypeStruct((b, l, GATE_W), f32)],
        compiler_params=pltpu.CompilerParams(dimension_semantics=("arbitrary", "arbitrary"),
                                             vmem_limit_bytes=VMEM_LIMIT),
        name="in_projection",
    )(x, g.reshape(1, d), scale, shift, w_main, w_gate)


def _outproj_kernel(a_ref, b_ref, c_ref, w_ref, g_ref, gate_ref, x_ref, o_ref):
    y = (jnp.dot(a_ref[0], w_ref[0:NA_WIDTH], preferred_element_type=f32)
         + jnp.dot(b_ref[0], w_ref[NA_WIDTH:NA_WIDTH + DN_WIDTH], preferred_element_type=f32)
         + jnp.dot(c_ref[0], w_ref[NA_WIDTH + DN_WIDTH:], preferred_element_type=f32))
    ms = jnp.mean(y * y, axis=-1, keepdims=True)
    yn = y * lax.rsqrt(ms + NORM_EPS) * g_ref[...]
    o_ref[0] = x_ref[0] + gate_ref[0] * yn


def out_projection(out_a, out_b, out_c, w_out, g_post, gate, x, tm):
    b, l, d = x.shape
    per_b = gate.shape[0] == b
    mod_map = (lambda bi, i: (bi, 0, 0)) if per_b else (lambda bi, i: (0, 0, 0))
    part = lambda a: pl.BlockSpec((1, tm, a.shape[-1]), lambda bi, i: (bi, i, 0))
    return pl.pallas_call(
        _outproj_kernel,
        grid=(b, l // tm),
        in_specs=[part(out_a), part(out_b), part(out_c),
                  pl.BlockSpec((d, d), lambda bi, i: (0, 0)),
                  pl.BlockSpec((1, d), lambda bi, i: (0, 0)),
                  pl.BlockSpec((1, 1, d), mod_map),
                  pl.BlockSpec((1, tm, d), lambda bi, i: (bi, i, 0))],
        out_specs=pl.BlockSpec((1, tm, d), lambda bi, i: (bi, i, 0)),
        out_shape=jax.ShapeDtypeStruct((b, l, d), f32),
        compiler_params=pltpu.CompilerParams(dimension_semantics=("arbitrary", "arbitrary"),
                                             vmem_limit_bytes=VMEM_LIMIT),
        name="out_projection",
    )(out_a, out_b, out_c, w_out, g_post.reshape(1, d), gate, x)


NA_TQ_ROWS = 4
NA_TQ = NA_TQ_ROWS * GRID_W
NA_KV_ROWS = NA_TQ_ROWS + NA_WIN_ROWS
NA_KV = NA_KV_ROWS * GRID_W
MASK_VALUE = -1e30


def _na_window_start(t, rows):
    return jnp.clip(t * NA_TQ_ROWS - NA_WIN_ROWS // 2, 0, rows - NA_KV_ROWS)


def _na_bias_kernel(rpb_ref, sel_ref, mask_ref, o_ref):
    o_ref[...] = jnp.dot(rpb_ref[...], sel_ref[...], preferred_element_type=f32, precision=HIGHEST) + mask_ref[...]


def na_bias_tables(rpb, rows):
    nh, ndr, ndc = rpb.shape
    col = np.arange(GRID_W)
    dc = np.clip(col[None, :] - col[:, None], -(NA_WIN_COLS - 1), NA_WIN_COLS - 1) + (NA_WIN_COLS - 1)
    c0 = np.clip(col - NA_WIN_COLS // 2, 0, GRID_W - NA_WIN_COLS)
    in_win = (col[None, :] >= c0[:, None]) & (col[None, :] < c0[:, None] + NA_WIN_COLS)
    sel = np.zeros((LANE, GRID_W * GRID_W), np.float32)
    sel[dc.reshape(-1), np.arange(GRID_W * GRID_W)] = 1.0
    cmask = np.where(in_win, 0.0, MASK_VALUE).astype(np.float32).reshape(1, -1)
    rpb2 = jnp.pad(rpb.astype(f32).reshape(nh * ndr, ndc), ((0, 0), (0, LANE - ndc)))
    tab = pl.pallas_call(
        _na_bias_kernel,
        out_shape=jax.ShapeDtypeStruct((nh * ndr, GRID_W * GRID_W), f32),
        name="na_bias",
    )(rpb2, jnp.asarray(sel), jnp.asarray(cmask)).reshape(nh, ndr, GRID_W, GRID_W)
    masked = jnp.full((nh, GRID_W, GRID_W), MASK_VALUE, f32)
    n_tiles = rows // NA_TQ_ROWS
    tabs = []
    for t in (0, 1, n_tiles - 1):
        r0 = t * NA_TQ_ROWS
        ws = int(np.clip(r0 - NA_WIN_ROWS // 2, 0, rows - NA_KV_ROWS))
        blocks = []
        for qr in range(r0, r0 + NA_TQ_ROWS):
            band0 = int(np.clip(qr - NA_WIN_ROWS // 2, 0, rows - NA_WIN_ROWS))
            blocks.append(jnp.concatenate(
                [tab[:, kr - qr + NA_WIN_ROWS - 1] if band0 <= kr < band0 + NA_WIN_ROWS else masked
                 for kr in range(ws, ws + NA_KV_ROWS)], axis=-1))
        tabs.append(jnp.concatenate(blocks, axis=-2))
    return jnp.stack(tabs)


def _na_kernel(q_ref, k_ref, v_ref, z_ref, kc_ref, vc_ref, bias_ref, o_ref, *, rows):
    t = pl.program_id(1)
    ws = pl.multiple_of(_na_window_start(t, rows) * GRID_W, GRID_W)
    q = q_ref[0]
    kw = k_ref[0, pl.ds(ws, NA_KV), :]
    vw = v_ref[0, pl.ds(ws, NA_KV), :]
    kc = kc_ref[0]
    vc = vc_ref[0]
    outs = []
    for h in range(NA_HEADS):
        sl = slice(h * NA_HEAD_DIM, (h + 1) * NA_HEAD_DIM)
        qh = q[:, sl] * (NA_HEAD_DIM ** -0.5)
        s1 = lax.dot_general(qh, kw[:, sl], NT_DIMS, preferred_element_type=f32) + bias_ref[0, h]
        s2 = lax.dot_general(qh, kc[:, sl], NT_DIMS, preferred_element_type=f32)
        m = jnp.maximum(jnp.max(s1, axis=-1, keepdims=True), jnp.max(s2, axis=-1, keepdims=True))
        p1 = jnp.exp(s1 - m)
        p2 = jnp.exp(s2 - m)
        den = jnp.sum(p1, axis=-1, keepdims=True) + jnp.sum(p2, axis=-1, keepdims=True)
        o = (jnp.dot(p1.astype(bf16), vw[:, sl], preferred_element_type=f32)
             + jnp.dot(p2.astype(bf16), vc[:, sl], preferred_element_type=f32))
        outs.append(o / den)
    z = z_ref[0].astype(f32)
    o = jnp.concatenate(outs, axis=-1) * (z * jax.nn.sigmoid(z))
    o_ref[0] = o.astype(o_ref.dtype)


def na_attention(px, pc, rpb):
    b, l, _ = px.shape
    lc = pc.shape[1]
    rows = l // GRID_W
    n_tiles = rows // NA_TQ_ROWS
    bias = na_bias_tables(rpb, rows)
    w = NA_WIDTH
    cq, ck, cv, cz = [(C_NA_QKV + j * w) // w for j in range(3)] + [C_NA_Z // w]

    def bias_map(bi, t):
        return (jnp.where(t == 0, 0, jnp.where(t == n_tiles - 1, 2, 1)), 0, 0, 0)

    return pl.pallas_call(
        functools.partial(_na_kernel, rows=rows),
        grid=(b, n_tiles),
        in_specs=[pl.BlockSpec((1, NA_TQ, w), lambda bi, t: (bi, t, cq)),
                  pl.BlockSpec((1, l, w), lambda bi, t: (bi, 0, ck)),
                  pl.BlockSpec((1, l, w), lambda bi, t: (bi, 0, cv)),
                  pl.BlockSpec((1, NA_TQ, w), lambda bi, t: (bi, t, cz)),
                  pl.BlockSpec((1, lc, w), lambda bi, t: (bi, 0, ck)),
                  pl.BlockSpec((1, lc, w), lambda bi, t: (bi, 0, cv)),
                  pl.BlockSpec((1, NA_HEADS, NA_TQ, NA_KV), bias_map)],
        out_specs=pl.BlockSpec((1, NA_TQ, w), lambda bi, t: (bi, t, 0)),
        out_shape=jax.ShapeDtypeStruct((b, l, w), bf16),
        compiler_params=pltpu.CompilerParams(dimension_semantics=("arbitrary", "arbitrary"),
                                             vmem_limit_bytes=VMEM_LIMIT),
        name="na_attention",
    )(px, px, px, px, pc, pc, bias)


def _ctx_attn_kernel(q_ref, k_ref, v_ref, z_ref, o_ref):
    q = q_ref[0]
    k = k_ref[0]
    v = v_ref[0]
    outs = []
    for h in range(NA_HEADS):
        sl = slice(h * NA_HEAD_DIM, (h + 1) * NA_HEAD_DIM)
        qh = q[:, sl] * (NA_HEAD_DIM ** -0.5)
        s = lax.dot_general(qh, k[:, sl], NT_DIMS, preferred_element_type=f32)
        p = jnp.exp(s - jnp.max(s, axis=-1, keepdims=True))
        den = jnp.sum(p, axis=-1, keepdims=True)
        outs.append(jnp.dot(p.astype(bf16), v[:, sl], preferred_element_type=f32) / den)
    z = z_ref[0].astype(f32)
    o_ref[0] = (jnp.concatenate(outs, axis=-1) * (z * jax.nn.sigmoid(z))).astype(o_ref.dtype)


def ctx_attention(pc):
    b, lc, _ = pc.shape
    w = NA_WIDTH
    cq, ck, cv, cz = [(C_NA_QKV + j * w) // w for j in range(3)] + [C_NA_Z // w]
    spec = lambda cidx: pl.BlockSpec((1, lc, w), lambda bi: (bi, 0, cidx))
    return pl.pallas_call(
        _ctx_attn_kernel,
        grid=(b,),
        in_specs=[spec(cq), spec(ck), spec(cv), spec(cz)],
        out_specs=pl.BlockSpec((1, lc, w), lambda bi: (bi, 0, 0)),
        out_shape=jax.ShapeDtypeStruct((b, lc, w), bf16),
        compiler_params=pltpu.CompilerParams(dimension_semantics=("arbitrary",), vmem_limit_bytes=VMEM_LIMIT),
        name="ctx_attention",
    )(pc, pc, pc, pc)


HY_CH = 128
FFT_N2 = 128
ROW_CHUNK = 256
STRIDE_PAD = 8
HY_FILT_ROWS = 512
HY_BANDS = (HY_EMB - 1) // 2


def _hy_filter_kernel(fr_ref, w1t_ref, w1c_ref, w1s_ref, b1_ref, w2_ref, b2_ref, w3_ref, b3_ref, freq_ref, wo_ref,
                      dl_ref, o_ref, *, l):
    i = pl.program_id(0)
    n = i * HY_FILT_ROWS + lax.broadcasted_iota(jnp.int32, (HY_FILT_ROWS, 1), 0)
    lag = jnp.where(n < l, n, 2 * l - n).astype(f32)
    t = lag * (1.0 / (l - 1))
    ang = (lag * (2.0 * math.pi / l)) * fr_ref[...]
    dot = functools.partial(jnp.dot, preferred_element_type=f32, precision=HIGHEST)
    pre = t * w1t_ref[...] + dot(jnp.cos(ang), w1c_ref[...]) - dot(jnp.sin(ang), w1s_ref[...]) + b1_ref[...]
    hid = jnp.sin(freq_ref[0:1] * pre)
    hid = jnp.sin(freq_ref[1:2] * (dot(hid, w2_ref[...]) + b2_ref[...]))
    hid = jnp.sin(freq_ref[2:3] * (dot(hid, w3_ref[...]) + b3_ref[...]))
    filt = dot(hid, wo_ref[...])
    window = jnp.exp(-t * dl_ref[...])
    live = jnp.where(n == l, 0.0, 1.0)
    fwd = n < l
    for o in range(HY_ORDER):
        a = filt[:, (2 * o) * HY_WIDTH:(2 * o + 1) * HY_WIDTH]
        b = filt[:, (2 * o + 1) * HY_WIDTH:(2 * o + 2) * HY_WIDTH]
        o_ref[:, o * HY_WIDTH:(o + 1) * HY_WIDTH] = jnp.where(fwd, a, b) * window * live


def hyena_filter_circular(l, w1, b1, w2, b2, w3, b3, freq, wout):
    n = 2 * l
    assert n % HY_FILT_ROWS == 0
    hid = w1.shape[1]
    fr = np.zeros((1, LANE), np.float32)
    fr[0, :HY_BANDS] = np.linspace(1e-4, HY_BANDS - 1, HY_BANDS, dtype=np.float32)
    padrows = lambda w: jnp.concatenate([w, jnp.zeros((LANE - w.shape[0], hid), f32)], axis=0)
    w1t, w1c, w1s = w1[0:1], padrows(w1[1:1 + HY_BANDS]), padrows(w1[1 + HY_BANDS:])
    max_decay = math.log(HY_DECAY_TARGET) / HY_FAST_DECAY
    min_decay = math.log(HY_DECAY_TARGET) / HY_SLOW_DECAY
    deltas = np.abs(np.linspace(min_decay, max_decay, HY_WIDTH, dtype=np.float32))[None]
    full = lambda a: pl.BlockSpec(a.shape, lambda i: (0,) * a.ndim)
    args = [jnp.asarray(fr), w1t, w1c, w1s, b1[None], w2, b2[None], w3, b3[None], freq, wout, jnp.asarray(deltas)]
    return pl.pallas_call(
        functools.partial(_hy_filter_kernel, l=l),
        grid=(n // HY_FILT_ROWS,),
        in_specs=[full(a) for a in args],
        out_specs=pl.BlockSpec((HY_FILT_ROWS, HY_ORDER * HY_WIDTH), lambda i: (i, 0)),
        out_shape=jax.ShapeDtypeStruct((n, HY_ORDER * HY_WIDTH), f32),
        compiler_params=pltpu.CompilerParams(dimension_semantics=("arbitrary",), vmem_limit_bytes=VMEM_LIMIT),
        name="hyena_filter",
    )(*args)


@functools.lru_cache(maxsize=None)
def _fft_consts(n, k_in, k_out):
    n1 = n // FFT_N2
    k1 = np.arange(n1)
    n2 = np.arange(FFT_N2)
    tt = FFT_N2 * np.arange(k_in)[None, None, :] + n2[:, None, None]
    ang = -2.0 * np.pi * (k1[None, :, None] * tt) / n
    f1 = np.concatenate([np.cos(ang), np.sin(ang)], axis=1)
    a2 = -2.0 * np.pi * np.outer(n2, n2) / FFT_N2
    cr, ci = np.cos(a2), np.sin(a2)
    fblk = np.block([[cr, -ci], [ci, cr]])
    fiblk = np.block([[cr, ci], [-ci, cr]])
    to = FFT_N2 * np.arange(k_out)[None, :, None] + n2[:, None, None]
    ango = 2.0 * np.pi * (k1[None, None, :] * to) / n
    hinv = np.concatenate([np.cos(ango), -np.sin(ango)], axis=2) / n
    return f1, fblk, fiblk, hinv


FFT_OUTER_UNROLL = 8
FFT_INNER_UNROLL = 4


def _fft_fwd_stage1(x_ref, f1_ref, ar_ref, ai_ref, k_in, n1, sa, dt, prec):
    def body(it, c):
        n2s = [it * FFT_OUTER_UNROLL + u for u in range(FFT_OUTER_UNROLL)]
        xs = [x_ref[pl.ds(n2, k_in, stride=FFT_N2), :].astype(dt) for n2 in n2s]
        res = [jnp.dot(f1_ref[n2], x, preferred_element_type=f32, precision=prec) for n2, x in zip(n2s, xs)]
        for n2, a in zip(n2s, res):
            off = pl.multiple_of(n2 * sa, 8)
            ar_ref[pl.ds(off, n1), :] = a[:n1]
            ai_ref[pl.ds(off, n1), :] = a[n1:]
        return c
    lax.fori_loop(0, FFT_N2 // FFT_OUTER_UNROLL, body, 0)


def _fft_inner_load(ar_ref, ai_ref, k1s, sa, dt):
    zr = jnp.concatenate([ar_ref[pl.ds(k1, FFT_N2, stride=sa), :] for k1 in k1s], axis=1)
    zi = jnp.concatenate([ai_ref[pl.ds(k1, FFT_N2, stride=sa), :] for k1 in k1s], axis=1)
    return jnp.concatenate([zr, zi], axis=0).astype(dt)


def _filter_fft_kernel(k_ref, f1_ref, fblk_ref, hr_ref, hi_ref, ar_ref, ai_ref, *, n):
    n1 = n // FFT_N2
    sa = n1 + STRIDE_PAD
    ch = k_ref.shape[-1]
    _fft_fwd_stage1(k_ref, f1_ref, ar_ref, ai_ref, n1, n1, sa, f32, HIGHEST)

    def body(it, c):
        k1s = [it * FFT_INNER_UNROLL + u for u in range(FFT_INNER_UNROLL)]
        xx = jnp.dot(fblk_ref[...], _fft_inner_load(ar_ref, ai_ref, k1s, sa, f32), preferred_element_type=f32,
                     precision=HIGHEST)
        for u, k1 in enumerate(k1s):
            off = pl.multiple_of(k1 * FFT_N2, FFT_N2)
            hr_ref[pl.ds(off, FFT_N2), :] = xx[:FFT_N2, u * ch:(u + 1) * ch]
            hi_ref[pl.ds(off, FFT_N2), :] = xx[FFT_N2:, u * ch:(u + 1) * ch]
        return c
    lax.fori_loop(0, n1 // FFT_INNER_UNROLL, body, 0)


def filter_spectrum(kern):
    n, c = kern.shape
    n1 = n // FFT_N2
    f1, fblk, _, _ = _fft_consts(n, n1, 1)
    f1 = jnp.asarray(f1, f32)
    fblk = jnp.asarray(fblk, f32)
    sa = n1 + STRIDE_PAD
    return pl.pallas_call(
        functools.partial(_filter_fft_kernel, n=n),
        grid=(c // HY_CH,),
        in_specs=[pl.BlockSpec((n, HY_CH), lambda j: (0, j)),
                  pl.BlockSpec(f1.shape, lambda j: (0, 0, 0)),
                  pl.BlockSpec(fblk.shape, lambda j: (0, 0))],
        out_specs=[pl.BlockSpec((n, HY_CH), lambda j: (0, j))] * 2,
        out_shape=[jax.ShapeDtypeStruct((n, c), f32)] * 2,
        scratch_shapes=[pltpu.VMEM((FFT_N2 * sa, HY_CH), f32)] * 2,
        compiler_params=pltpu.CompilerParams(dimension_semantics=("arbitrary",), vmem_limit_bytes=VMEM_LIMIT),
        name="filter_spectrum",
    )(kern, f1, fblk)


def _conv3_rows(src_ref, w_ref, pad_ref, dst_ref, l):
    zeros = jnp.zeros((8, src_ref.shape[-1]), f32)
    pad_ref[0:8, :] = zeros
    pad_ref[l + 8:l + 16, :] = zeros

    def cp(i, c):
        r = pl.multiple_of(i * ROW_CHUNK, ROW_CHUNK)
        pad_ref[pl.ds(8 + r, ROW_CHUNK), :] = src_ref[0, pl.ds(r, ROW_CHUNK), :].astype(f32)
        return c
    lax.fori_loop(0, l // ROW_CHUNK, cp, 0)
    w = w_ref[...].astype(f32)

    def cv(i, c):
        r = pl.multiple_of(i * ROW_CHUNK, ROW_CHUNK)
        blk = pad_ref[pl.ds(r, ROW_CHUNK + 16), :]
        dst_ref[pl.ds(r, ROW_CHUNK), :] = (w[0:1] * blk[7:7 + ROW_CHUNK] + w[1:2] * blk[8:8 + ROW_CHUNK]
                                           + w[2:3] * blk[9:9 + ROW_CHUNK])
        return c
    lax.fori_loop(0, l // ROW_CHUNK, cv, 0)


def _hy_order_kernel(yin_ref, graw_ref, z_ref, cwy_ref, cwg_ref, skip_ref, hr_ref, hi_ref, f1_ref, fblk_ref, fiblk_ref,
                     hinv_ref, o_ref, pad_ref, y_ref, gate_ref, c_ref, ar_ref, ai_ref, br_ref, bi_ref,
                     *, l, first, last):
    n = 2 * l
    n1 = n // FFT_N2
    k1n = l // FFT_N2
    sa = n1 + STRIDE_PAD
    sb = FFT_N2 + STRIDE_PAD
    if first:
        _conv3_rows(yin_ref, cwy_ref, pad_ref, y_ref, l)
        src_ref = y_ref
    else:
        src_ref = yin_ref.at[0]
    _conv3_rows(graw_ref, cwg_ref, pad_ref, gate_ref, l)

    _fft_fwd_stage1(src_ref, f1_ref, ar_ref, ai_ref, k1n, n1, sa, bf16, None)

    ch = c_ref.shape[-1]

    def mid(it, c):
        k1s = [it * FFT_INNER_UNROLL + u for u in range(FFT_INNER_UNROLL)]
        xx = jnp.dot(fblk_ref[...], _fft_inner_load(ar_ref, ai_ref, k1s, sa, bf16), preferred_element_type=f32)
        xr, xi = xx[:FFT_N2], xx[FFT_N2:]
        hr = jnp.concatenate([hr_ref[pl.ds(pl.multiple_of(k1 * FFT_N2, FFT_N2), FFT_N2), :] for k1 in k1s], axis=1)
        hi = jnp.concatenate([hi_ref[pl.ds(pl.multiple_of(k1 * FFT_N2, FFT_N2), FFT_N2), :] for k1 in k1s], axis=1)
        yy = jnp.concatenate([xr * hr - xi * hi, xr * hi + xi * hr], axis=0).astype(bf16)
        bb = jnp.dot(fiblk_ref[...], yy, preferred_element_type=f32)
        for u, k1 in enumerate(k1s):
            boff = pl.multiple_of(k1 * sb, 8)
            br_ref[pl.ds(boff, FFT_N2), :] = bb[:FFT_N2, u * ch:(u + 1) * ch]
            bi_ref[pl.ds(boff, FFT_N2), :] = bb[FFT_N2:, u * ch:(u + 1) * ch]
        return c
    lax.fori_loop(0, n1 // FFT_INNER_UNROLL, mid, 0)

    def inv2(it, c):
        n2s = [it * FFT_OUTER_UNROLL + u for u in range(FFT_OUTER_UNROLL)]
        bbs = [jnp.concatenate([br_ref[pl.ds(n2, n1, stride=sb), :], bi_ref[pl.ds(n2, n1, stride=sb), :]],
                               axis=0).astype(bf16) for n2 in n2s]
        res = [jnp.dot(hinv_ref[n2], bb, preferred_element_type=f32) for n2, bb in zip(n2s, bbs)]
        for n2, o in zip(n2s, res):
            c_ref[pl.ds(n2, k1n, stride=FFT_N2), :] = o
        return c
    lax.fori_loop(0, FFT_N2 // FFT_OUTER_UNROLL, inv2, 0)

    skip = skip_ref[0].astype(f32)

    def fin(i, c):
        r = pl.multiple_of(i * ROW_CHUNK, ROW_CHUNK)
        y = src_ref[pl.ds(r, ROW_CHUNK), :]
        y = gate_ref[pl.ds(r, ROW_CHUNK), :] * (c_ref[pl.ds(r, ROW_CHUNK), :] + y * skip)
        if last:
            z = z_ref[0, pl.ds(r, ROW_CHUNK), :].astype(f32)
            y = y * (z * jax.nn.sigmoid(z))
        o_ref[0, pl.ds(r, ROW_CHUNK), :] = y.astype(o_ref.dtype)
        return c
    lax.fori_loop(0, l // ROW_CHUNK, fin, 0)


def hyena_order(yin, px, hr, hi, conv_w, skip, order, first, last):
    b, l, _ = px.shape
    n = 2 * l
    n1 = n // FFT_N2
    k1n = l // FFT_N2
    sa, sb = n1 + STRIDE_PAD, FFT_N2 + STRIDE_PAD
    nh = HY_WIDTH // HY_CH
    f1, fblk, fiblk, hinv = [jnp.asarray(a, bf16) for a in _fft_consts(n, k1n, k1n)]
    cb = lambda col: col // HY_CH
    c_v, c_g, c_z = cb(C_HY_VX), cb(C_HY_VX + (order + 1) * HY_WIDTH), cb(C_HY_Z)
    once = pl.Buffered(1)
    yin_spec = (pl.BlockSpec((1, l, HY_CH), lambda h, bi: (bi, 0, c_v + h)) if first
                else pl.BlockSpec((1, l, HY_CH), lambda h, bi: (bi, 0, h)))
    return pl.pallas_call(
        functools.partial(_hy_order_kernel, l=l, first=first, last=last),
        grid=(nh, b),
        in_specs=[yin_spec,
                  pl.BlockSpec((1, l, HY_CH), lambda h, bi: (bi, 0, c_g + h)),
                  pl.BlockSpec((1, l, HY_CH), lambda h, bi: (bi, 0, c_z + h)),
                  pl.BlockSpec((3, HY_CH), lambda h, bi: (0, h)),
                  pl.BlockSpec((3, HY_CH), lambda h, bi: (0, (order + 1) * nh + h)),
                  pl.BlockSpec((1, 1, HY_CH), lambda h, bi: (order, 0, h)),
                  pl.BlockSpec((n, HY_CH), lambda h, bi: (0, order * nh + h), pipeline_mode=once),
                  pl.BlockSpec((n, HY_CH), lambda h, bi: (0, order * nh + h), pipeline_mode=once),
                  pl.BlockSpec(f1.shape, lambda h, bi: (0, 0, 0), pipeline_mode=once),
                  pl.BlockSpec(fblk.shape, lambda h, bi: (0, 0), pipeline_mode=once),
                  pl.BlockSpec(fiblk.shape, lambda h, bi: (0, 0), pipeline_mode=once),
                  pl.BlockSpec(hinv.shape, lambda h, bi: (0, 0, 0), pipeline_mode=once)],
        out_specs=pl.BlockSpec((1, l, HY_CH), lambda h, bi: (bi, 0, h)),
        out_shape=jax.ShapeDtypeStruct((b, l, HY_WIDTH), bf16 if last else f32),
        scratch_shapes=[pltpu.VMEM((l + 16, HY_CH), f32), pltpu.VMEM((l, HY_CH), f32), pltpu.VMEM((l, HY_CH), f32),
                        pltpu.VMEM((l, HY_CH), f32),
                        pltpu.VMEM((FFT_N2 * sa, HY_CH), f32), pltpu.VMEM((FFT_N2 * sa, HY_CH), f32),
                        pltpu.VMEM((n1 * sb, HY_CH), f32), pltpu.VMEM((n1 * sb, HY_CH), f32)],
        compiler_params=pltpu.CompilerParams(dimension_semantics=("arbitrary", "arbitrary"),
                                             vmem_limit_bytes=56 * 1024 * 1024),
        name=f"hyena_order{order}",
    )(yin, px, px, conv_w, conv_w, skip.reshape(HY_ORDER, 1, HY_WIDTH), hr, hi, f1, fblk, fiblk, hinv)


def hyena_latent(px, conv_w, skip, w1, b1, w2, b2, w3, b3, freq, wout):
    l = px.shape[1]
    kern = hyena_filter_circular(l, w1, b1, w2, b2, w3, b3, freq, wout)
    hr, hi = filter_spectrum(kern)
    y1 = hyena_order(px, px, hr, hi, conv_w, skip, 0, True, False)
    return hyena_order(y1, px, hr, hi, conv_w, skip, 1, False, True)


@functools.lru_cache(maxsize=None)
def _dense_dft_consts(l):
    n = 2 * l
    k = np.arange(n)
    ang = -2.0 * np.pi * np.outer(k, np.arange(n)) / n
    fwd = np.concatenate([np.cos(ang), np.sin(ang)], axis=0)
    angi = 2.0 * np.pi * np.outer(np.arange(l), k) / n
    inv = np.concatenate([np.cos(angi), -np.sin(angi)], axis=1) / n
    return fwd, inv


def _hy_ctx_kernel(p_ref, kern_ref, cw_ref, skip_ref, fwd_ref, inv_ref, o_ref, pad_ref, t_ref, *, l):
    n = 2 * l
    dot = functools.partial(jnp.dot, preferred_element_type=f32, precision=HIGHEST)
    w = HY_WIDTH

    def conv3(col):
        pad_ref[0:8, :] = jnp.zeros((8, w), f32)
        pad_ref[l + 8:l + 16, :] = jnp.zeros((8, w), f32)
        pad_ref[8:8 + l, :] = p_ref[0, :, col:col + w].astype(f32)
        cw = cw_ref[:, col - C_HY_VX:col - C_HY_VX + w].astype(f32)
        return cw[0:1] * pad_ref[7:7 + l, :] + cw[1:2] * pad_ref[8:8 + l, :] + cw[2:3] * pad_ref[9:9 + l, :]

    y = conv3(C_HY_VX)
    for o in range(HY_ORDER):
        gate = conv3(C_HY_VX + (o + 1) * w)
        hh = dot(fwd_ref[...], kern_ref[:, o * w:(o + 1) * w])
        xx = dot(fwd_ref[:, :l], y)
        xr, xi, hr, hi = xx[:n], xx[n:], hh[:n], hh[n:]
        t_ref[0:n, :] = xr * hr - xi * hi
        t_ref[n:2 * n, :] = xr * hi + xi * hr
        conv = dot(inv_ref[...], t_ref[...])
        y = gate * (conv + y * skip_ref[o:o + 1, :].astype(f32))
    z = p_ref[0, :, C_HY_Z:C_HY_Z + w].astype(f32)
    o_ref[0] = (y * (z * jax.nn.sigmoid(z))).astype(o_ref.dtype)


def hyena_ctx(pc, conv_w, skip, w1, b1, w2, b2, w3, b3, freq, wout):
    b, lc, _ = pc.shape
    n = 2 * lc
    kern = hyena_filter_circular(lc, w1, b1, w2, b2, w3, b3, freq, wout)
    fwd, inv = [jnp.asarray(a, f32) for a in _dense_dft_consts(lc)]
    full = lambda a: pl.BlockSpec(a.shape, lambda bi: (0,) * a.ndim)
    return pl.pallas_call(
        functools.partial(_hy_ctx_kernel, l=lc),
        grid=(b,),
        in_specs=[pl.BlockSpec((1, lc, MAIN_W), lambda bi: (bi, 0, 0)), full(kern), full(conv_w), full(skip),
                  full(fwd), full(inv)],
        out_specs=pl.BlockSpec((1, lc, HY_WIDTH), lambda bi: (bi, 0, 0)),
        out_shape=jax.ShapeDtypeStruct((b, lc, HY_WIDTH), bf16),
        scratch_shapes=[pltpu.VMEM((lc + 16, HY_WIDTH), f32), pltpu.VMEM((2 * n, HY_WIDTH), f32)],
        compiler_params=pltpu.CompilerParams(dimension_semantics=("arbitrary",), vmem_limit_bytes=VMEM_LIMIT),
        name="hyena_ctx",
    )(pc, kern, conv_w, skip, fwd, inv)


DN_PREP_ROWS = 256
DN_SUB_ROWS = 64
DN_HALO = 8


@functools.lru_cache(maxsize=None)
def _rope_tables(l):
    half = DN_HEAD_DIM // 2
    nf = half // 2
    t = np.arange(l)
    inv = (np.float32(ROPE_BASE) ** (-np.arange(nf, dtype=np.float32) / nf)).astype(np.float32)
    ang_r = ((t // GRID_W).astype(np.float32)[:, None] * inv[None, :]).astype(np.float32)
    ang_c = ((t % GRID_W).astype(np.float32)[:, None] * inv[None, :]).astype(np.float32)
    cos = np.concatenate([np.cos(ang_r), np.cos(ang_r), np.cos(ang_c), np.cos(ang_c)], axis=1)
    sin = np.concatenate([-np.sin(ang_r), np.sin(ang_r), -np.sin(ang_c), np.sin(ang_c)], axis=1)
    return cos.astype(np.float32), sin.astype(np.float32)


def _dn_prep_kernel(*refs, rope, n_tiles):
    (q_ref, qp_ref, qn_ref, k_ref, kp_ref, kn_ref, v_ref, vp_ref, vn_ref, g_ref, cw_ref, cos_ref, sin_ref, al_ref,
     dt_ref, qo_ref, ko_ref, vo_ref, go_ref, pad_ref) = refs
    i = pl.program_id(1)
    t = DN_PREP_ROWS
    w = DN_WIDTH
    lane = lax.broadcasted_iota(jnp.int32, (DN_SUB_ROWS, w), 1)
    first_half = (lane % (DN_HEAD_DIM // 2)) < (DN_HEAD_DIM // 4)
    for idx, (m_ref, p_ref, n_ref, o_ref) in enumerate(((q_ref, qp_ref, qn_ref, qo_ref), (k_ref, kp_ref, kn_ref, ko_ref),
                                                        (v_ref, vp_ref, vn_ref, vo_ref))):
        pad_ref[0:DN_HALO, :] = jnp.where(i == 0, 0.0, p_ref[0].astype(f32))
        pad_ref[DN_HALO:DN_HALO + t, :] = m_ref[0].astype(f32)
        pad_ref[DN_HALO + t:2 * DN_HALO + t, :] = jnp.where(i == n_tiles - 1, 0.0, n_ref[0].astype(f32))
        cw = cw_ref[:, idx * w:(idx + 1) * w].astype(f32)

        def sub(s, c):
            r = pl.multiple_of(s * DN_SUB_ROWS, DN_SUB_ROWS)
            blk = pad_ref[pl.ds(r, DN_SUB_ROWS + 2 * DN_HALO), :]
            base = DN_HALO - DN_CONV_W // 2
            y = cw[0:1] * blk[base:base + DN_SUB_ROWS]
            for j in range(1, DN_CONV_W):
                y = y + cw[j:j + 1] * blk[base + j:base + j + DN_SUB_ROWS]
            y = y * jax.nn.sigmoid(y)
            if idx < 2:
                if rope:
                    cos = jnp.concatenate([cos_ref[pl.ds(r, DN_SUB_ROWS), :]] * DN_HEADS, axis=1)
                    sin = jnp.concatenate([sin_ref[pl.ds(r, DN_SUB_ROWS), :]] * DN_HEADS, axis=1)
                    q4 = DN_HEAD_DIM // 4
                    swapped = jnp.where(first_half, pltpu.roll(y, w - q4, 1), pltpu.roll(y, q4, 1))
                    y = y * cos + swapped * sin
                outs = []
                for h in range(DN_HEADS):
                    yh = y[:, h * DN_HEAD_DIM:(h + 1) * DN_HEAD_DIM]
                    nrm = lax.rsqrt(jnp.sum(yh * yh, axis=-1, keepdims=True) + NORM_EPS)
                    if idx == 0:
                        nrm = nrm * (DN_HEAD_DIM ** -0.5)
                    outs.append(yh * nrm)
                y = jnp.concatenate(outs, axis=1)
            o_ref[0, pl.ds(r, DN_SUB_ROWS), :] = y.astype(o_ref.dtype)
            return c
        lax.fori_loop(0, t // DN_SUB_ROWS, sub, 0)

    g = g_ref[0]
    glane = lax.broadcasted_iota(jnp.int32, g.shape, 1)
    xa = g + dt_ref[...]
    softplus = jnp.maximum(xa, 0.0) + jnp.log1p(jnp.exp(-jnp.abs(xa)))
    go_ref[0] = jnp.where(glane < 2 * DN_HEADS, jax.nn.sigmoid(g), -jnp.exp(al_ref[...]) * softplus)


def dn_prep(p, gates_raw, conv_w, a_log, dt_bias, rope):
    b, l, _ = p.shape
    t = DN_PREP_ROWS
    w = DN_WIDTH
    n_tiles = l // t
    hb = t // DN_HALO
    cblk = C_DN_QKV // w
    cos, sin = [jnp.asarray(a) for a in _rope_tables(l)]
    pad = jnp.zeros((2 * DN_HEADS,), f32)
    rest = jnp.zeros((GATE_W - 4 * DN_HEADS,), f32)
    al = jnp.concatenate([pad, a_log.reshape(-1), rest])[None]
    dt = jnp.concatenate([pad, dt_bias.reshape(-1), rest])[None]

    def slab(j):
        return [pl.BlockSpec((1, t, w), lambda bi, i: (bi, i, cblk + j)),
                pl.BlockSpec((1, DN_HALO, w), lambda bi, i: (bi, jnp.maximum(i * hb - 1, 0), cblk + j)),
                pl.BlockSpec((1, DN_HALO, w), lambda bi, i: (bi, jnp.minimum((i + 1) * hb, l // DN_HALO - 1), cblk + j))]

    row = lambda a: pl.BlockSpec(a.shape, lambda bi, i: (0, 0))
    tab = pl.BlockSpec((t, DN_HEAD_DIM), lambda bi, i: (i, 0))
    out_blk = pl.BlockSpec((1, t, w), lambda bi, i: (bi, i, 0))
    return pl.pallas_call(
        functools.partial(_dn_prep_kernel, rope=rope, n_tiles=n_tiles),
        grid=(b, n_tiles),
        in_specs=slab(0) + slab(1) + slab(2) + [
            pl.BlockSpec((1, t, GATE_W), lambda bi, i: (bi, i, 0)), row(conv_w), tab, tab, row(al), row(dt)],
        out_specs=[out_blk, out_blk, out_blk, pl.BlockSpec((1, t, GATE_W), lambda bi, i: (bi, i, 0))],
        out_shape=[jax.ShapeDtypeStruct((b, l, w), bf16)] * 3 + [jax.ShapeDtypeStruct((b, l, GATE_W), f32)],
        scratch_shapes=[pltpu.VMEM((t + 2 * DN_HALO, w), f32)],
        compiler_params=pltpu.CompilerParams(dimension_semantics=("arbitrary", "arbitrary"),
                                             vmem_limit_bytes=VMEM_LIMIT),
        name="dn_prep",
    )(p, p, p, p, p, p, p, p, p, gates_raw, conv_w, cos, sin, al, dt)


DN_GROUP = 4
DN_GROUP_ROWS = DN_GROUP * DN_CHUNK
DN_UNROLL = 2


def _bdot(a, b):
    return jnp.dot(a.astype(bf16), b.astype(bf16), preferred_element_type=f32)


def _dn_chunk_kernel(qc_ref, kc_ref, vc_ref, gc_ref, qx_ref, kx_ref, vx_ref, gx_ref, u_ref, wq_ref, ak_ref, eg_ref):
    g = pl.program_id(1)
    c = DN_CHUNK
    dh = DN_HEAD_DIM
    is_ctx = g == 0
    ri = lax.broadcasted_iota(jnp.int32, (c, c), 0)
    ci = lax.broadcasted_iota(jnp.int32, (c, c), 1)
    eye = (ri == ci).astype(f32)
    er = lax.broadcasted_iota(jnp.int32, (GATE_W, GATE_W), 0)
    ec = lax.broadcasted_iota(jnp.int32, (GATE_W, GATE_W), 1)
    eye_b = (er == ec).astype(bf16)
    tri_b = (ri >= ci).astype(bf16)
    masks = [((ri <= ci) if d else (ri >= ci), (ri < ci) if d else (ri > ci)) for d in range(2)]

    def sel_dot(sel, x, dims):
        hi = x.astype(bf16)
        mid = (x - hi.astype(f32)).astype(bf16)
        lo = (x - hi.astype(f32) - mid.astype(f32)).astype(bf16)
        return sum(lax.dot_general(sel, t, dims, preferred_element_type=f32) for t in (hi, mid, lo))

    def body(it, carry):
        units = []
        for jj in range(DN_UNROLL):
            j = it * DN_UNROLL + jj
            r = pl.multiple_of(j * c, c)
            pick = lambda a_ref, b_ref: jnp.where(is_ctx, a_ref[0, pl.ds(r, c), :], b_ref[0, pl.ds(r, c), :])
            q, k, v, gates = pick(qc_ref, qx_ref), pick(kc_ref, kx_ref), pick(vc_ref, vx_ref), pick(gc_ref, gx_ref)
            gtot = jnp.sum(gates, axis=0, keepdims=True)
            gfwd = sel_dot(tri_b, gates, (((1,), (0,)), ((), ())))
            gcs = (gfwd, gtot - gfwd + gates)
            g_t = sel_dot(eye_b, jnp.concatenate([gfwd, gates], axis=0), NT_DIMS)
            gcs_t = (g_t[:, :c], jnp.sum(g_t[:, c:], axis=1, keepdims=True) - g_t[:, :c] + g_t[:, c:])
            qks = [lax.dot_general(q[:, h * dh:(h + 1) * dh], k[:, h * dh:(h + 1) * dh], NT_DIMS,
                                   preferred_element_type=f32) for h in range(DN_HEADS)]
            kks = [lax.dot_general(k[:, h * dh:(h + 1) * dh], k[:, h * dh:(h + 1) * dh], NT_DIMS,
                                   preferred_element_type=f32) for h in range(DN_HEADS)]
            for d in range(2):
                incl, strict = masks[d]
                egs = []
                for h in range(DN_HEADS):
                    hs = slice(h * dh, (h + 1) * dh)
                    col = d * DN_HEADS + h
                    gl = 2 * DN_HEADS + col
                    kh, vh, qh = k[:, hs].astype(f32), v[:, hs].astype(f32), q[:, hs].astype(f32)
                    beta = gates[:, col:col + 1]
                    gcol = gcs[d][:, gl:gl + 1]
                    glast = gtot[:, gl:gl + 1]
                    decay = jnp.where(incl, jnp.exp(jnp.minimum(gcol - gcs_t[d][gl:gl + 1, :], 0.0)), 0.0)
                    eg = jnp.exp(gcol)
                    wq_ref[d, 0, j, h, c:2 * c, :] = (qh * eg).astype(bf16)
                    ak_ref[d, 0, j, h, 0:c, :] = jnp.where(incl, qks[h] * decay, 0.0).astype(bf16)
                    ak_ref[d, 0, j, h, c:c + dh, :] = jnp.transpose(kh * jnp.exp(glast - gcol)).astype(bf16)
                    egs.append(jnp.broadcast_to(jnp.exp(glast), (1, GATE_W)))
                    kb = kh * beta
                    units.append(dict(j=j, d=d, h=h, n=-jnp.where(strict, beta * kks[h] * decay, 0.0),
                                      rhs=jnp.concatenate([vh * beta, kb * eg], axis=1).astype(bf16)))
                eg_ref[d, 0, j] = jnp.concatenate(egs + [jnp.zeros((8 - DN_HEADS, GATE_W), f32)], axis=0)
        for un in units:
            un["t"] = eye + un["n"]
            un["p"] = _bdot(un["n"], un["n"])
        for _ in range(int(math.log2(c)) - 2):
            for un in units:
                both = _bdot(jnp.concatenate([un["p"], un["t"]], axis=0), un["p"])
                un["p"], un["t"] = both[:c], un["t"] + both[c:]
        for un in units:
            un["t"] = un["t"] + _bdot(un["t"], un["p"])
        for un in units:
            sol = _bdot(un["t"], un["rhs"])
            u_ref[un["d"], 0, un["j"], un["h"]] = sol[:, :dh].astype(bf16)
            wq_ref[un["d"], 0, un["j"], un["h"], 0:c, :] = sol[:, dh:].astype(bf16)
        return carry
    lax.fori_loop(0, DN_GROUP // DN_UNROLL, body, 0)


def dn_chunks(qc, kc, vc, gc, qx, kx, vx, gx):
    b, lc, w = qc.shape
    l = qx.shape[1]
    assert lc == DN_GROUP_ROWS and l % DN_GROUP_ROWS == 0
    ng = 1 + l // DN_GROUP_ROWS
    nch = ng * DN_GROUP
    c, dh, nhd = DN_CHUNK, DN_HEAD_DIM, DN_HEADS
    cmap = lambda bi, g: (bi, 0, 0)
    xmap = lambda bi, g: (bi, jnp.maximum(g - 1, 0), 0)
    blk = lambda width, m: pl.BlockSpec((1, DN_GROUP_ROWS, width), m)
    omap = lambda bi, g: (0, bi, g, 0, 0, 0)
    return pl.pallas_call(
        _dn_chunk_kernel,
        grid=(b, ng),
        in_specs=[blk(w, cmap), blk(w, cmap), blk(w, cmap), blk(GATE_W, cmap),
                  blk(w, xmap), blk(w, xmap), blk(w, xmap), blk(GATE_W, xmap)],
        out_specs=[pl.BlockSpec((2, 1, DN_GROUP, nhd, c, dh), omap),
                   pl.BlockSpec((2, 1, DN_GROUP, nhd, 2 * c, dh), omap),
                   pl.BlockSpec((2, 1, DN_GROUP, nhd, c + dh, c), omap),
                   pl.BlockSpec((2, 1, DN_GROUP, 8, GATE_W), lambda bi, g: (0, bi, g, 0, 0))],
        out_shape=[jax.ShapeDtypeStruct((2, b, nch, nhd, c, dh), bf16),
                   jax.ShapeDtypeStruct((2, b, nch, nhd, 2 * c, dh), bf16),
                   jax.ShapeDtypeStruct((2, b, nch, nhd, c + dh, c), bf16),
                   jax.ShapeDtypeStruct((2, b, nch, 8, GATE_W), f32)],
        compiler_params=pltpu.CompilerParams(dimension_semantics=("arbitrary", "arbitrary"),
                                             vmem_limit_bytes=VMEM_LIMIT),
        name="dn_chunks",
    )(qc, kc, vc, gc, qx, kx, vx, gx)


def _bwd_group(s, ng):
    return jnp.where(s == 0, 0, ng - s)


def _dn_serial_kernel(uf_ref, wqf_ref, akf_ref, egf_ref, ub_ref, wqb_ref, akb_ref, egb_ref, of_ref, ob_ref, s_ref):
    c = DN_CHUNK
    dh = DN_HEAD_DIM

    @pl.when(pl.program_id(1) == 0)
    def _():
        s_ref[...] = jnp.zeros(s_ref.shape, f32)

    dirs = ((uf_ref, wqf_ref, akf_ref, egf_ref, of_ref), (ub_ref, wqb_ref, akb_ref, egb_ref, ob_ref))
    states = [[s_ref[d, h] for h in range(DN_HEADS)] for d in range(2)]
    units = [(d, h) for d in range(2) for h in range(DN_HEADS)]
    for jj in range(DN_GROUP):
        js = (jj, DN_GROUP - 1 - jj)
        r1 = {(d, h): jnp.dot(dirs[d][1][0, 0, js[d], h], states[d][h].astype(bf16), preferred_element_type=f32)
              for d, h in units}
        r2 = {}
        for d, h in units:
            v_new = dirs[d][0][0, 0, js[d], h].astype(f32) - r1[d, h][:c]
            r2[d, h] = jnp.dot(dirs[d][2][0, 0, js[d], h], v_new.astype(bf16), preferred_element_type=f32)
        for d, h in units:
            j = js[d]
            dirs[d][4][0, j * c:(j + 1) * c, h * dh:(h + 1) * dh] = r1[d, h][c:] + r2[d, h][:c]
            states[d][h] = states[d][h] * dirs[d][3][0, 0, j, h:h + 1, :] + r2[d, h][c:]
    for d in range(2):
        for h in range(DN_HEADS):
            s_ref[d, h] = states[d][h]


def dn_serial(u, wq, ak, eg):
    _, b, nch, nhd, c, dh = u.shape
    ng = nch // DN_GROUP
    w = nhd * dh
    fmap6 = lambda bi, s: (0, bi, s, 0, 0, 0)
    bmap6 = lambda bi, s: (1, bi, _bwd_group(s, ng), 0, 0, 0)
    fmap5 = lambda bi, s: (0, bi, s, 0, 0)
    bmap5 = lambda bi, s: (1, bi, _bwd_group(s, ng), 0, 0)
    blk6 = lambda a, m: pl.BlockSpec((1, 1, DN_GROUP) + a.shape[3:], m)
    egblk = lambda m: pl.BlockSpec((1, 1, DN_GROUP, 8, GATE_W), m)
    return pl.pallas_call(
        _dn_serial_kernel,
        grid=(b, ng),
        in_specs=[blk6(u, fmap6), blk6(wq, fmap6), blk6(ak, fmap6), egblk(fmap5),
                  blk6(u, bmap6), blk6(wq, bmap6), blk6(ak, bmap6), egblk(bmap5)],
        out_specs=[pl.BlockSpec((1, DN_GROUP_ROWS, w), lambda bi, s: (bi, s, 0)),
                   pl.BlockSpec((1, DN_GROUP_ROWS, w), lambda bi, s: (bi, _bwd_group(s, ng), 0))],
        out_shape=[jax.ShapeDtypeStruct((b, nch * c, w), f32)] * 2,
        scratch_shapes=[pltpu.VMEM((2, nhd, dh, dh), f32)],
        compiler_params=pltpu.CompilerParams(dimension_semantics=("arbitrary", "arbitrary"),
                                             vmem_limit_bytes=VMEM_LIMIT),
        name="dn_serial",
    )(u, wq, ak, eg, u, wq, ak, eg)


def _dn_out_kernel(of_ref, ob_ref, z_ref, nw_ref, y_ref):
    o = of_ref[0] + ob_ref[0]
    z = z_ref[0].astype(f32)
    nw = nw_ref[...].astype(f32)
    outs = []
    for h in range(DN_HEADS):
        oh = o[:, h * DN_HEAD_DIM:(h + 1) * DN_HEAD_DIM]
        outs.append(oh * lax.rsqrt(jnp.mean(oh * oh, axis=-1, keepdims=True) + NORM_EPS) * nw)
    y_ref[0] = (jnp.concatenate(outs, axis=1) * (z * jax.nn.sigmoid(z))).astype(y_ref.dtype)


def dn_output(o_f, o_b, p, norm_w, first_block):
    b, l, _ = p.shape
    w = DN_WIDTH
    tm = DN_GROUP_ROWS
    omap = lambda bi, i: (bi, first_block + i, 0)
    return pl.pallas_call(
        _dn_out_kernel,
        grid=(b, l // tm),
        in_specs=[pl.BlockSpec((1, tm, w), omap), pl.BlockSpec((1, tm, w), omap),
                  pl.BlockSpec((1, tm, w), lambda bi, i: (bi, i, C_DN_Z // w)),
                  pl.BlockSpec((1, DN_HEAD_DIM), lambda bi, i: (0, 0))],
        out_specs=pl.BlockSpec((1, tm, w), lambda bi, i: (bi, i, 0)),
        out_shape=jax.ShapeDtypeStruct((b, l, w), bf16),
        compiler_params=pltpu.CompilerParams(dimension_semantics=("arbitrary", "arbitrary"),
                                             vmem_limit_bytes=VMEM_LIMIT),
        name="dn_output",
    )(o_f, o_b, p, norm_w.reshape(1, DN_HEAD_DIM))


def _pack_w_in(w_in):
    depth, d, _ = w_in.shape
    main = jnp.concatenate([w_in[:, :, :3072], w_in[:, :, 3088:4112]], axis=-1).astype(bf16)
    gate = jnp.concatenate([w_in[:, :, 3072:3088], jnp.zeros((depth, d, GATE_W - 16), w_in.dtype)],
                           axis=-1).astype(bf16)
    return main, gate


def kernel(x, c, ctx, c_ctx, w_ada, b_ada, g_pre, g_post, w_in, w_out, na_rpb, dn_conv, dn_a_log, dn_dt_bias,
           dn_norm, hy_conv, hy_w1, hy_b1, hy_w2, hy_b2, hy_w3, hy_b3, hy_freq, hy_wout, hy_skip):
    bsz, l, d = x.shape
    lc = ctx.shape[1]
    depth = w_in.shape[0]
    cond = jnp.concatenate([c, c_ctx[None], jnp.zeros((7, d), f32)], axis=0)
    mod = modulation_all(cond, w_ada, b_ada)
    w_main, w_gate = _pack_w_in(w_in)
    w_out_b = w_out.astype(bf16)

    for i in range(depth):
        last = i == depth - 1
        shift_x, scale_x, gate_x = [mod[i, :bsz, j * d:(j + 1) * d][:, None] for j in range(3)]
        shift_c, scale_c, gate_c = [mod[i, bsz:bsz + 1, j * d:(j + 1) * d][:, None] for j in range(3)]
        px, gx = in_projection(x, g_pre[i], scale_x, shift_x, w_main[i], w_gate[i], 512)
        pc, gc = in_projection(ctx, g_pre[i], scale_c, shift_c, w_main[i], w_gate[i], 256)

        hy_args = (hy_conv[i], hy_skip[i], hy_w1[i], hy_b1[i], hy_w2[i], hy_b2[i], hy_w3[i], hy_b3[i], hy_freq[i],
                   hy_wout[i])
        dn_args = (dn_conv[i], dn_a_log[i], dn_dt_bias[i])

        out_a_x = na_attention(px, pc, na_rpb[i])

        dn_c = dn_prep(pc, gc, *dn_args, False)
        dn_x = dn_prep(px, gx, *dn_args, True)
        do_f, do_b = dn_serial(*dn_chunks(*dn_c, *dn_x))
        out_b_x = dn_output(do_f, do_b, px, dn_norm[i], lc // DN_GROUP_ROWS)

        out_c_x = hyena_latent(px, *hy_args)

        new_x = out_projection(out_a_x, out_b_x, out_c_x, w_out_b[i], g_post[i], gate_x, x, 512)

        if not last:
            out_a_c = ctx_attention(pc)
            out_b_c = dn_output(do_f, do_b, pc, dn_norm[i], 0)
            out_c_c = hyena_ctx(pc, *hy_args)
            ctx = out_projection(out_a_c, out_b_c, out_c_c, w_out_b[i], g_post[i], gate_c, ctx, lc)
        x = new_x
    return x
```

```python
import functools
import math

import numpy as np
import jax
import jax.numpy as jnp
from jax import lax
from jax.experimental import pallas as pl
from jax.experimental.pallas import tpu as pltpu

D_MODEL = 1024
GRID_W = 64
NA_HEAD_DIM = 64
NA_WIDTH = 256
NA_HEADS = 4
NA_WIN_ROWS = 8
NA_WIN_COLS = 16
DN_HEAD_DIM = 128
DN_WIDTH = 512
DN_HEADS = 4
DN_CONV_W = 5
DN_CHUNK = 64
HY_WIDTH = 256
HY_ORDER = 2
HY_EMB = 33
HY_DECAY_TARGET = 1e-2
HY_FAST_DECAY = 0.3
HY_SLOW_DECAY = 1.5
ROPE_BASE = 10000.0
NORM_EPS = 1e-6

C_NA_QKV, C_NA_Z, C_DN_QKV, C_DN_Z, C_HY_VX, C_HY_Z = 0, 768, 1024, 2560, 3072, 3840
MAIN_W = 4096
GATE_W = 128
LANE = 128
VMEM_LIMIT = 52 * 1024 * 1024

f32 = jnp.float32
bf16 = jnp.bfloat16
HIGHEST = lax.Precision.HIGHEST
NT_DIMS = (((1,), (1,)), ((), ()))
TN_DIMS = (((0,), (0,)), ((), ()))


def _mod_kernel(c_ref, w_ref, b_ref, o_ref):
    c = c_ref[...]
    a = c * jax.nn.sigmoid(c)
    o_ref[0] = jnp.dot(a, w_ref[0], preferred_element_type=f32, precision=lax.Precision.HIGHEST) + b_ref[0]


def modulation_all(cond, w_ada, b_ada):
    depth, d, d3 = w_ada.shape
    r = cond.shape[0]
    tn = 512
    return pl.pallas_call(
        _mod_kernel,
        grid=(depth, d3 // tn),
        in_specs=[pl.BlockSpec((r, d), lambda i, j: (0, 0)),
                  pl.BlockSpec((1, d, tn), lambda i, j: (i, 0, j)),
                  pl.BlockSpec((1, 1, tn), lambda i, j: (i, 0, j))],
        out_specs=pl.BlockSpec((1, r, tn), lambda i, j: (i, 0, j)),
        out_shape=jax.ShapeDtypeStruct((depth, r, d3), f32),
        compiler_params=pltpu.CompilerParams(dimension_semantics=("arbitrary", "arbitrary"),
                                             vmem_limit_bytes=VMEM_LIMIT),
        name="modulation",
    )(cond, w_ada, b_ada.reshape(depth, 1, d3))


def _inproj_kernel(x_ref, g_ref, sc_ref, sh_ref, w_ref, wg_ref, o_ref, og_ref):
    x = x_ref[0]
    ms = jnp.mean(x * x, axis=-1, keepdims=True)
    h = x * lax.rsqrt(ms + NORM_EPS) * g_ref[...]
    h = (h * (1.0 + sc_ref[0]) + sh_ref[0]).astype(bf16)
    for j in range(MAIN_W // 1024):
        o_ref[0, :, j * 1024:(j + 1) * 1024] = jnp.dot(
            h, w_ref[:, j * 1024:(j + 1) * 1024], preferred_element_type=f32).astype(bf16)
    og_ref[0] = jnp.dot(h, wg_ref[...], preferred_element_type=f32)


def in_projection(x, g, scale, shift, w_main, w_gate, tm):
    b, l, d = x.shape
    per_b = scale.shape[0] == b
    mod_map = (lambda bi, i: (bi, 0, 0)) if per_b else (lambda bi, i: (0, 0, 0))
    return pl.pallas_call(
        _inproj_kernel,
        grid=(b, l // tm),
        in_specs=[pl.BlockSpec((1, tm, d), lambda bi, i: (bi, i, 0)),
                  pl.BlockSpec((1, d), lambda bi, i: (0, 0)),
                  pl.BlockSpec((1, 1, d), mod_map),
                  pl.BlockSpec((1, 1, d), mod_map),
                  pl.BlockSpec((d, MAIN_W), lambda bi, i: (0, 0)),
                  pl.BlockSpec((d, GATE_W), lambda bi, i: (0, 0))],
        out_specs=[pl.BlockSpec((1, tm, MAIN_W), lambda bi, i: (bi, i, 0)),
                   pl.BlockSpec((1, tm, GATE_W), lambda bi, i: (bi, i, 0))],
        out_shape=[jax.ShapeDtypeStruct((b, l, MAIN_W), bf16),
                   jax.ShapeDtypeStruct((b, l, GATE_W), f32)],
        compiler_params=pltpu.CompilerParams(dimension_semantics=("arbitrary", "arbitrary"),
                                             vmem_limit_bytes=VMEM_LIMIT),
        name="in_projection",
    )(x, g.reshape(1, d), scale, shift, w_main, w_gate)


def _outproj_kernel(a_ref, b_ref, c_ref, w_ref, g_ref, gate_ref, x_ref, o_ref):
    y = (jnp.dot(a_ref[0], w_ref[0:NA_WIDTH], preferred_element_type=f32)
         + jnp.dot(b_ref[0], w_ref[NA_WIDTH:NA_WIDTH + DN_WIDTH], preferred_element_type=f32)
         + jnp.dot(c_ref[0], w_ref[NA_WIDTH + DN_WIDTH:], preferred_element_type=f32))
    ms = jnp.mean(y * y, axis=-1, keepdims=True)
    yn = y * lax.rsqrt(ms + NORM_EPS) * g_ref[...]
    o_ref[0] = x_ref[0] + gate_ref[0] * yn


def out_projection(out_a, out_b, out_c, w_out, g_post, gate, x, tm):
    b, l, d = x.shape
    per_b = gate.shape[0] == b
    mod_map = (lambda bi, i: (bi, 0, 0)) if per_b else (lambda bi, i: (0, 0, 0))
    part = lambda a: pl.BlockSpec((1, tm, a.shape[-1]), lambda bi, i: (bi, i, 0))
    return pl.pallas_call(
        _outproj_kernel,
        grid=(b, l // tm),
        in_specs=[part(out_a), part(out_b), part(out_c),
                  pl.BlockSpec((d, d), lambda bi, i: (0, 0)),
                  pl.BlockSpec((1, d), lambda bi, i: (0, 0)),
                  pl.BlockSpec((1, 1, d), mod_map),
                  pl.BlockSpec((1, tm, d), lambda bi, i: (bi, i, 0))],
        out_specs=pl.BlockSpec((1, tm, d), lambda bi, i: (bi, i, 0)),
        out_shape=jax.ShapeDtypeStruct((b, l, d), f32),
        compiler_params=pltpu.CompilerParams(dimension_semantics=("arbitrary", "arbitrary"),
                                             vmem_limit_bytes=VMEM_LIMIT),
        name="out_projection",
    )(out_a, out_b, out_c, w_out, g_post.reshape(1, d), gate, x)


NA_TQ_ROWS = 4
NA_TQ = NA_TQ_ROWS * GRID_W
NA_KV_ROWS = NA_TQ_ROWS + NA_WIN_ROWS
NA_KV = NA_KV_ROWS * GRID_W
MASK_VALUE = -1e30


def _na_window_start(t, rows):
    return jnp.clip(t * NA_TQ_ROWS - NA_WIN_ROWS // 2, 0, rows - NA_KV_ROWS)


def _na_bias_kernel(rpb_ref, sel_ref, mask_ref, o_ref):
    o_ref[...] = jnp.dot(rpb_ref[...], sel_ref[...], preferred_element_type=f32, precision=HIGHEST) + mask_ref[...]


def na_bias_tables(rpb, rows):
    nh, ndr, ndc = rpb.shape
    col = np.arange(GRID_W)
    dc = np.clip(col[None, :] - col[:, None], -(NA_WIN_COLS - 1), NA_WIN_COLS - 1) + (NA_WIN_COLS - 1)
    c0 = np.clip(col - NA_WIN_COLS // 2, 0, GRID_W - NA_WIN_COLS)
    in_win = (col[None, :] >= c0[:, None]) & (col[None, :] < c0[:, None] + NA_WIN_COLS)
    sel = np.zeros((LANE, GRID_W * GRID_W), np.float32)
    sel[dc.reshape(-1), np.arange(GRID_W * GRID_W)] = 1.0
    cmask = np.where(in_win, 0.0, MASK_VALUE).astype(np.float32).reshape(1, -1)
    rpb2 = jnp.pad(rpb.astype(f32).reshape(nh * ndr, ndc), ((0, 0), (0, LANE - ndc)))
    tab = pl.pallas_call(
        _na_bias_kernel,
        out_shape=jax.ShapeDtypeStruct((nh * ndr, GRID_W * GRID_W), f32),
        name="na_bias",
    )(rpb2, jnp.asarray(sel), jnp.asarray(cmask)).reshape(nh, ndr, GRID_W, GRID_W)
    masked = jnp.full((nh, GRID_W, GRID_W), MASK_VALUE, f32)
    n_tiles = rows // NA_TQ_ROWS
    tabs = []
    for t in (0, 1, n_tiles - 1):
        r0 = t * NA_TQ_ROWS
        ws = int(np.clip(r0 - NA_WIN_ROWS // 2, 0, rows - NA_KV_ROWS))
        blocks = []
        for qr in range(r0, r0 + NA_TQ_ROWS):
            band0 = int(np.clip(qr - NA_WIN_ROWS // 2, 0, rows - NA_WIN_ROWS))
            blocks.append(jnp.concatenate(
                [tab[:, kr - qr + NA_WIN_ROWS - 1] if band0 <= kr < band0 + NA_WIN_ROWS else masked
                 for kr in range(ws, ws + NA_KV_ROWS)], axis=-1))
        tabs.append(jnp.concatenate(blocks, axis=-2))
    return jnp.stack(tabs)


def _na_kernel(q_ref, k_ref, v_ref, z_ref, kc_ref, vc_ref, bias_ref, o_ref, *, rows):
    t = pl.program_id(1)
    ws = pl.multiple_of(_na_window_start(t, rows) * GRID_W, GRID_W)
    q = q_ref[0]
    kw = k_ref[0, pl.ds(ws, NA_KV), :]
    vw = v_ref[0, pl.ds(ws, NA_KV), :]
    kc = kc_ref[0]
    vc = vc_ref[0]
    outs = []
    for h in range(NA_HEADS):
        sl = slice(h * NA_HEAD_DIM, (h + 1) * NA_HEAD_DIM)
        qh = q[:, sl] * (NA_HEAD_DIM ** -0.5)
        s1 = lax.dot_general(qh, kw[:, sl], NT_DIMS, preferred_element_type=f32) + bias_ref[0, h]
        s2 = lax.dot_general(qh, kc[:, sl], NT_DIMS, preferred_element_type=f32)
        m = jnp.maximum(jnp.max(s1, axis=-1, keepdims=True), jnp.max(s2, axis=-1, keepdims=True))
        p1 = jnp.exp(s1 - m)
        p2 = jnp.exp(s2 - m)
        den = jnp.sum(p1, axis=-1, keepdims=True) + jnp.sum(p2, axis=-1, keepdims=True)
        o = (jnp.dot(p1.astype(bf16), vw[:, sl], preferred_element_type=f32)
             + jnp.dot(p2.astype(bf16), vc[:, sl], preferred_element_type=f32))
        outs.append(o / den)
    z = z_ref[0].astype(f32)
    o = jnp.concatenate(outs, axis=-1) * (z * jax.nn.sigmoid(z))
    o_ref[0] = o.astype(o_ref.dtype)


def na_attention(px, pc, rpb):
    b, l, _ = px.shape
    lc = pc.shape[1]
    rows = l // GRID_W
    n_tiles = rows // NA_TQ_ROWS
    bias = na_bias_tables(rpb, rows)
    w = NA_WIDTH
    cq, ck, cv, cz = [(C_NA_QKV + j * w) // w for j in range(3)] + [C_NA_Z // w]

    def bias_map(bi, t):
        return (jnp.where(t == 0, 0, jnp.where(t == n_tiles - 1, 2, 1)), 0, 0, 0)

    return pl.pallas_call(
        functools.partial(_na_kernel, rows=rows),
        grid=(b, n_tiles),
        in_specs=[pl.BlockSpec((1, NA_TQ, w), lambda bi, t: (bi, t, cq)),
                  pl.BlockSpec((1, l, w), lambda bi, t: (bi, 0, ck)),
                  pl.BlockSpec((1, l, w), lambda bi, t: (bi, 0, cv)),
                  pl.BlockSpec((1, NA_TQ, w), lambda bi, t: (bi, t, cz)),
                  pl.BlockSpec((1, lc, w), lambda bi, t: (bi, 0, ck)),
                  pl.BlockSpec((1, lc, w), lambda bi, t: (bi, 0, cv)),
                  pl.BlockSpec((1, NA_HEADS, NA_TQ, NA_KV), bias_map)],
        out_specs=pl.BlockSpec((1, NA_TQ, w), lambda bi, t: (bi, t, 0)),
        out_shape=jax.ShapeDtypeStruct((b, l, w), bf16),
        compiler_params=pltpu.CompilerParams(dimension_semantics=("arbitrary", "arbitrary"),
                                             vmem_limit_bytes=VMEM_LIMIT),
        name="na_attention",
    )(px, px, px, px, pc, pc, bias)


def _ctx_attn_kernel(q_ref, k_ref, v_ref, z_ref, o_ref):
    q = q_ref[0]
    k = k_ref[0]
    v = v_ref[0]
    outs = []
    for h in range(NA_HEADS):
        sl = slice(h * NA_HEAD_DIM, (h + 1) * NA_HEAD_DIM)
        qh = q[:, sl] * (NA_HEAD_DIM ** -0.5)
        s = lax.dot_general(qh, k[:, sl], NT_DIMS, preferred_element_type=f32)
        p = jnp.exp(s - jnp.max(s, axis=-1, keepdims=True))
        den = jnp.sum(p, axis=-1, keepdims=True)
        outs.append(jnp.dot(p.astype(bf16), v[:, sl], preferred_element_type=f32) / den)
    z = z_ref[0].astype(f32)
    o_ref[0] = (jnp.concatenate(outs, axis=-1) * (z * jax.nn.sigmoid(z))).astype(o_ref.dtype)


def ctx_attention(pc):
    b, lc, _ = pc.shape
    w = NA_WIDTH
    cq, ck, cv, cz = [(C_NA_QKV + j * w) // w for j in range(3)] + [C_NA_Z // w]
    spec = lambda cidx: pl.BlockSpec((1, lc, w), lambda bi: (bi, 0, cidx))
    return pl.pallas_call(
        _ctx_attn_kernel,
        grid=(b,),
        in_specs=[spec(cq), spec(ck), spec(cv), spec(cz)],
        out_specs=pl.BlockSpec((1, lc, w), lambda bi: (bi, 0, 0)),
        out_shape=jax.ShapeDtypeStruct((b, lc, w), bf16),
        compiler_params=pltpu.CompilerParams(dimension_semantics=("arbitrary",), vmem_limit_bytes=VMEM_LIMIT),
        name="ctx_attention",
    )(pc, pc, pc, pc)


HY_CH = 128
FFT_N2 = 128
ROW_CHUNK = 256
STRIDE_PAD = 8
HY_FILT_ROWS = 512
HY_BANDS = (HY_EMB - 1) // 2


def _hy_filter_kernel(fr_ref, w1t_ref, w1c_ref, w1s_ref, b1_ref, w2_ref, b2_ref, w3_ref, b3_ref, freq_ref, wo_ref,
                      dl_ref, o_ref, *, l):
    i = pl.program_id(0)
    n = i * HY_FILT_ROWS + lax.broadcasted_iota(jnp.int32, (HY_FILT_ROWS, 1), 0)
    lag = jnp.where(n < l, n, 2 * l - n).astype(f32)
    t = lag * (1.0 / (l - 1))
    ang = (lag * (2.0 * math.pi / l)) * fr_ref[...]
    dot = functools.partial(jnp.dot, preferred_element_type=f32, precision=HIGHEST)
    pre = t * w1t_ref[...] + dot(jnp.cos(ang), w1c_ref[...]) - dot(jnp.sin(ang), w1s_ref[...]) + b1_ref[...]
    hid = jnp.sin(freq_ref[0:1] * pre)
    hid = jnp.sin(freq_ref[1:2] * (dot(hid, w2_ref[...]) + b2_ref[...]))
    hid = jnp.sin(freq_ref[2:3] * (dot(hid, w3_ref[...]) + b3_ref[...]))
    filt = dot(hid, wo_ref[...])
    window = jnp.exp(-t * dl_ref[...])
    live = jnp.where(n == l, 0.0, 1.0)
    fwd = n < l
    for o in range(HY_ORDER):
        a = filt[:, (2 * o) * HY_WIDTH:(2 * o + 1) * HY_WIDTH]
        b = filt[:, (2 * o + 1) * HY_WIDTH:(2 * o + 2) * HY_WIDTH]
        o_ref[:, o * HY_WIDTH:(o + 1) * HY_WIDTH] = jnp.where(fwd, a, b) * window * live


def hyena_filter_circular(l, w1, b1, w2, b2, w3, b3, freq, wout):
    n = 2 * l
    assert n % HY_FILT_ROWS == 0
    hid = w1.shape[1]
    fr = np.zeros((1, LANE), np.float32)
    fr[0, :HY_BANDS] = np.linspace(1e-4, HY_BANDS - 1, HY_BANDS, dtype=np.float32)
    padrows = lambda w: jnp.concatenate([w, jnp.zeros((LANE - w.shape[0], hid), f32)], axis=0)
    w1t, w1c, w1s = w1[0:1], padrows(w1[1:1 + HY_BANDS]), padrows(w1[1 + HY_BANDS:])
    max_decay = math.log(HY_DECAY_TARGET) / HY_FAST_DECAY
    min_decay = math.log(HY_DECAY_TARGET) / HY_SLOW_DECAY
    deltas = np.abs(np.linspace(min_decay, max_decay, HY_WIDTH, dtype=np.float32))[None]
    full = lambda a: pl.BlockSpec(a.shape, lambda i: (0,) * a.ndim)
    args = [jnp.asarray(fr), w1t, w1c, w1s, b1[None], w2, b2[None], w3, b3[None], freq, wout, jnp.asarray(deltas)]
    return pl.pallas_call(
        functools.partial(_hy_filter_kernel, l=l),
        grid=(n // HY_FILT_ROWS,),
        in_specs=[full(a) for a in args],
        out_specs=pl.BlockSpec((HY_FILT_ROWS, HY_ORDER * HY_WIDTH), lambda i: (i, 0)),
        out_shape=jax.ShapeDtypeStruct((n, HY_ORDER * HY_WIDTH), f32),
        compiler_params=pltpu.CompilerParams(dimension_semantics=("arbitrary",), vmem_limit_bytes=VMEM_LIMIT),
        name="hyena_filter",
    )(*args)


@functools.lru_cache(maxsize=None)
def _fft_consts(n, k_in, k_out):
    n1 = n // FFT_N2
    k1 = np.arange(n1)
    n2 = np.arange(FFT_N2)
    tt = FFT_N2 * np.arange(k_in)[None, None, :] + n2[:, None, None]
    ang = -2.0 * np.pi * (k1[None, :, None] * tt) / n
    f1 = np.concatenate([np.cos(ang), np.sin(ang)], axis=1)
    a2 = -2.0 * np.pi * np.outer(n2, n2) / FFT_N2
    cr, ci = np.cos(a2), np.sin(a2)
    fblk = np.block([[cr, -ci], [ci, cr]])
    fiblk = np.block([[cr, ci], [-ci, cr]])
    to = FFT_N2 * np.arange(k_out)[None, :, None] + n2[:, None, None]
    ango = 2.0 * np.pi * (k1[None, None, :] * to) / n
    hinv = np.concatenate([np.cos(ango), -np.sin(ango)], axis=2) / n
    return f1, fblk, fiblk, hinv


FFT_OUTER_UNROLL = 8
FFT_INNER_UNROLL = 2
FFT_INNER_GROUPS = 8


def _fft_fwd_stage1(x_ref, f1_ref, ar_ref, ai_ref, k_in, n1, sa, dt, prec):
    def body(it, c):
        n2s = [it * FFT_OUTER_UNROLL + u for u in range(FFT_OUTER_UNROLL)]
        xs = [x_ref[pl.ds(n2, k_in, stride=FFT_N2), :].astype(dt) for n2 in n2s]
        res = [jnp.dot(f1_ref[n2], x, preferred_element_type=f32, precision=prec) for n2, x in zip(n2s, xs)]
        for n2, a in zip(n2s, res):
            off = pl.multiple_of(n2 * sa, 8)
            ar_ref[pl.ds(off, n1), :] = a[:n1]
            ai_ref[pl.ds(off, n1), :] = a[n1:]
        return c
    lax.fori_loop(0, FFT_N2 // FFT_OUTER_UNROLL, body, 0)


def _fft_inner_load(ar_ref, ai_ref, k1s, sa, dt):
    zr = jnp.concatenate([ar_ref[pl.ds(k1, FFT_N2, stride=sa), :] for k1 in k1s], axis=1)
    zi = jnp.concatenate([ai_ref[pl.ds(k1, FFT_N2, stride=sa), :] for k1 in k1s], axis=1)
    return jnp.concatenate([zr, zi], axis=0).astype(dt)


def _filter_fft_kernel(k_ref, f1_ref, fblk_ref, hr_ref, hi_ref, ar_ref, ai_ref, *, n):
    n1 = n // FFT_N2
    sa = n1 + STRIDE_PAD
    ch = k_ref.shape[-1]
    _fft_fwd_stage1(k_ref, f1_ref, ar_ref, ai_ref, n1, n1, sa, f32, HIGHEST)

    def body(it, c):
        k1s = [it * FFT_INNER_UNROLL + u for u in range(FFT_INNER_UNROLL)]
        xx = jnp.dot(fblk_ref[...], _fft_inner_load(ar_ref, ai_ref, k1s, sa, f32), preferred_element_type=f32,
                     precision=HIGHEST)
        for u, k1 in enumerate(k1s):
            off = pl.multiple_of(k1 * FFT_N2, FFT_N2)
            hr_ref[pl.ds(off, FFT_N2), :] = xx[:FFT_N2, u * ch:(u + 1) * ch]
            hi_ref[pl.ds(off, FFT_N2), :] = xx[FFT_N2:, u * ch:(u + 1) * ch]
        return c
    lax.fori_loop(0, n1 // FFT_INNER_UNROLL, body, 0)


def filter_spectrum(kern):
    n, c = kern.shape
    n1 = n // FFT_N2
    f1, fblk, _, _ = _fft_consts(n, n1, 1)
    f1 = jnp.asarray(f1, f32)
    fblk = jnp.asarray(fblk, f32)
    sa = n1 + STRIDE_PAD
    return pl.pallas_call(
        functools.partial(_filter_fft_kernel, n=n),
        grid=(c // HY_CH,),
        in_specs=[pl.BlockSpec((n, HY_CH), lambda j: (0, j)),
                  pl.BlockSpec(f1.shape, lambda j: (0, 0, 0)),
                  pl.BlockSpec(fblk.shape, lambda j: (0, 0))],
        out_specs=[pl.BlockSpec((n, HY_CH), lambda j: (0, j))] * 2,
        out_shape=[jax.ShapeDtypeStruct((n, c), f32)] * 2,
        scratch_shapes=[pltpu.VMEM((FFT_N2 * sa, HY_CH), f32)] * 2,
        compiler_params=pltpu.CompilerParams(dimension_semantics=("arbitrary",), vmem_limit_bytes=VMEM_LIMIT),
        name="filter_spectrum",
    )(kern, f1, fblk)


def _conv3_rows(src_ref, w_ref, pad_ref, dst_ref, l):
    zeros = jnp.zeros((8, src_ref.shape[-1]), f32)
    pad_ref[0:8, :] = zeros
    pad_ref[l + 8:l + 16, :] = zeros

    def cp(i, c):
        r = pl.multiple_of(i * ROW_CHUNK, ROW_CHUNK)
        pad_ref[pl.ds(8 + r, ROW_CHUNK), :] = src_ref[0, pl.ds(r, ROW_CHUNK), :].astype(f32)
        return c
    lax.fori_loop(0, l // ROW_CHUNK, cp, 0)
    w = w_ref[...].astype(f32)

    def cv(i, c):
        r = pl.multiple_of(i * ROW_CHUNK, ROW_CHUNK)
        blk = pad_ref[pl.ds(r, ROW_CHUNK + 16), :]
        dst_ref[pl.ds(r, ROW_CHUNK), :] = (w[0:1] * blk[7:7 + ROW_CHUNK] + w[1:2] * blk[8:8 + ROW_CHUNK]
                                           + w[2:3] * blk[9:9 + ROW_CHUNK])
        return c
    lax.fori_loop(0, l // ROW_CHUNK, cv, 0)


def _hy_order_kernel(yin_ref, graw_ref, z_ref, cwy_ref, cwg_ref, skip_ref, hr_ref, hi_ref, f1_ref, fblk_ref, fiblk_ref,
                     hinv_ref, o_ref, pad_ref, y_ref, gate_ref, c_ref, ar_ref, ai_ref, br_ref, bi_ref,
                     *, l, first, last):
    n = 2 * l
    n1 = n // FFT_N2
    k1n = l // FFT_N2
    sa = n1 + STRIDE_PAD
    sb = FFT_N2 + STRIDE_PAD
    if first:
        _conv3_rows(yin_ref, cwy_ref, pad_ref, y_ref, l)
        src_ref = y_ref
    else:
        src_ref = yin_ref.at[0]
    _conv3_rows(graw_ref, cwg_ref, pad_ref, gate_ref, l)

    _fft_fwd_stage1(src_ref, f1_ref, ar_ref, ai_ref, k1n, n1, sa, bf16, None)

    ch = c_ref.shape[-1]

    def mid(it, c):
        groups = [[(it * FFT_INNER_GROUPS + gi) * FFT_INNER_UNROLL + u for u in range(FFT_INNER_UNROLL)]
                  for gi in range(FFT_INNER_GROUPS)]
        zs = [_fft_inner_load(ar_ref, ai_ref, k1s, sa, bf16) for k1s in groups]
        xxs = [jnp.dot(fblk_ref[...], z, preferred_element_type=f32) for z in zs]
        yys = []
        for k1s, xx in zip(groups, xxs):
            xr, xi = xx[:FFT_N2], xx[FFT_N2:]
            hr = jnp.concatenate([hr_ref[pl.ds(pl.multiple_of(k1 * FFT_N2, FFT_N2), FFT_N2), :] for k1 in k1s], axis=1)
            hi = jnp.concatenate([hi_ref[pl.ds(pl.multiple_of(k1 * FFT_N2, FFT_N2), FFT_N2), :] for k1 in k1s], axis=1)
            yys.append(jnp.concatenate([xr * hr - xi * hi, xr * hi + xi * hr], axis=0).astype(bf16))
        bbs = [jnp.dot(fiblk_ref[...], yy, preferred_element_type=f32) for yy in yys]
        for k1s, bb in zip(groups, bbs):
            for u, k1 in enumerate(k1s):
                boff = pl.multiple_of(k1 * sb, 8)
                br_ref[pl.ds(boff, FFT_N2), :] = bb[:FFT_N2, u * ch:(u + 1) * ch]
                bi_ref[pl.ds(boff, FFT_N2), :] = bb[FFT_N2:, u * ch:(u + 1) * ch]
        return c
    lax.fori_loop(0, n1 // (FFT_INNER_UNROLL * FFT_INNER_GROUPS), mid, 0)

    def inv2(it, c):
        n2s = [it * FFT_OUTER_UNROLL + u for u in range(FFT_OUTER_UNROLL)]
        bbs = [jnp.concatenate([br_ref[pl.ds(n2, n1, stride=sb), :], bi_ref[pl.ds(n2, n1, stride=sb), :]],
                               axis=0).astype(bf16) for n2 in n2s]
        res = [jnp.dot(hinv_ref[n2], bb, preferred_element_type=f32) for n2, bb in zip(n2s, bbs)]
        for n2, o in zip(n2s, res):
            c_ref[pl.ds(n2, k1n, stride=FFT_N2), :] = o
        return c
    lax.fori_loop(0, FFT_N2 // FFT_OUTER_UNROLL, inv2, 0)

    skip = skip_ref[0].astype(f32)

    def fin(i, c):
        r = pl.multiple_of(i * ROW_CHUNK, ROW_CHUNK)
        y = src_ref[pl.ds(r, ROW_CHUNK), :]
        y = gate_ref[pl.ds(r, ROW_CHUNK), :] * (c_ref[pl.ds(r, ROW_CHUNK), :] + y * skip)
        if last:
            z = z_ref[0, pl.ds(r, ROW_CHUNK), :].astype(f32)
            y = y * (z * jax.nn.sigmoid(z))
        o_ref[0, pl.ds(r, ROW_CHUNK), :] = y.astype(o_ref.dtype)
        return c
    lax.fori_loop(0, l // ROW_CHUNK, fin, 0)


def hyena_order(yin, px, hr, hi, conv_w, skip, order, first, last):
    b, l, _ = px.shape
    n = 2 * l
    n1 = n // FFT_N2
    k1n = l // FFT_N2
    sa, sb = n1 + STRIDE_PAD, FFT_N2 + STRIDE_PAD
    nh = HY_WIDTH // HY_CH
    f1, fblk, fiblk, hinv = [jnp.asarray(a, bf16) for a in _fft_consts(n, k1n, k1n)]
    cb = lambda col: col // HY_CH
    c_v, c_g, c_z = cb(C_HY_VX), cb(C_HY_VX + (order + 1) * HY_WIDTH), cb(C_HY_Z)
    once = pl.Buffered(1)
    yin_spec = (pl.BlockSpec((1, l, HY_CH), lambda h, bi: (bi, 0, c_v + h)) if first
                else pl.BlockSpec((1, l, HY_CH), lambda h, bi: (bi, 0, h)))
    return pl.pallas_call(
        functools.partial(_hy_order_kernel, l=l, first=first, last=last),
        grid=(nh, b),
        in_specs=[yin_spec,
                  pl.BlockSpec((1, l, HY_CH), lambda h, bi: (bi, 0, c_g + h)),
                  pl.BlockSpec((1, l, HY_CH), lambda h, bi: (bi, 0, c_z + h)),
                  pl.BlockSpec((3, HY_CH), lambda h, bi: (0, h)),
                  pl.BlockSpec((3, HY_CH), lambda h, bi: (0, (order + 1) * nh + h)),
                  pl.BlockSpec((1, 1, HY_CH), lambda h, bi: (order, 0, h)),
                  pl.BlockSpec((n, HY_CH), lambda h, bi: (0, order * nh + h), pipeline_mode=once),
                  pl.BlockSpec((n, HY_CH), lambda h, bi: (0, order * nh + h), pipeline_mode=once),
                  pl.BlockSpec(f1.shape, lambda h, bi: (0, 0, 0), pipeline_mode=once),
                  pl.BlockSpec(fblk.shape, lambda h, bi: (0, 0), pipeline_mode=once),
                  pl.BlockSpec(fiblk.shape, lambda h, bi: (0, 0), pipeline_mode=once),
                  pl.BlockSpec(hinv.shape, lambda h, bi: (0, 0, 0), pipeline_mode=once)],
        out_specs=pl.BlockSpec((1, l, HY_CH), lambda h, bi: (bi, 0, h)),
        out_shape=jax.ShapeDtypeStruct((b, l, HY_WIDTH), bf16 if last else f32),
        scratch_shapes=[pltpu.VMEM((l + 16, HY_CH), f32), pltpu.VMEM((l, HY_CH), f32), pltpu.VMEM((l, HY_CH), f32),
                        pltpu.VMEM((l, HY_CH), f32),
                        pltpu.VMEM((FFT_N2 * sa, HY_CH), f32), pltpu.VMEM((FFT_N2 * sa, HY_CH), f32),
                        pltpu.VMEM((n1 * sb, HY_CH), f32), pltpu.VMEM((n1 * sb, HY_CH), f32)],
        compiler_params=pltpu.CompilerParams(dimension_semantics=("arbitrary", "arbitrary"),
                                             vmem_limit_bytes=56 * 1024 * 1024),
        name=f"hyena_order{order}",
    )(yin, px, px, conv_w, conv_w, skip.reshape(HY_ORDER, 1, HY_WIDTH), hr, hi, f1, fblk, fiblk, hinv)


def hyena_latent(px, conv_w, skip, w1, b1, w2, b2, w3, b3, freq, wout):
    l = px.shape[1]
    kern = hyena_filter_circular(l, w1, b1, w2, b2, w3, b3, freq, wout)
    hr, hi = filter_spectrum(kern)
    y1 = hyena_order(px, px, hr, hi, conv_w, skip, 0, True, False)
    return hyena_order(y1, px, hr, hi, conv_w, skip, 1, False, True)


@functools.lru_cache(maxsize=None)
def _dense_dft_consts(l):
    n = 2 * l
    k = np.arange(n)
    ang = -2.0 * np.pi * np.outer(k, np.arange(n)) / n
    fwd = np.concatenate([np.cos(ang), np.sin(ang)], axis=0)
    angi = 2.0 * np.pi * np.outer(np.arange(l), k) / n
    inv = np.concatenate([np.cos(angi), -np.sin(angi)], axis=1) / n
    return fwd, inv


def _hy_ctx_kernel(p_ref, kern_ref, cw_ref, skip_ref, fwd_ref, inv_ref, o_ref, pad_ref, t_ref, *, l):
    n = 2 * l
    dot = functools.partial(jnp.dot, preferred_element_type=f32, precision=HIGHEST)
    w = HY_WIDTH

    def conv3(col):
        pad_ref[0:8, :] = jnp.zeros((8, w), f32)
        pad_ref[l + 8:l + 16, :] = jnp.zeros((8, w), f32)
        pad_ref[8:8 + l, :] = p_ref[0, :, col:col + w].astype(f32)
        cw = cw_ref[:, col - C_HY_VX:col - C_HY_VX + w].astype(f32)
        return cw[0:1] * pad_ref[7:7 + l, :] + cw[1:2] * pad_ref[8:8 + l, :] + cw[2:3] * pad_ref[9:9 + l, :]

    y = conv3(C_HY_VX)
    for o in range(HY_ORDER):
        gate = conv3(C_HY_VX + (o + 1) * w)
        hh = dot(fwd_ref[...], kern_ref[:, o * w:(o + 1) * w])
        xx = dot(fwd_ref[:, :l], y)
        xr, xi, hr, hi = xx[:n], xx[n:], hh[:n], hh[n:]
        t_ref[0:n, :] = xr * hr - xi * hi
        t_ref[n:2 * n, :] = xr * hi + xi * hr
        conv = dot(inv_ref[...], t_ref[...])
        y = gate * (conv + y * skip_ref[o:o + 1, :].astype(f32))
    z = p_ref[0, :, C_HY_Z:C_HY_Z + w].astype(f32)
    o_ref[0] = (y * (z * jax.nn.sigmoid(z))).astype(o_ref.dtype)


def hyena_ctx(pc, conv_w, skip, w1, b1, w2, b2, w3, b3, freq, wout):
    b, lc, _ = pc.shape
    n = 2 * lc
    kern = hyena_filter_circular(lc, w1, b1, w2, b2, w3, b3, freq, wout)
    fwd, inv = [jnp.asarray(a, f32) for a in _dense_dft_consts(lc)]
    full = lambda a: pl.BlockSpec(a.shape, lambda bi: (0,) * a.ndim)
    return pl.pallas_call(
        functools.partial(_hy_ctx_kernel, l=lc),
        grid=(b,),
        in_specs=[pl.BlockSpec((1, lc, MAIN_W), lambda bi: (bi, 0, 0)), full(kern), full(conv_w), full(skip),
                  full(fwd), full(inv)],
        out_specs=pl.BlockSpec((1, lc, HY_WIDTH), lambda bi: (bi, 0, 0)),
        out_shape=jax.ShapeDtypeStruct((b, lc, HY_WIDTH), bf16),
        scratch_shapes=[pltpu.VMEM((lc + 16, HY_WIDTH), f32), pltpu.VMEM((2 * n, HY_WIDTH), f32)],
        compiler_params=pltpu.CompilerParams(dimension_semantics=("arbitrary",), vmem_limit_bytes=VMEM_LIMIT),
        name="hyena_ctx",
    )(pc, kern, conv_w, skip, fwd, inv)


DN_PREP_ROWS = 256
DN_SUB_ROWS = 64
DN_HALO = 8


@functools.lru_cache(maxsize=None)
def _rope_tables(l):
    half = DN_HEAD_DIM // 2
    nf = half // 2
    t = np.arange(l)
    inv = (np.float32(ROPE_BASE) ** (-np.arange(nf, dtype=np.float32) / nf)).astype(np.float32)
    ang_r = ((t // GRID_W).astype(np.float32)[:, None] * inv[None, :]).astype(np.float32)
    ang_c = ((t % GRID_W).astype(np.float32)[:, None] * inv[None, :]).astype(np.float32)
    cos = np.concatenate([np.cos(ang_r), np.cos(ang_r), np.cos(ang_c), np.cos(ang_c)], axis=1)
    sin = np.concatenate([-np.sin(ang_r), np.sin(ang_r), -np.sin(ang_c), np.sin(ang_c)], axis=1)
    return cos.astype(np.float32), sin.astype(np.float32)


def _dn_prep_kernel(*refs, rope, n_tiles):
    (q_ref, qp_ref, qn_ref, k_ref, kp_ref, kn_ref, v_ref, vp_ref, vn_ref, g_ref, cw_ref, cos_ref, sin_ref, al_ref,
     dt_ref, qo_ref, ko_ref, vo_ref, go_ref, pad_ref) = refs
    i = pl.program_id(1)
    t = DN_PREP_ROWS
    w = DN_WIDTH
    lane = lax.broadcasted_iota(jnp.int32, (DN_SUB_ROWS, w), 1)
    first_half = (lane % (DN_HEAD_DIM // 2)) < (DN_HEAD_DIM // 4)
    for idx, (m_ref, p_ref, n_ref, o_ref) in enumerate(((q_ref, qp_ref, qn_ref, qo_ref), (k_ref, kp_ref, kn_ref, ko_ref),
                                                        (v_ref, vp_ref, vn_ref, vo_ref))):
        pad_ref[0:DN_HALO, :] = jnp.where(i == 0, 0.0, p_ref[0].astype(f32))
        pad_ref[DN_HALO:DN_HALO + t, :] = m_ref[0].astype(f32)
        pad_ref[DN_HALO + t:2 * DN_HALO + t, :] = jnp.where(i == n_tiles - 1, 0.0, n_ref[0].astype(f32))
        cw = cw_ref[:, idx * w:(idx + 1) * w].astype(f32)

        def sub(s, c):
            r = pl.multiple_of(s * DN_SUB_ROWS, DN_SUB_ROWS)
            blk = pad_ref[pl.ds(r, DN_SUB_ROWS + 2 * DN_HALO), :]
            base = DN_HALO - DN_CONV_W // 2
            y = cw[0:1] * blk[base:base + DN_SUB_ROWS]
            for j in range(1, DN_CONV_W):
                y = y + cw[j:j + 1] * blk[base + j:base + j + DN_SUB_ROWS]
            y = y * jax.nn.sigmoid(y)
            if idx < 2:
                if rope:
                    cos = jnp.concatenate([cos_ref[pl.ds(r, DN_SUB_ROWS), :]] * DN_HEADS, axis=1)
                    sin = jnp.concatenate([sin_ref[pl.ds(r, DN_SUB_ROWS), :]] * DN_HEADS, axis=1)
                    q4 = DN_HEAD_DIM // 4
                    swapped = jnp.where(first_half, pltpu.roll(y, w - q4, 1), pltpu.roll(y, q4, 1))
                    y = y * cos + swapped * sin
                outs = []
                for h in range(DN_HEADS):
                    yh = y[:, h * DN_HEAD_DIM:(h + 1) * DN_HEAD_DIM]
                    nrm = lax.rsqrt(jnp.sum(yh * yh, axis=-1, keepdims=True) + NORM_EPS)
                    if idx == 0:
                        nrm = nrm * (DN_HEAD_DIM ** -0.5)
                    outs.append(yh * nrm)
                y = jnp.concatenate(outs, axis=1)
            o_ref[0, pl.ds(r, DN_SUB_ROWS), :] = y.astype(o_ref.dtype)
            return c
        lax.fori_loop(0, t // DN_SUB_ROWS, sub, 0)

    g = g_ref[0]
    glane = lax.broadcasted_iota(jnp.int32, g.shape, 1)
    xa = g + dt_ref[...]
    softplus = jnp.maximum(xa, 0.0) + jnp.log1p(jnp.exp(-jnp.abs(xa)))
    go_ref[0] = jnp.where(glane < 2 * DN_HEADS, jax.nn.sigmoid(g), -jnp.exp(al_ref[...]) * softplus)


def dn_prep(p, gates_raw, conv_w, a_log, dt_bias, rope):
    b, l, _ = p.shape
    t = DN_PREP_ROWS
    w = DN_WIDTH
    n_tiles = l // t
    hb = t // DN_HALO
    cblk = C_DN_QKV // w
    cos, sin = [jnp.asarray(a) for a in _rope_tables(l)]
    pad = jnp.zeros((2 * DN_HEADS,), f32)
    rest = jnp.zeros((GATE_W - 4 * DN_HEADS,), f32)
    al = jnp.concatenate([pad, a_log.reshape(-1), rest])[None]
    dt = jnp.concatenate([pad, dt_bias.reshape(-1), rest])[None]

    def slab(j):
        return [pl.BlockSpec((1, t, w), lambda bi, i: (bi, i, cblk + j)),
                pl.BlockSpec((1, DN_HALO, w), lambda bi, i: (bi, jnp.maximum(i * hb - 1, 0), cblk + j)),
                pl.BlockSpec((1, DN_HALO, w), lambda bi, i: (bi, jnp.minimum((i + 1) * hb, l // DN_HALO - 1), cblk + j))]

    row = lambda a: pl.BlockSpec(a.shape, lambda bi, i: (0, 0))
    tab = pl.BlockSpec((t, DN_HEAD_DIM), lambda bi, i: (i, 0))
    out_blk = pl.BlockSpec((1, t, w), lambda bi, i: (bi, i, 0))
    return pl.pallas_call(
        functools.partial(_dn_prep_kernel, rope=rope, n_tiles=n_tiles),
        grid=(b, n_tiles),
        in_specs=slab(0) + slab(1) + slab(2) + [
            pl.BlockSpec((1, t, GATE_W), lambda bi, i: (bi, i, 0)), row(conv_w), tab, tab, row(al), row(dt)],
        out_specs=[out_blk, out_blk, out_blk, pl.BlockSpec((1, t, GATE_W), lambda bi, i: (bi, i, 0))],
        out_shape=[jax.ShapeDtypeStruct((b, l, w), bf16)] * 3 + [jax.ShapeDtypeStruct((b, l, GATE_W), f32)],
        scratch_shapes=[pltpu.VMEM((t + 2 * DN_HALO, w), f32)],
        compiler_params=pltpu.CompilerParams(dimension_semantics=("arbitrary", "arbitrary"),
                                             vmem_limit_bytes=VMEM_LIMIT),
        name="dn_prep",
    )(p, p, p, p, p, p, p, p, p, gates_raw, conv_w, cos, sin, al, dt)


DN_GROUP = 4
DN_GROUP_ROWS = DN_GROUP * DN_CHUNK
DN_BATCH_BLOCK = 2
DN_UNROLL = 2


def _bdot(a, b):
    return jnp.dot(a.astype(bf16), b.astype(bf16), preferred_element_type=f32)


def _dn_chunk_kernel(qc_ref, kc_ref, vc_ref, gc_ref, qx_ref, kx_ref, vx_ref, gx_ref, u_ref, wq_ref, ak_ref, eg_ref):
    g = pl.program_id(1)
    c = DN_CHUNK
    dh = DN_HEAD_DIM
    is_ctx = g == 0
    ri = lax.broadcasted_iota(jnp.int32, (c, c), 0)
    ci = lax.broadcasted_iota(jnp.int32, (c, c), 1)
    eye = (ri == ci).astype(f32)
    er = lax.broadcasted_iota(jnp.int32, (GATE_W, GATE_W), 0)
    ec = lax.broadcasted_iota(jnp.int32, (GATE_W, GATE_W), 1)
    eye_b = (er == ec).astype(bf16)
    tri_b = (ri >= ci).astype(bf16)
    masks = [((ri <= ci) if d else (ri >= ci), (ri < ci) if d else (ri > ci)) for d in range(2)]

    def sel_dot(sel, x, dims):
        hi = x.astype(bf16)
        mid = (x - hi.astype(f32)).astype(bf16)
        lo = (x - hi.astype(f32) - mid.astype(f32)).astype(bf16)
        return sum(lax.dot_general(sel, t, dims, preferred_element_type=f32) for t in (hi, mid, lo))

    def body(it, carry):
        units = []
        for jj in range(DN_UNROLL):
            j = it * DN_UNROLL + jj
            r = pl.multiple_of(j * c, c)
            pick = lambda a_ref, b_ref: jnp.where(is_ctx, a_ref[0, pl.ds(r, c), :], b_ref[0, pl.ds(r, c), :])
            q, k, v, gates = pick(qc_ref, qx_ref), pick(kc_ref, kx_ref), pick(vc_ref, vx_ref), pick(gc_ref, gx_ref)
            gtot = jnp.sum(gates, axis=0, keepdims=True)
            gfwd = sel_dot(tri_b, gates, (((1,), (0,)), ((), ())))
            gcs = (gfwd, gtot - gfwd + gates)
            g_t = sel_dot(eye_b, jnp.concatenate([gfwd, gates], axis=0), NT_DIMS)
            gcs_t = (g_t[:, :c], jnp.sum(g_t[:, c:], axis=1, keepdims=True) - g_t[:, :c] + g_t[:, c:])
            qks = [lax.dot_general(q[:, h * dh:(h + 1) * dh], k[:, h * dh:(h + 1) * dh], NT_DIMS,
                                   preferred_element_type=f32) for h in range(DN_HEADS)]
            kks = [lax.dot_general(k[:, h * dh:(h + 1) * dh], k[:, h * dh:(h + 1) * dh], NT_DIMS,
                                   preferred_element_type=f32) for h in range(DN_HEADS)]
            for d in range(2):
                incl, strict = masks[d]
                egs = []
                for h in range(DN_HEADS):
                    hs = slice(h * dh, (h + 1) * dh)
                    col = d * DN_HEADS + h
                    gl = 2 * DN_HEADS + col
                    kh, vh, qh = k[:, hs].astype(f32), v[:, hs].astype(f32), q[:, hs].astype(f32)
                    beta = gates[:, col:col + 1]
                    gcol = gcs[d][:, gl:gl + 1]
                    glast = gtot[:, gl:gl + 1]
                    decay = jnp.where(incl, jnp.exp(jnp.minimum(gcol - gcs_t[d][gl:gl + 1, :], 0.0)), 0.0)
                    eg = jnp.exp(gcol)
                    wq_ref[d, 0, j, h, c:2 * c, :] = (qh * eg).astype(bf16)
                    ak_ref[d, 0, j, h, 0:c, :] = jnp.where(incl, qks[h] * decay, 0.0).astype(bf16)
                    ak_ref[d, 0, j, h, c:c + dh, :] = jnp.transpose(kh * jnp.exp(glast - gcol)).astype(bf16)
                    egs.append(jnp.broadcast_to(jnp.exp(glast), (1, GATE_W)))
                    kb = kh * beta
                    units.append(dict(j=j, d=d, h=h, n=-jnp.where(strict, beta * kks[h] * decay, 0.0),
                                      rhs=jnp.concatenate([vh * beta, kb * eg], axis=1).astype(bf16)))
                eg_ref[d, 0, j] = jnp.concatenate(egs + [jnp.zeros((8 - DN_HEADS, GATE_W), f32)], axis=0)
        for un in units:
            un["t"] = eye + un["n"]
            un["p"] = _bdot(un["n"], un["n"])
        for _ in range(int(math.log2(c)) - 2):
            for un in units:
                both = _bdot(jnp.concatenate([un["p"], un["t"]], axis=0), un["p"])
                un["p"], un["t"] = both[:c], un["t"] + both[c:]
        for un in units:
            un["t"] = un["t"] + _bdot(un["t"], un["p"])
        for un in units:
            sol = _bdot(un["t"], un["rhs"])
            u_ref[un["d"], 0, un["j"], un["h"]] = sol[:, :dh].astype(bf16)
            wq_ref[un["d"], 0, un["j"], un["h"], 0:c, :] = sol[:, dh:].astype(bf16)
        return carry
    lax.fori_loop(0, DN_GROUP // DN_UNROLL, body, 0)


def dn_chunks(qc, kc, vc, gc, qx, kx, vx, gx):
    b, lc, w = qc.shape
    l = qx.shape[1]
    assert lc == DN_GROUP_ROWS and l % DN_GROUP_ROWS == 0
    ng = 1 + l // DN_GROUP_ROWS
    nch = ng * DN_GROUP
    c, dh, nhd = DN_CHUNK, DN_HEAD_DIM, DN_HEADS
    cmap = lambda bi, g: (bi, 0, 0)
    xmap = lambda bi, g: (bi, jnp.maximum(g - 1, 0), 0)
    blk = lambda width, m: pl.BlockSpec((1, DN_GROUP_ROWS, width), m)
    omap = lambda bi, g: (0, bi, g, 0, 0, 0)
    return pl.pallas_call(
        _dn_chunk_kernel,
        grid=(b, ng),
        in_specs=[blk(w, cmap), blk(w, cmap), blk(w, cmap), blk(GATE_W, cmap),
                  blk(w, xmap), blk(w, xmap), blk(w, xmap), blk(GATE_W, xmap)],
        out_specs=[pl.BlockSpec((2, 1, DN_GROUP, nhd, c, dh), omap),
                   pl.BlockSpec((2, 1, DN_GROUP, nhd, 2 * c, dh), omap),
                   pl.BlockSpec((2, 1, DN_GROUP, nhd, c + dh, c), omap),
                   pl.BlockSpec((2, 1, DN_GROUP, 8, GATE_W), lambda bi, g: (0, bi, g, 0, 0))],
        out_shape=[jax.ShapeDtypeStruct((2, b, nch, nhd, c, dh), bf16),
                   jax.ShapeDtypeStruct((2, b, nch, nhd, 2 * c, dh), bf16),
                   jax.ShapeDtypeStruct((2, b, nch, nhd, c + dh, c), bf16),
                   jax.ShapeDtypeStruct((2, b, nch, 8, GATE_W), f32)],
        compiler_params=pltpu.CompilerParams(dimension_semantics=("arbitrary", "arbitrary"),
                                             vmem_limit_bytes=VMEM_LIMIT),
        name="dn_chunks",
    )(qc, kc, vc, gc, qx, kx, vx, gx)


def _bwd_group(s, ng):
    return jnp.where(s == 0, 0, ng - s)


def _dn_serial_kernel(uf_ref, wqf_ref, akf_ref, egf_ref, ub_ref, wqb_ref, akb_ref, egb_ref, of_ref, ob_ref, s_ref):
    c = DN_CHUNK
    dh = DN_HEAD_DIM

    @pl.when(pl.program_id(1) == 0)
    def _():
        s_ref[...] = jnp.zeros(s_ref.shape, f32)

    dirs = ((uf_ref, wqf_ref, akf_ref, egf_ref, of_ref), (ub_ref, wqb_ref, akb_ref, egb_ref, ob_ref))
    units = [(e, d, h) for e in range(DN_BATCH_BLOCK) for d in range(2) for h in range(DN_HEADS)]
    states = {un: s_ref[un] for un in units}
    for jj in range(DN_GROUP):
        js = (jj, DN_GROUP - 1 - jj)
        r1 = {(e, d, h): jnp.dot(dirs[d][1][0, e, js[d], h], states[e, d, h].astype(bf16), preferred_element_type=f32)
              for e, d, h in units}
        r2 = {}
        for e, d, h in units:
            v_new = dirs[d][0][0, e, js[d], h].astype(f32) - r1[e, d, h][:c]
            r2[e, d, h] = jnp.dot(dirs[d][2][0, e, js[d], h], v_new.astype(bf16), preferred_element_type=f32)
        for e, d, h in units:
            j = js[d]
            dirs[d][4][e, j * c:(j + 1) * c, h * dh:(h + 1) * dh] = r1[e, d, h][c:] + r2[e, d, h][:c]
            states[e, d, h] = states[e, d, h] * dirs[d][3][0, e, j, h:h + 1, :] + r2[e, d, h][c:]
    for un in units:
        s_ref[un] = states[un]


def dn_serial(u, wq, ak, eg):
    _, b, nch, nhd, c, dh = u.shape
    ng = nch // DN_GROUP
    w = nhd * dh
    fmap6 = lambda bi, s: (0, bi, s, 0, 0, 0)
    bmap6 = lambda bi, s: (1, bi, _bwd_group(s, ng), 0, 0, 0)
    fmap5 = lambda bi, s: (0, bi, s, 0, 0)
    bmap5 = lambda bi, s: (1, bi, _bwd_group(s, ng), 0, 0)
    bb = DN_BATCH_BLOCK
    assert b % bb == 0
    blk6 = lambda a, m: pl.BlockSpec((1, bb, DN_GROUP) + a.shape[3:], m)
    egblk = lambda m: pl.BlockSpec((1, bb, DN_GROUP, 8, GATE_W), m)
    return pl.pallas_call(
        _dn_serial_kernel,
        grid=(b // bb, ng),
        in_specs=[blk6(u, fmap6), blk6(wq, fmap6), blk6(ak, fmap6), egblk(fmap5),
                  blk6(u, bmap6), blk6(wq, bmap6), blk6(ak, bmap6), egblk(bmap5)],
        out_specs=[pl.BlockSpec((bb, DN_GROUP_ROWS, w), lambda bi, s: (bi, s, 0)),
                   pl.BlockSpec((bb, DN_GROUP_ROWS, w), lambda bi, s: (bi, _bwd_group(s, ng), 0))],
        out_shape=[jax.ShapeDtypeStruct((b, nch * c, w), f32)] * 2,
        scratch_shapes=[pltpu.VMEM((bb, 2, nhd, dh, dh), f32)],
        compiler_params=pltpu.CompilerParams(dimension_semantics=("arbitrary", "arbitrary"),
                                             vmem_limit_bytes=VMEM_LIMIT),
        name="dn_serial",
    )(u, wq, ak, eg, u, wq, ak, eg)


def _dn_out_kernel(of_ref, ob_ref, z_ref, nw_ref, y_ref):
    o = of_ref[0] + ob_ref[0]
    z = z_ref[0].astype(f32)
    nw = nw_ref[...].astype(f32)
    outs = []
    for h in range(DN_HEADS):
        oh = o[:, h * DN_HEAD_DIM:(h + 1) * DN_HEAD_DIM]
        outs.append(oh * lax.rsqrt(jnp.mean(oh * oh, axis=-1, keepdims=True) + NORM_EPS) * nw)
    y_ref[0] = (jnp.concatenate(outs, axis=1) * (z * jax.nn.sigmoid(z))).astype(y_ref.dtype)


def dn_output(o_f, o_b, p, norm_w, first_block):
    b, l, _ = p.shape
    w = DN_WIDTH
    tm = DN_GROUP_ROWS
    omap = lambda bi, i: (bi, first_block + i, 0)
    return pl.pallas_call(
        _dn_out_kernel,
        grid=(b, l // tm),
        in_specs=[pl.BlockSpec((1, tm, w), omap), pl.BlockSpec((1, tm, w), omap),
                  pl.BlockSpec((1, tm, w), lambda bi, i: (bi, i, C_DN_Z // w)),
                  pl.BlockSpec((1, DN_HEAD_DIM), lambda bi, i: (0, 0))],
        out_specs=pl.BlockSpec((1, tm, w), lambda bi, i: (bi, i, 0)),
        out_shape=jax.ShapeDtypeStruct((b, l, w), bf16),
        compiler_params=pltpu.CompilerParams(dimension_semantics=("arbitrary", "arbitrary"),
                                             vmem_limit_bytes=VMEM_LIMIT),
        name="dn_output",
    )(o_f, o_b, p, norm_w.reshape(1, DN_HEAD_DIM))


def _pack_w_in(w_in):
    depth, d, _ = w_in.shape
    main = jnp.concatenate([w_in[:, :, :3072], w_in[:, :, 3088:4112]], axis=-1).astype(bf16)
    gate = jnp.concatenate([w_in[:, :, 3072:3088], jnp.zeros((depth, d, GATE_W - 16), w_in.dtype)],
                           axis=-1).astype(bf16)
    return main, gate


def kernel(x, c, ctx, c_ctx, w_ada, b_ada, g_pre, g_post, w_in, w_out, na_rpb, dn_conv, dn_a_log, dn_dt_bias,
           dn_norm, hy_conv, hy_w1, hy_b1, hy_w2, hy_b2, hy_w3, hy_b3, hy_freq, hy_wout, hy_skip):
    bsz, l, d = x.shape
    lc = ctx.shape[1]
    depth = w_in.shape[0]
    cond = jnp.concatenate([c, c_ctx[None], jnp.zeros((7, d), f32)], axis=0)
    mod = modulation_all(cond, w_ada, b_ada)
    w_main, w_gate = _pack_w_in(w_in)
    w_out_b = w_out.astype(bf16)

    for i in range(depth):
        last = i == depth - 1
        shift_x, scale_x, gate_x = [mod[i, :bsz, j * d:(j + 1) * d][:, None] for j in range(3)]
        shift_c, scale_c, gate_c = [mod[i, bsz:bsz + 1, j * d:(j + 1) * d][:, None] for j in range(3)]
        px, gx = in_projection(x, g_pre[i], scale_x, shift_x, w_main[i], w_gate[i], 512)
        pc, gc = in_projection(ctx, g_pre[i], scale_c, shift_c, w_main[i], w_gate[i], 256)

        hy_args = (hy_conv[i], hy_skip[i], hy_w1[i], hy_b1[i], hy_w2[i], hy_b2[i], hy_w3[i], hy_b3[i], hy_freq[i],
                   hy_wout[i])
        dn_args = (dn_conv[i], dn_a_log[i], dn_dt_bias[i])

        out_a_x = na_attention(px, pc, na_rpb[i])

        dn_c = dn_prep(pc, gc, *dn_args, False)
        dn_x = dn_prep(px, gx, *dn_args, True)
        do_f, do_b = dn_serial(*dn_chunks(*dn_c, *dn_x))
        out_b_x = dn_output(do_f, do_b, px, dn_norm[i], lc // DN_GROUP_ROWS)

        out_c_x = hyena_latent(px, *hy_args)

        new_x = out_projection(out_a_x, out_b_x, out_c_x, w_out_b[i], g_post[i], gate_x, x, 512)

        if not last:
            out_a_c = ctx_attention(pc)
            out_b_c = dn_output(do_f, do_b, pc, dn_norm[i], 0)
            out_c_c = hyena_ctx(pc, *hy_args)
            ctx = out_projection(out_a_c, out_b_c, out_c_c, w_out_b[i], g_post[i], gate_c, ctx, lc)
        x = new_x
    return x
```

```python
import functools
import math

import numpy as np
import jax
import jax.numpy as jnp
from jax import lax
from jax.experimental import pallas as pl
from jax.experimental.pallas import tpu as pltpu

D_MODEL = 1024
GRID_W = 64
NA_HEAD_DIM = 64
NA_WIDTH = 256
NA_HEADS = 4
NA_WIN_ROWS = 8
NA_WIN_COLS = 16
DN_HEAD_DIM = 128
DN_WIDTH = 512
DN_HEADS = 4
DN_CONV_W = 5
DN_CHUNK = 64
HY_WIDTH = 256
HY_ORDER = 2
HY_EMB = 33
HY_DECAY_TARGET = 1e-2
HY_FAST_DECAY = 0.3
HY_SLOW_DECAY = 1.5
ROPE_BASE = 10000.0
NORM_EPS = 1e-6

C_NA_QKV, C_NA_Z, C_DN_QKV, C_DN_Z, C_HY_VX, C_HY_Z = 0, 768, 1024, 2560, 3072, 3840
MAIN_W = 4096
GATE_W = 128
LANE = 128
VMEM_LIMIT = 52 * 1024 * 1024

f32 = jnp.float32
bf16 = jnp.bfloat16
HIGHEST = lax.Precision.HIGHEST
NT_DIMS = (((1,), (1,)), ((), ()))
TN_DIMS = (((0,), (0,)), ((), ()))


def _mod_kernel(c_ref, w_ref, b_ref, o_ref):
    c = c_ref[...]
    a = c * jax.nn.sigmoid(c)
    o_ref[0] = jnp.dot(a, w_ref[0], preferred_element_type=f32, precision=lax.Precision.HIGHEST) + b_ref[0]


def modulation_all(cond, w_ada, b_ada):
    depth, d, d3 = w_ada.shape
    r = cond.shape[0]
    tn = 512
    return pl.pallas_call(
        _mod_kernel,
        grid=(depth, d3 // tn),
        in_specs=[pl.BlockSpec((r, d), lambda i, j: (0, 0)),
                  pl.BlockSpec((1, d, tn), lambda i, j: (i, 0, j)),
                  pl.BlockSpec((1, 1, tn), lambda i, j: (i, 0, j))],
        out_specs=pl.BlockSpec((1, r, tn), lambda i, j: (i, 0, j)),
        out_shape=jax.ShapeDtypeStruct((depth, r, d3), f32),
        compiler_params=pltpu.CompilerParams(dimension_semantics=("arbitrary", "arbitrary"),
                                             vmem_limit_bytes=VMEM_LIMIT),
        name="modulation",
    )(cond, w_ada, b_ada.reshape(depth, 1, d3))


def _inproj_kernel(x_ref, g_ref, sc_ref, sh_ref, w_ref, wg_ref, o_ref, og_ref):
    x = x_ref[0]
    ms = jnp.mean(x * x, axis=-1, keepdims=True)
    h = x * lax.rsqrt(ms + NORM_EPS) * g_ref[...]
    h = (h * (1.0 + sc_ref[0]) + sh_ref[0]).astype(bf16)
    for j in range(MAIN_W // 1024):
        o_ref[0, :, j * 1024:(j + 1) * 1024] = jnp.dot(
            h, w_ref[:, j * 1024:(j + 1) * 1024], preferred_element_type=f32).astype(bf16)
    og_ref[0] = jnp.dot(h, wg_ref[...], preferred_element_type=f32)


def in_projection(x, g, scale, shift, w_main, w_gate, tm):
    b, l, d = x.shape
    per_b = scale.shape[0] == b
    mod_map = (lambda bi, i: (bi, 0, 0)) if per_b else (lambda bi, i: (0, 0, 0))
    return pl.pallas_call(
        _inproj_kernel,
        grid=(b, l // tm),
        in_specs=[pl.BlockSpec((1, tm, d), lambda bi, i: (bi, i, 0)),
                  pl.BlockSpec((1, d), lambda bi, i: (0, 0)),
                  pl.BlockSpec((1, 1, d), mod_map),
                  pl.BlockSpec((1, 1, d), mod_map),
                  pl.BlockSpec((d, MAIN_W), lambda bi, i: (0, 0)),
                  pl.BlockSpec((d, GATE_W), lambda bi, i: (0, 0))],
        out_specs=[pl.BlockSpec((1, tm, MAIN_W), lambda bi, i: (bi, i, 0)),
                   pl.BlockSpec((1, tm, GATE_W), lambda bi, i: (bi, i, 0))],
        out_shape=[jax.ShapeDtypeStruct((b, l, MAIN_W), bf16),
                   jax.ShapeDtypeStruct((b, l, GATE_W), f32)],
        compiler_params=pltpu.CompilerParams(dimension_semantics=("arbitrary", "arbitrary"),
                                             vmem_limit_bytes=VMEM_LIMIT),
        name="in_projection",
    )(x, g.reshape(1, d), scale, shift, w_main, w_gate)


def _outproj_kernel(a_ref, b_ref, c_ref, w_ref, g_ref, gate_ref, x_ref, o_ref):
    y = (jnp.dot(a_ref[0], w_ref[0:NA_WIDTH], preferred_element_type=f32)
         + jnp.dot(b_ref[0], w_ref[NA_WIDTH:NA_WIDTH + DN_WIDTH], preferred_element_type=f32)
         + jnp.dot(c_ref[0], w_ref[NA_WIDTH + DN_WIDTH:], preferred_element_type=f32))
    ms = jnp.mean(y * y, axis=-1, keepdims=True)
    yn = y * lax.rsqrt(ms + NORM_EPS) * g_ref[...]
    o_ref[0] = x_ref[0] + gate_ref[0] * yn


def out_projection(out_a, out_b, out_c, w_out, g_post, gate, x, tm):
    b, l, d = x.shape
    per_b = gate.shape[0] == b
    mod_map = (lambda bi, i: (bi, 0, 0)) if per_b else (lambda bi, i: (0, 0, 0))
    part = lambda a: pl.BlockSpec((1, tm, a.shape[-1]), lambda bi, i: (bi, i, 0))
    return pl.pallas_call(
        _outproj_kernel,
        grid=(b, l // tm),
        in_specs=[part(out_a), part(out_b), part(out_c),
                  pl.BlockSpec((d, d), lambda bi, i: (0, 0)),
                  pl.BlockSpec((1, d), lambda bi, i: (0, 0)),
                  pl.BlockSpec((1, 1, d), mod_map),
                  pl.BlockSpec((1, tm, d), lambda bi, i: (bi, i, 0))],
        out_specs=pl.BlockSpec((1, tm, d), lambda bi, i: (bi, i, 0)),
        out_shape=jax.ShapeDtypeStruct((b, l, d), f32),
        compiler_params=pltpu.CompilerParams(dimension_semantics=("arbitrary", "arbitrary"),
                                             vmem_limit_bytes=VMEM_LIMIT),
        name="out_projection",
    )(out_a, out_b, out_c, w_out, g_post.reshape(1, d), gate, x)


NA_TQ_ROWS = 4
NA_TQ = NA_TQ_ROWS * GRID_W
NA_KV_ROWS = NA_TQ_ROWS + NA_WIN_ROWS
NA_KV = NA_KV_ROWS * GRID_W
MASK_VALUE = -1e30


def _na_window_start(t, rows):
    return jnp.clip(t * NA_TQ_ROWS - NA_WIN_ROWS // 2, 0, rows - NA_KV_ROWS)


def _na_bias_kernel(rpb_ref, sel_ref, mask_ref, o_ref):
    o_ref[...] = jnp.dot(rpb_ref[...], sel_ref[...], preferred_element_type=f32, precision=HIGHEST) + mask_ref[...]


def na_bias_tables(rpb, rows):
    nh, ndr, ndc = rpb.shape
    col = np.arange(GRID_W)
    dc = np.clip(col[None, :] - col[:, None], -(NA_WIN_COLS - 1), NA_WIN_COLS - 1) + (NA_WIN_COLS - 1)
    c0 = np.clip(col - NA_WIN_COLS // 2, 0, GRID_W - NA_WIN_COLS)
    in_win = (col[None, :] >= c0[:, None]) & (col[None, :] < c0[:, None] + NA_WIN_COLS)
    sel = np.zeros((LANE, GRID_W * GRID_W), np.float32)
    sel[dc.reshape(-1), np.arange(GRID_W * GRID_W)] = 1.0
    cmask = np.where(in_win, 0.0, MASK_VALUE).astype(np.float32).reshape(1, -1)
    rpb2 = jnp.pad(rpb.astype(f32).reshape(nh * ndr, ndc), ((0, 0), (0, LANE - ndc)))
    tab = pl.pallas_call(
        _na_bias_kernel,
        out_shape=jax.ShapeDtypeStruct((nh * ndr, GRID_W * GRID_W), f32),
        name="na_bias",
    )(rpb2, jnp.asarray(sel), jnp.asarray(cmask)).reshape(nh, ndr, GRID_W, GRID_W)
    masked = jnp.full((nh, GRID_W, GRID_W), MASK_VALUE, f32)
    n_tiles = rows // NA_TQ_ROWS
    tabs = []
    for t in (0, 1, n_tiles - 1):
        r0 = t * NA_TQ_ROWS
        ws = int(np.clip(r0 - NA_WIN_ROWS // 2, 0, rows - NA_KV_ROWS))
        blocks = []
        for qr in range(r0, r0 + NA_TQ_ROWS):
            band0 = int(np.clip(qr - NA_WIN_ROWS // 2, 0, rows - NA_WIN_ROWS))
            blocks.append(jnp.concatenate(
                [tab[:, kr - qr + NA_WIN_ROWS - 1] if band0 <= kr < band0 + NA_WIN_ROWS else masked
                 for kr in range(ws, ws + NA_KV_ROWS)], axis=-1))
        tabs.append(jnp.concatenate(blocks, axis=-2))
    return jnp.stack(tabs)


def _na_kernel(q_ref, k_ref, v_ref, z_ref, kc_ref, vc_ref, bias_ref, o_ref, *, rows):
    t = pl.program_id(1)
    ws = pl.multiple_of(_na_window_start(t, rows) * GRID_W, GRID_W)
    q = q_ref[0]
    kw = k_ref[0, pl.ds(ws, NA_KV), :]
    vw = v_ref[0, pl.ds(ws, NA_KV), :]
    kc = kc_ref[0]
    vc = vc_ref[0]
    outs = []
    for h in range(NA_HEADS):
        sl = slice(h * NA_HEAD_DIM, (h + 1) * NA_HEAD_DIM)
        qh = q[:, sl] * (NA_HEAD_DIM ** -0.5)
        s1 = lax.dot_general(qh, kw[:, sl], NT_DIMS, preferred_element_type=f32) + bias_ref[0, h]
        s2 = lax.dot_general(qh, kc[:, sl], NT_DIMS, preferred_element_type=f32)
        m = jnp.maximum(jnp.max(s1, axis=-1, keepdims=True), jnp.max(s2, axis=-1, keepdims=True))
        p1 = jnp.exp(s1 - m)
        p2 = jnp.exp(s2 - m)
        den = jnp.sum(p1, axis=-1, keepdims=True) + jnp.sum(p2, axis=-1, keepdims=True)
        o = (jnp.dot(p1.astype(bf16), vw[:, sl], preferred_element_type=f32)
             + jnp.dot(p2.astype(bf16), vc[:, sl], preferred_element_type=f32))
        outs.append(o / den)
    z = z_ref[0].astype(f32)
    o = jnp.concatenate(outs, axis=-1) * (z * jax.nn.sigmoid(z))
    o_ref[0] = o.astype(o_ref.dtype)


def na_attention(px, pc, rpb):
    b, l, _ = px.shape
    lc = pc.shape[1]
    rows = l // GRID_W
    n_tiles = rows // NA_TQ_ROWS
    bias = na_bias_tables(rpb, rows)
    w = NA_WIDTH
    cq, ck, cv, cz = [(C_NA_QKV + j * w) // w for j in range(3)] + [C_NA_Z // w]

    def bias_map(bi, t):
        return (jnp.where(t == 0, 0, jnp.where(t == n_tiles - 1, 2, 1)), 0, 0, 0)

    return pl.pallas_call(
        functools.partial(_na_kernel, rows=rows),
        grid=(b, n_tiles),
        in_specs=[pl.BlockSpec((1, NA_TQ, w), lambda bi, t: (bi, t, cq)),
                  pl.BlockSpec((1, l, w), lambda bi, t: (bi, 0, ck)),
                  pl.BlockSpec((1, l, w), lambda bi, t: (bi, 0, cv)),
                  pl.BlockSpec((1, NA_TQ, w), lambda bi, t: (bi, t, cz)),
                  pl.BlockSpec((1, lc, w), lambda bi, t: (bi, 0, ck)),
                  pl.BlockSpec((1, lc, w), lambda bi, t: (bi, 0, cv)),
                  pl.BlockSpec((1, NA_HEADS, NA_TQ, NA_KV), bias_map)],
        out_specs=pl.BlockSpec((1, NA_TQ, w), lambda bi, t: (bi, t, 0)),
        out_shape=jax.ShapeDtypeStruct((b, l, w), bf16),
        compiler_params=pltpu.CompilerParams(dimension_semantics=("arbitrary", "arbitrary"),
                                             vmem_limit_bytes=VMEM_LIMIT),
        name="na_attention",
    )(px, px, px, px, pc, pc, bias)


def _ctx_attn_kernel(q_ref, k_ref, v_ref, z_ref, o_ref):
    q = q_ref[0]
    k = k_ref[0]
    v = v_ref[0]
    outs = []
    for h in range(NA_HEADS):
        sl = slice(h * NA_HEAD_DIM, (h + 1) * NA_HEAD_DIM)
        qh = q[:, sl] * (NA_HEAD_DIM ** -0.5)
        s = lax.dot_general(qh, k[:, sl], NT_DIMS, preferred_element_type=f32)
        p = jnp.exp(s - jnp.max(s, axis=-1, keepdims=True))
        den = jnp.sum(p, axis=-1, keepdims=True)
        outs.append(jnp.dot(p.astype(bf16), v[:, sl], preferred_element_type=f32) / den)
    z = z_ref[0].astype(f32)
    o_ref[0] = (jnp.concatenate(outs, axis=-1) * (z * jax.nn.sigmoid(z))).astype(o_ref.dtype)


def ctx_attention(pc):
    b, lc, _ = pc.shape
    w = NA_WIDTH
    cq, ck, cv, cz = [(C_NA_QKV + j * w) // w for j in range(3)] + [C_NA_Z // w]
    spec = lambda cidx: pl.BlockSpec((1, lc, w), lambda bi: (bi, 0, cidx))
    return pl.pallas_call(
        _ctx_attn_kernel,
        grid=(b,),
        in_specs=[spec(cq), spec(ck), spec(cv), spec(cz)],
        out_specs=pl.BlockSpec((1, lc, w), lambda bi: (bi, 0, 0)),
        out_shape=jax.ShapeDtypeStruct((b, lc, w), bf16),
        compiler_params=pltpu.CompilerParams(dimension_semantics=("arbitrary",), vmem_limit_bytes=VMEM_LIMIT),
        name="ctx_attention",
    )(pc, pc, pc, pc)


HY_CH = 128
FFT_N2 = 128
ROW_CHUNK = 256
STRIDE_PAD = 8
HY_FILT_ROWS = 512
HY_BANDS = (HY_EMB - 1) // 2


def _hy_filter_kernel(fr_ref, w1t_ref, w1c_ref, w1s_ref, b1_ref, w2_ref, b2_ref, w3_ref, b3_ref, freq_ref, wo_ref,
                      dl_ref, o_ref, *, l):
    i = pl.program_id(0)
    n = i * HY_FILT_ROWS + lax.broadcasted_iota(jnp.int32, (HY_FILT_ROWS, 1), 0)
    lag = jnp.where(n < l, n, 2 * l - n).astype(f32)
    t = lag * (1.0 / (l - 1))
    ang = (lag * (2.0 * math.pi / l)) * fr_ref[...]
    dot = functools.partial(jnp.dot, preferred_element_type=f32, precision=HIGHEST)
    pre = t * w1t_ref[...] + dot(jnp.cos(ang), w1c_ref[...]) - dot(jnp.sin(ang), w1s_ref[...]) + b1_ref[...]
    hid = jnp.sin(freq_ref[0:1] * pre)
    hid = jnp.sin(freq_ref[1:2] * (dot(hid, w2_ref[...]) + b2_ref[...]))
    hid = jnp.sin(freq_ref[2:3] * (dot(hid, w3_ref[...]) + b3_ref[...]))
    filt = dot(hid, wo_ref[...])
    window = jnp.exp(-t * dl_ref[...])
    live = jnp.where(n == l, 0.0, 1.0)
    fwd = n < l
    for o in range(HY_ORDER):
        a = filt[:, (2 * o) * HY_WIDTH:(2 * o + 1) * HY_WIDTH]
        b = filt[:, (2 * o + 1) * HY_WIDTH:(2 * o + 2) * HY_WIDTH]
        o_ref[:, o * HY_WIDTH:(o + 1) * HY_WIDTH] = jnp.where(fwd, a, b) * window * live


def hyena_filter_circular(l, w1, b1, w2, b2, w3, b3, freq, wout):
    n = 2 * l
    assert n % HY_FILT_ROWS == 0
    hid = w1.shape[1]
    fr = np.zeros((1, LANE), np.float32)
    fr[0, :HY_BANDS] = np.linspace(1e-4, HY_BANDS - 1, HY_BANDS, dtype=np.float32)
    padrows = lambda w: jnp.concatenate([w, jnp.zeros((LANE - w.shape[0], hid), f32)], axis=0)
    w1t, w1c, w1s = w1[0:1], padrows(w1[1:1 + HY_BANDS]), padrows(w1[1 + HY_BANDS:])
    max_decay = math.log(HY_DECAY_TARGET) / HY_FAST_DECAY
    min_decay = math.log(HY_DECAY_TARGET) / HY_SLOW_DECAY
    deltas = np.abs(np.linspace(min_decay, max_decay, HY_WIDTH, dtype=np.float32))[None]
    full = lambda a: pl.BlockSpec(a.shape, lambda i: (0,) * a.ndim)
    args = [jnp.asarray(fr), w1t, w1c, w1s, b1[None], w2, b2[None], w3, b3[None], freq, wout, jnp.asarray(deltas)]
    return pl.pallas_call(
        functools.partial(_hy_filter_kernel, l=l),
        grid=(n // HY_FILT_ROWS,),
        in_specs=[full(a) for a in args],
        out_specs=pl.BlockSpec((HY_FILT_ROWS, HY_ORDER * HY_WIDTH), lambda i: (i, 0)),
        out_shape=jax.ShapeDtypeStruct((n, HY_ORDER * HY_WIDTH), f32),
        compiler_params=pltpu.CompilerParams(dimension_semantics=("arbitrary",), vmem_limit_bytes=VMEM_LIMIT),
        name="hyena_filter",
    )(*args)


@functools.lru_cache(maxsize=None)
def _fft_consts(n, k_in, k_out):
    n1 = n // FFT_N2
    k1 = np.arange(n1)
    n2 = np.arange(FFT_N2)
    tt = FFT_N2 * np.arange(k_in)[None, None, :] + n2[:, None, None]
    ang = -2.0 * np.pi * (k1[None, :, None] * tt) / n
    f1 = np.concatenate([np.cos(ang), np.sin(ang)], axis=1)
    a2 = -2.0 * np.pi * np.outer(n2, n2) / FFT_N2
    cr, ci = np.cos(a2), np.sin(a2)
    fblk = np.block([[cr, -ci], [ci, cr]])
    fiblk = np.block([[cr, ci], [-ci, cr]])
    to = FFT_N2 * np.arange(k_out)[None, :, None] + n2[:, None, None]
    ango = 2.0 * np.pi * (k1[None, None, :] * to) / n
    hinv = np.concatenate([np.cos(ango), -np.sin(ango)], axis=2) / n
    return f1, fblk, fiblk, hinv


FFT_OUTER_UNROLL = 8
FFT_INNER_UNROLL = 2
FFT_INNER_GROUPS = 8


def _fft_fwd_stage1(x_ref, f1_ref, ar_ref, ai_ref, k_in, n1, sa, dt, prec):
    def body(it, c):
        n2s = [it * FFT_OUTER_UNROLL + u for u in range(FFT_OUTER_UNROLL)]
        xs = [x_ref[pl.ds(n2, k_in, stride=FFT_N2), :].astype(dt) for n2 in n2s]
        res = [jnp.dot(f1_ref[n2], x, preferred_element_type=f32, precision=prec) for n2, x in zip(n2s, xs)]
        for n2, a in zip(n2s, res):
            off = pl.multiple_of(n2 * sa, 8)
            ar_ref[pl.ds(off, n1), :] = a[:n1]
            ai_ref[pl.ds(off, n1), :] = a[n1:]
        return c
    lax.fori_loop(0, FFT_N2 // FFT_OUTER_UNROLL, body, 0)


def _fft_inner_load(ar_ref, ai_ref, k1s, sa, dt):
    zr = jnp.concatenate([ar_ref[pl.ds(k1, FFT_N2, stride=sa), :] for k1 in k1s], axis=1)
    zi = jnp.concatenate([ai_ref[pl.ds(k1, FFT_N2, stride=sa), :] for k1 in k1s], axis=1)
    return jnp.concatenate([zr, zi], axis=0).astype(dt)


def _filter_fft_kernel(k_ref, f1_ref, fblk_ref, hr_ref, hi_ref, ar_ref, ai_ref, *, n):
    n1 = n // FFT_N2
    sa = n1 + STRIDE_PAD
    ch = k_ref.shape[-1]
    _fft_fwd_stage1(k_ref, f1_ref, ar_ref, ai_ref, n1, n1, sa, f32, HIGHEST)

    def body(it, c):
        k1s = [it * FFT_INNER_UNROLL + u for u in range(FFT_INNER_UNROLL)]
        xx = jnp.dot(fblk_ref[...], _fft_inner_load(ar_ref, ai_ref, k1s, sa, f32), preferred_element_type=f32,
                     precision=HIGHEST)
        for u, k1 in enumerate(k1s):
            off = pl.multiple_of(k1 * FFT_N2, FFT_N2)
            hr_ref[pl.ds(off, FFT_N2), :] = xx[:FFT_N2, u * ch:(u + 1) * ch]
            hi_ref[pl.ds(off, FFT_N2), :] = xx[FFT_N2:, u * ch:(u + 1) * ch]
        return c
    lax.fori_loop(0, n1 // FFT_INNER_UNROLL, body, 0)


def filter_spectrum(kern):
    n, c = kern.shape
    n1 = n // FFT_N2
    f1, fblk, _, _ = _fft_consts(n, n1, 1)
    f1 = jnp.asarray(f1, f32)
    fblk = jnp.asarray(fblk, f32)
    sa = n1 + STRIDE_PAD
    return pl.pallas_call(
        functools.partial(_filter_fft_kernel, n=n),
        grid=(c // HY_CH,),
        in_specs=[pl.BlockSpec((n, HY_CH), lambda j: (0, j)),
                  pl.BlockSpec(f1.shape, lambda j: (0, 0, 0)),
                  pl.BlockSpec(fblk.shape, lambda j: (0, 0))],
        out_specs=[pl.BlockSpec((n, HY_CH), lambda j: (0, j))] * 2,
        out_shape=[jax.ShapeDtypeStruct((n, c), f32)] * 2,
        scratch_shapes=[pltpu.VMEM((FFT_N2 * sa, HY_CH), f32)] * 2,
        compiler_params=pltpu.CompilerParams(dimension_semantics=("arbitrary",), vmem_limit_bytes=VMEM_LIMIT),
        name="filter_spectrum",
    )(kern, f1, fblk)


def _conv3_rows(src_ref, w_ref, pad_ref, dst_ref, l):
    zeros = jnp.zeros((8, src_ref.shape[-1]), f32)
    pad_ref[0:8, :] = zeros
    pad_ref[l + 8:l + 16, :] = zeros

    def cp(i, c):
        r = pl.multiple_of(i * ROW_CHUNK, ROW_CHUNK)
        pad_ref[pl.ds(8 + r, ROW_CHUNK), :] = src_ref[0, pl.ds(r, ROW_CHUNK), :].astype(f32)
        return c
    lax.fori_loop(0, l // ROW_CHUNK, cp, 0)
    w = w_ref[...].astype(f32)

    def cv(i, c):
        r = pl.multiple_of(i * ROW_CHUNK, ROW_CHUNK)
        blk = pad_ref[pl.ds(r, ROW_CHUNK + 16), :]
        dst_ref[pl.ds(r, ROW_CHUNK), :] = (w[0:1] * blk[7:7 + ROW_CHUNK] + w[1:2] * blk[8:8 + ROW_CHUNK]
                                           + w[2:3] * blk[9:9 + ROW_CHUNK])
        return c
    lax.fori_loop(0, l // ROW_CHUNK, cv, 0)


def _hy_order_kernel(yin_ref, graw_ref, z_ref, cwy_ref, cwg_ref, skip_ref, hr_ref, hi_ref, f1_ref, fblk_ref, fiblk_ref,
                     hinv_ref, o_ref, pad_ref, y_ref, gate_ref, c_ref, ar_ref, ai_ref, br_ref, bi_ref,
                     *, l, first, last):
    n = 2 * l
    n1 = n // FFT_N2
    k1n = l // FFT_N2
    sa = n1 + STRIDE_PAD
    sb = FFT_N2 + STRIDE_PAD
    if first:
        _conv3_rows(yin_ref, cwy_ref, pad_ref, y_ref, l)
        src_ref = y_ref
    else:
        src_ref = yin_ref.at[0]
    _conv3_rows(graw_ref, cwg_ref, pad_ref, gate_ref, l)

    _fft_fwd_stage1(src_ref, f1_ref, ar_ref, ai_ref, k1n, n1, sa, bf16, None)

    ch = c_ref.shape[-1]

    def mid(it, c):
        groups = [[(it * FFT_INNER_GROUPS + gi) * FFT_INNER_UNROLL + u for u in range(FFT_INNER_UNROLL)]
                  for gi in range(FFT_INNER_GROUPS)]
        zs = [_fft_inner_load(ar_ref, ai_ref, k1s, sa, bf16) for k1s in groups]
        xxs = [jnp.dot(fblk_ref[...], z, preferred_element_type=f32) for z in zs]
        yys = []
        for k1s, xx in zip(groups, xxs):
            xr, xi = xx[:FFT_N2], xx[FFT_N2:]
            hr = jnp.concatenate([hr_ref[pl.ds(pl.multiple_of(k1 * FFT_N2, FFT_N2), FFT_N2), :] for k1 in k1s], axis=1)
            hi = jnp.concatenate([hi_ref[pl.ds(pl.multiple_of(k1 * FFT_N2, FFT_N2), FFT_N2), :] for k1 in k1s], axis=1)
            yys.append(jnp.concatenate([xr * hr - xi * hi, xr * hi + xi * hr], axis=0).astype(bf16))
        bbs = [jnp.dot(fiblk_ref[...], yy, preferred_element_type=f32) for yy in yys]
        for k1s, bb in zip(groups, bbs):
            for u, k1 in enumerate(k1s):
                boff = pl.multiple_of(k1 * sb, 8)
                br_ref[pl.ds(boff, FFT_N2), :] = bb[:FFT_N2, u * ch:(u + 1) * ch]
                bi_ref[pl.ds(boff, FFT_N2), :] = bb[FFT_N2:, u * ch:(u + 1) * ch]
        return c
    lax.fori_loop(0, n1 // (FFT_INNER_UNROLL * FFT_INNER_GROUPS), mid, 0)

    def inv2(it, c):
        n2s = [it * FFT_OUTER_UNROLL + u for u in range(FFT_OUTER_UNROLL)]
        bbs = [jnp.concatenate([br_ref[pl.ds(n2, n1, stride=sb), :], bi_ref[pl.ds(n2, n1, stride=sb), :]],
                               axis=0).astype(bf16) for n2 in n2s]
        res = [jnp.dot(hinv_ref[n2], bb, preferred_element_type=f32) for n2, bb in zip(n2s, bbs)]
        for n2, o in zip(n2s, res):
            c_ref[pl.ds(n2, k1n, stride=FFT_N2), :] = o
        return c
    lax.fori_loop(0, FFT_N2 // FFT_OUTER_UNROLL, inv2, 0)

    skip = skip_ref[0].astype(f32)

    def fin(i, c):
        r = pl.multiple_of(i * ROW_CHUNK, ROW_CHUNK)
        y = src_ref[pl.ds(r, ROW_CHUNK), :]
        y = gate_ref[pl.ds(r, ROW_CHUNK), :] * (c_ref[pl.ds(r, ROW_CHUNK), :] + y * skip)
        if last:
            z = z_ref[0, pl.ds(r, ROW_CHUNK), :].astype(f32)
            y = y * (z * jax.nn.sigmoid(z))
        o_ref[0, pl.ds(r, ROW_CHUNK), :] = y.astype(o_ref.dtype)
        return c
    lax.fori_loop(0, l // ROW_CHUNK, fin, 0)


def hyena_order(yin, px, hr, hi, conv_w, skip, order, first, last):
    b, l, _ = px.shape
    n = 2 * l
    n1 = n // FFT_N2
    k1n = l // FFT_N2
    sa, sb = n1 + STRIDE_PAD, FFT_N2 + STRIDE_PAD
    nh = HY_WIDTH // HY_CH
    f1, fblk, fiblk, hinv = [jnp.asarray(a, bf16) for a in _fft_consts(n, k1n, k1n)]
    cb = lambda col: col // HY_CH
    c_v, c_g, c_z = cb(C_HY_VX), cb(C_HY_VX + (order + 1) * HY_WIDTH), cb(C_HY_Z)
    once = pl.Buffered(1)
    yin_spec = (pl.BlockSpec((1, l, HY_CH), lambda h, bi: (bi, 0, c_v + h)) if first
                else pl.BlockSpec((1, l, HY_CH), lambda h, bi: (bi, 0, h)))
    return pl.pallas_call(
        functools.partial(_hy_order_kernel, l=l, first=first, last=last),
        grid=(nh, b),
        in_specs=[yin_spec,
                  pl.BlockSpec((1, l, HY_CH), lambda h, bi: (bi, 0, c_g + h)),
                  pl.BlockSpec((1, l, HY_CH), lambda h, bi: (bi, 0, c_z + h)),
                  pl.BlockSpec((3, HY_CH), lambda h, bi: (0, h)),
                  pl.BlockSpec((3, HY_CH), lambda h, bi: (0, (order + 1) * nh + h)),
                  pl.BlockSpec((1, 1, HY_CH), lambda h, bi: (order, 0, h)),
                  pl.BlockSpec((n, HY_CH), lambda h, bi: (0, order * nh + h), pipeline_mode=once),
                  pl.BlockSpec((n, HY_CH), lambda h, bi: (0, order * nh + h), pipeline_mode=once),
                  pl.BlockSpec(f1.shape, lambda h, bi: (0, 0, 0), pipeline_mode=once),
                  pl.BlockSpec(fblk.shape, lambda h, bi: (0, 0), pipeline_mode=once),
                  pl.BlockSpec(fiblk.shape, lambda h, bi: (0, 0), pipeline_mode=once),
                  pl.BlockSpec(hinv.shape, lambda h, bi: (0, 0, 0), pipeline_mode=once)],
        out_specs=pl.BlockSpec((1, l, HY_CH), lambda h, bi: (bi, 0, h)),
        out_shape=jax.ShapeDtypeStruct((b, l, HY_WIDTH), bf16 if last else f32),
        scratch_shapes=[pltpu.VMEM((l + 16, HY_CH), f32), pltpu.VMEM((l, HY_CH), f32), pltpu.VMEM((l, HY_CH), f32),
                        pltpu.VMEM((l, HY_CH), f32),
                        pltpu.VMEM((FFT_N2 * sa, HY_CH), f32), pltpu.VMEM((FFT_N2 * sa, HY_CH), f32),
                        pltpu.VMEM((n1 * sb, HY_CH), f32), pltpu.VMEM((n1 * sb, HY_CH), f32)],
        compiler_params=pltpu.CompilerParams(dimension_semantics=("arbitrary", "arbitrary"),
                                             vmem_limit_bytes=56 * 1024 * 1024),
        name=f"hyena_order{order}",
    )(yin, px, px, conv_w, conv_w, skip.reshape(HY_ORDER, 1, HY_WIDTH), hr, hi, f1, fblk, fiblk, hinv)


def hyena_latent(px, conv_w, skip, w1, b1, w2, b2, w3, b3, freq, wout):
    l = px.shape[1]
    kern = hyena_filter_circular(l, w1, b1, w2, b2, w3, b3, freq, wout)
    hr, hi = filter_spectrum(kern)
    y1 = hyena_order(px, px, hr, hi, conv_w, skip, 0, True, False)
    return hyena_order(y1, px, hr, hi, conv_w, skip, 1, False, True)


@functools.lru_cache(maxsize=None)
def _dense_dft_consts(l):
    n = 2 * l
    k = np.arange(n)
    ang = -2.0 * np.pi * np.outer(k, np.arange(n)) / n
    fwd = np.concatenate([np.cos(ang), np.sin(ang)], axis=0)
    angi = 2.0 * np.pi * np.outer(np.arange(l), k) / n
    inv = np.concatenate([np.cos(angi), -np.sin(angi)], axis=1) / n
    return fwd, inv


def _hy_ctx_kernel(p_ref, kern_ref, cw_ref, skip_ref, fwd_ref, inv_ref, o_ref, pad_ref, t_ref, *, l):
    n = 2 * l
    dot = functools.partial(jnp.dot, preferred_element_type=f32, precision=HIGHEST)
    w = HY_WIDTH

    def conv3(col):
        pad_ref[0:8, :] = jnp.zeros((8, w), f32)
        pad_ref[l + 8:l + 16, :] = jnp.zeros((8, w), f32)
        pad_ref[8:8 + l, :] = p_ref[0, :, col:col + w].astype(f32)
        cw = cw_ref[:, col - C_HY_VX:col - C_HY_VX + w].astype(f32)
        return cw[0:1] * pad_ref[7:7 + l, :] + cw[1:2] * pad_ref[8:8 + l, :] + cw[2:3] * pad_ref[9:9 + l, :]

    y = conv3(C_HY_VX)
    for o in range(HY_ORDER):
        gate = conv3(C_HY_VX + (o + 1) * w)
        hh = dot(fwd_ref[...], kern_ref[:, o * w:(o + 1) * w])
        xx = dot(fwd_ref[:, :l], y)
        xr, xi, hr, hi = xx[:n], xx[n:], hh[:n], hh[n:]
        t_ref[0:n, :] = xr * hr - xi * hi
        t_ref[n:2 * n, :] = xr * hi + xi * hr
        conv = dot(inv_ref[...], t_ref[...])
        y = gate * (conv + y * skip_ref[o:o + 1, :].astype(f32))
    z = p_ref[0, :, C_HY_Z:C_HY_Z + w].astype(f32)
    o_ref[0] = (y * (z * jax.nn.sigmoid(z))).astype(o_ref.dtype)


def hyena_ctx(pc, conv_w, skip, w1, b1, w2, b2, w3, b3, freq, wout):
    b, lc, _ = pc.shape
    n = 2 * lc
    kern = hyena_filter_circular(lc, w1, b1, w2, b2, w3, b3, freq, wout)
    fwd, inv = [jnp.asarray(a, f32) for a in _dense_dft_consts(lc)]
    full = lambda a: pl.BlockSpec(a.shape, lambda bi: (0,) * a.ndim)
    return pl.pallas_call(
        functools.partial(_hy_ctx_kernel, l=lc),
        grid=(b,),
        in_specs=[pl.BlockSpec((1, lc, MAIN_W), lambda bi: (bi, 0, 0)), full(kern), full(conv_w), full(skip),
                  full(fwd), full(inv)],
        out_specs=pl.BlockSpec((1, lc, HY_WIDTH), lambda bi: (bi, 0, 0)),
        out_shape=jax.ShapeDtypeStruct((b, lc, HY_WIDTH), bf16),
        scratch_shapes=[pltpu.VMEM((lc + 16, HY_WIDTH), f32), pltpu.VMEM((2 * n, HY_WIDTH), f32)],
        compiler_params=pltpu.CompilerParams(dimension_semantics=("arbitrary",), vmem_limit_bytes=VMEM_LIMIT),
        name="hyena_ctx",
    )(pc, kern, conv_w, skip, fwd, inv)


DN_PREP_ROWS = 256
DN_SUB_ROWS = 64
DN_HALO = 16


@functools.lru_cache(maxsize=None)
def _shift_matrix():
    s = np.zeros((DN_CONV_W * DN_SUB_ROWS, DN_SUB_ROWS + 2 * DN_HALO), np.float32)
    for j in range(DN_CONV_W):
        for m in range(DN_SUB_ROWS):
            s[j * DN_SUB_ROWS + m, m + j + DN_HALO - DN_CONV_W // 2] = 1.0
    return s


@functools.lru_cache(maxsize=None)
def _rope_tables(l):
    half = DN_HEAD_DIM // 2
    nf = half // 2
    t = np.arange(l)
    inv = (np.float32(ROPE_BASE) ** (-np.arange(nf, dtype=np.float32) / nf)).astype(np.float32)
    ang_r = ((t // GRID_W).astype(np.float32)[:, None] * inv[None, :]).astype(np.float32)
    ang_c = ((t % GRID_W).astype(np.float32)[:, None] * inv[None, :]).astype(np.float32)
    cos = np.concatenate([np.cos(ang_r), np.cos(ang_r), np.cos(ang_c), np.cos(ang_c)], axis=1)
    sin = np.concatenate([-np.sin(ang_r), np.sin(ang_r), -np.sin(ang_c), np.sin(ang_c)], axis=1)
    return cos.astype(np.float32), sin.astype(np.float32)


def _dn_prep_kernel(*refs, rope, n_tiles):
    (q_ref, qp_ref, qn_ref, k_ref, kp_ref, kn_ref, v_ref, vp_ref, vn_ref, g_ref, cw_ref, sh_ref, cos_ref, sin_ref,
     al_ref, dt_ref, qo_ref, ko_ref, vo_ref, go_ref, pad_ref, taps_ref) = refs
    i = pl.program_id(1)
    t = DN_PREP_ROWS
    w = DN_WIDTH
    lane = lax.broadcasted_iota(jnp.int32, (DN_SUB_ROWS, w), 1)
    first_half = (lane % (DN_HEAD_DIM // 2)) < (DN_HEAD_DIM // 4)
    for idx, (m_ref, p_ref, n_ref, o_ref) in enumerate(((q_ref, qp_ref, qn_ref, qo_ref), (k_ref, kp_ref, kn_ref, ko_ref),
                                                        (v_ref, vp_ref, vn_ref, vo_ref))):
        pad_ref[0:DN_HALO, :] = jnp.where(i == 0, jnp.zeros_like(p_ref[0]), p_ref[0])
        pad_ref[DN_HALO:DN_HALO + t, :] = m_ref[0]
        pad_ref[DN_HALO + t:2 * DN_HALO + t, :] = jnp.where(i == n_tiles - 1, jnp.zeros_like(n_ref[0]), n_ref[0])
        cw = cw_ref[:, idx * w:(idx + 1) * w].astype(f32)
        for s in range(t // DN_SUB_ROWS):
            taps_ref[s] = jnp.dot(sh_ref[...], pad_ref[s * DN_SUB_ROWS:(s + 1) * DN_SUB_ROWS + 2 * DN_HALO, :],
                                  preferred_element_type=f32)

        def sub(s, c):
            r = s * DN_SUB_ROWS
            y = cw[0:1] * taps_ref[s, 0:DN_SUB_ROWS, :]
            for j in range(1, DN_CONV_W):
                y = y + cw[j:j + 1] * taps_ref[s, j * DN_SUB_ROWS:(j + 1) * DN_SUB_ROWS, :]
            y = y * jax.nn.sigmoid(y)
            if idx < 2:
                if rope:
                    cos = jnp.concatenate([cos_ref[pl.ds(r, DN_SUB_ROWS), :]] * DN_HEADS, axis=1)
                    sin = jnp.concatenate([sin_ref[pl.ds(r, DN_SUB_ROWS), :]] * DN_HEADS, axis=1)
                    q4 = DN_HEAD_DIM // 4
                    swapped = jnp.where(first_half, pltpu.roll(y, w - q4, 1), pltpu.roll(y, q4, 1))
                    y = y * cos + swapped * sin
                outs = []
                for h in range(DN_HEADS):
                    yh = y[:, h * DN_HEAD_DIM:(h + 1) * DN_HEAD_DIM]
                    nrm = lax.rsqrt(jnp.sum(yh * yh, axis=-1, keepdims=True) + NORM_EPS)
                    if idx == 0:
                        nrm = nrm * (DN_HEAD_DIM ** -0.5)
                    outs.append(yh * nrm)
                y = jnp.concatenate(outs, axis=1)
            o_ref[0, pl.ds(r, DN_SUB_ROWS), :] = y.astype(o_ref.dtype)
            return c
        for s in range(t // DN_SUB_ROWS):
            sub(s, 0)

    g = g_ref[0]
    glane = lax.broadcasted_iota(jnp.int32, g.shape, 1)
    xa = g + dt_ref[...]
    softplus = jnp.maximum(xa, 0.0) + jnp.log1p(jnp.exp(-jnp.abs(xa)))
    go_ref[0] = jnp.where(glane < 2 * DN_HEADS, jax.nn.sigmoid(g), -jnp.exp(al_ref[...]) * softplus)


def dn_prep(p, gates_raw, conv_w, a_log, dt_bias, rope):
    b, l, _ = p.shape
    t = DN_PREP_ROWS
    w = DN_WIDTH
    n_tiles = l // t
    hb = t // DN_HALO
    cblk = C_DN_QKV // w
    cos, sin = [jnp.asarray(a) for a in _rope_tables(l)]
    shift = jnp.asarray(_shift_matrix(), bf16)
    pad = jnp.zeros((2 * DN_HEADS,), f32)
    rest = jnp.zeros((GATE_W - 4 * DN_HEADS,), f32)
    al = jnp.concatenate([pad, a_log.reshape(-1), rest])[None]
    dt = jnp.concatenate([pad, dt_bias.reshape(-1), rest])[None]

    def slab(j):
        return [pl.BlockSpec((1, t, w), lambda bi, i: (bi, i, cblk + j)),
                pl.BlockSpec((1, DN_HALO, w), lambda bi, i: (bi, jnp.maximum(i * hb - 1, 0), cblk + j)),
                pl.BlockSpec((1, DN_HALO, w), lambda bi, i: (bi, jnp.minimum((i + 1) * hb, l // DN_HALO - 1), cblk + j))]

    row = lambda a: pl.BlockSpec(a.shape, lambda bi, i: (0, 0))
    tab = pl.BlockSpec((t, DN_HEAD_DIM), lambda bi, i: (i, 0))
    out_blk = pl.BlockSpec((1, t, w), lambda bi, i: (bi, i, 0))
    return pl.pallas_call(
        functools.partial(_dn_prep_kernel, rope=rope, n_tiles=n_tiles),
        grid=(b, n_tiles),
        in_specs=slab(0) + slab(1) + slab(2) + [
            pl.BlockSpec((1, t, GATE_W), lambda bi, i: (bi, i, 0)), row(conv_w), row(shift), tab, tab, row(al),
            row(dt)],
        out_specs=[out_blk, out_blk, out_blk, pl.BlockSpec((1, t, GATE_W), lambda bi, i: (bi, i, 0))],
        out_shape=[jax.ShapeDtypeStruct((b, l, w), bf16)] * 3 + [jax.ShapeDtypeStruct((b, l, GATE_W), f32)],
        scratch_shapes=[pltpu.VMEM((t + 2 * DN_HALO, w), bf16),
                        pltpu.VMEM((t // DN_SUB_ROWS, DN_CONV_W * DN_SUB_ROWS, w), f32)],
        compiler_params=pltpu.CompilerParams(dimension_semantics=("arbitrary", "arbitrary"),
                                             vmem_limit_bytes=VMEM_LIMIT),
        name="dn_prep",
    )(p, p, p, p, p, p, p, p, p, gates_raw, conv_w, shift, cos, sin, al, dt)


DN_GROUP = 4
DN_GROUP_ROWS = DN_GROUP * DN_CHUNK
DN_BATCH_BLOCK = 2
DN_UNROLL = 2


def _bdot(a, b):
    return jnp.dot(a.astype(bf16), b.astype(bf16), preferred_element_type=f32)


def _dn_chunk_kernel(qc_ref, kc_ref, vc_ref, gc_ref, qx_ref, kx_ref, vx_ref, gx_ref, u_ref, wq_ref, ak_ref, eg_ref):
    g = pl.program_id(1)
    c = DN_CHUNK
    dh = DN_HEAD_DIM
    is_ctx = g == 0
    ri = lax.broadcasted_iota(jnp.int32, (c, c), 0)
    ci = lax.broadcasted_iota(jnp.int32, (c, c), 1)
    eye = (ri == ci).astype(f32)
    er = lax.broadcasted_iota(jnp.int32, (GATE_W, GATE_W), 0)
    ec = lax.broadcasted_iota(jnp.int32, (GATE_W, GATE_W), 1)
    eye_b = (er == ec).astype(bf16)
    tri_b = (ri >= ci).astype(bf16)
    masks = [((ri <= ci) if d else (ri >= ci), (ri < ci) if d else (ri > ci)) for d in range(2)]

    def sel_dot(sel, x, dims):
        hi = x.astype(bf16)
        mid = (x - hi.astype(f32)).astype(bf16)
        lo = (x - hi.astype(f32) - mid.astype(f32)).astype(bf16)
        return sum(lax.dot_general(sel, t, dims, preferred_element_type=f32) for t in (hi, mid, lo))

    def body(it, carry):
        units = []
        for jj in range(DN_UNROLL):
            j = it * DN_UNROLL + jj
            r = pl.multiple_of(j * c, c)
            pick = lambda a_ref, b_ref: jnp.where(is_ctx, a_ref[0, pl.ds(r, c), :], b_ref[0, pl.ds(r, c), :])
            q, k, v, gates = pick(qc_ref, qx_ref), pick(kc_ref, kx_ref), pick(vc_ref, vx_ref), pick(gc_ref, gx_ref)
            gtot = jnp.sum(gates, axis=0, keepdims=True)
            gfwd = sel_dot(tri_b, gates, (((1,), (0,)), ((), ())))
            gcs = (gfwd, gtot - gfwd + gates)
            g_t = sel_dot(eye_b, jnp.concatenate([gfwd, gates], axis=0), NT_DIMS)
            gcs_t = (g_t[:, :c], jnp.sum(g_t[:, c:], axis=1, keepdims=True) - g_t[:, :c] + g_t[:, c:])
            qks = [lax.dot_general(q[:, h * dh:(h + 1) * dh], k[:, h * dh:(h + 1) * dh], NT_DIMS,
                                   preferred_element_type=f32) for h in range(DN_HEADS)]
            kks = [lax.dot_general(k[:, h * dh:(h + 1) * dh], k[:, h * dh:(h + 1) * dh], NT_DIMS,
                                   preferred_element_type=f32) for h in range(DN_HEADS)]
            for d in range(2):
                incl, strict = masks[d]
                egs = []
                for h in range(DN_HEADS):
                    hs = slice(h * dh, (h + 1) * dh)
                    col = d * DN_HEADS + h
                    gl = 2 * DN_HEADS + col
                    kh, vh, qh = k[:, hs].astype(f32), v[:, hs].astype(f32), q[:, hs].astype(f32)
                    beta = gates[:, col:col + 1]
                    gcol = gcs[d][:, gl:gl + 1]
                    glast = gtot[:, gl:gl + 1]
                    decay = jnp.where(incl, jnp.exp(jnp.minimum(gcol - gcs_t[d][gl:gl + 1, :], 0.0)), 0.0)
                    eg = jnp.exp(gcol)
                    wq_ref[d, 0, j, h, c:2 * c, :] = (qh * eg).astype(bf16)
                    ak_ref[d, 0, j, h, 0:c, :] = jnp.where(incl, qks[h] * decay, 0.0).astype(bf16)
                    ak_ref[d, 0, j, h, c:c + dh, :] = jnp.transpose(kh * jnp.exp(glast - gcol)).astype(bf16)
                    egs.append(jnp.broadcast_to(jnp.exp(glast), (1, GATE_W)))
                    kb = kh * beta
                    units.append(dict(j=j, d=d, h=h, n=-jnp.where(strict, beta * kks[h] * decay, 0.0),
                                      rhs=jnp.concatenate([vh * beta, kb * eg], axis=1).astype(bf16)))
                eg_ref[d, 0, j] = jnp.concatenate(egs + [jnp.zeros((8 - DN_HEADS, GATE_W), f32)], axis=0)
        for un in units:
            un["t"] = eye + un["n"]
            un["p"] = _bdot(un["n"], un["n"])
        for _ in range(int(math.log2(c)) - 2):
            for un in units:
                both = _bdot(jnp.concatenate([un["p"], un["t"]], axis=0), un["p"])
                un["p"], un["t"] = both[:c], un["t"] + both[c:]
        for un in units:
            un["t"] = un["t"] + _bdot(un["t"], un["p"])
        for un in units:
            sol = _bdot(un["t"], un["rhs"])
            u_ref[un["d"], 0, un["j"], un["h"]] = sol[:, :dh].astype(bf16)
            wq_ref[un["d"], 0, un["j"], un["h"], 0:c, :] = sol[:, dh:].astype(bf16)
        return carry
    lax.fori_loop(0, DN_GROUP // DN_UNROLL, body, 0)


def dn_chunks(qc, kc, vc, gc, qx, kx, vx, gx):
    b, lc, w = qc.shape
    l = qx.shape[1]
    assert lc == DN_GROUP_ROWS and l % DN_GROUP_ROWS == 0
    ng = 1 + l // DN_GROUP_ROWS
    nch = ng * DN_GROUP
    c, dh, nhd = DN_CHUNK, DN_HEAD_DIM, DN_HEADS
    cmap = lambda bi, g: (bi, 0, 0)
    xmap = lambda bi, g: (bi, jnp.maximum(g - 1, 0), 0)
    blk = lambda width, m: pl.BlockSpec((1, DN_GROUP_ROWS, width), m)
    omap = lambda bi, g: (0, bi, g, 0, 0, 0)
    return pl.pallas_call(
        _dn_chunk_kernel,
        grid=(b, ng),
        in_specs=[blk(w, cmap), blk(w, cmap), blk(w, cmap), blk(GATE_W, cmap),
                  blk(w, xmap), blk(w, xmap), blk(w, xmap), blk(GATE_W, xmap)],
        out_specs=[pl.BlockSpec((2, 1, DN_GROUP, nhd, c, dh), omap),
                   pl.BlockSpec((2, 1, DN_GROUP, nhd, 2 * c, dh), omap),
                   pl.BlockSpec((2, 1, DN_GROUP, nhd, c + dh, c), omap),
                   pl.BlockSpec((2, 1, DN_GROUP, 8, GATE_W), lambda bi, g: (0, bi, g, 0, 0))],
        out_shape=[jax.ShapeDtypeStruct((2, b, nch, nhd, c, dh), bf16),
                   jax.ShapeDtypeStruct((2, b, nch, nhd, 2 * c, dh), bf16),
                   jax.ShapeDtypeStruct((2, b, nch, nhd, c + dh, c), bf16),
                   jax.ShapeDtypeStruct((2, b, nch, 8, GATE_W), f32)],
        compiler_params=pltpu.CompilerParams(dimension_semantics=("arbitrary", "arbitrary"),
                                             vmem_limit_bytes=VMEM_LIMIT),
        name="dn_chunks",
    )(qc, kc, vc, gc, qx, kx, vx, gx)


def _bwd_group(s, ng):
    return jnp.where(s == 0, 0, ng - s)


def _dn_serial_kernel(uf_ref, wqf_ref, akf_ref, egf_ref, ub_ref, wqb_ref, akb_ref, egb_ref, of_ref, ob_ref, s_ref):
    c = DN_CHUNK
    dh = DN_HEAD_DIM

    @pl.when(pl.program_id(1) == 0)
    def _():
        s_ref[...] = jnp.zeros(s_ref.shape, f32)

    dirs = ((uf_ref, wqf_ref, akf_ref, egf_ref, of_ref), (ub_ref, wqb_ref, akb_ref, egb_ref, ob_ref))
    units = [(e, d, h) for e in range(DN_BATCH_BLOCK) for d in range(2) for h in range(DN_HEADS)]
    states = {un: s_ref[un] for un in units}
    for jj in range(DN_GROUP):
        js = (jj, DN_GROUP - 1 - jj)
        r1 = {(e, d, h): jnp.dot(dirs[d][1][0, e, js[d], h], states[e, d, h].astype(bf16), preferred_element_type=f32)
              for e, d, h in units}
        r2 = {}
        for e, d, h in units:
            v_new = dirs[d][0][0, e, js[d], h].astype(f32) - r1[e, d, h][:c]
            r2[e, d, h] = jnp.dot(dirs[d][2][0, e, js[d], h], v_new.astype(bf16), preferred_element_type=f32)
        for e, d, h in units:
            j = js[d]
            dirs[d][4][e, j * c:(j + 1) * c, h * dh:(h + 1) * dh] = r1[e, d, h][c:] + r2[e, d, h][:c]
            states[e, d, h] = states[e, d, h] * dirs[d][3][0, e, j, h:h + 1, :] + r2[e, d, h][c:]
    for un in units:
        s_ref[un] = states[un]


def dn_serial(u, wq, ak, eg):
    _, b, nch, nhd, c, dh = u.shape
    ng = nch // DN_GROUP
    w = nhd * dh
    fmap6 = lambda bi, s: (0, bi, s, 0, 0, 0)
    bmap6 = lambda bi, s: (1, bi, _bwd_group(s, ng), 0, 0, 0)
    fmap5 = lambda bi, s: (0, bi, s, 0, 0)
    bmap5 = lambda bi, s: (1, bi, _bwd_group(s, ng), 0, 0)
    bb = DN_BATCH_BLOCK
    assert b % bb == 0
    blk6 = lambda a, m: pl.BlockSpec((1, bb, DN_GROUP) + a.shape[3:], m)
    egblk = lambda m: pl.BlockSpec((1, bb, DN_GROUP, 8, GATE_W), m)
    return pl.pallas_call(
        _dn_serial_kernel,
        grid=(b // bb, ng),
        in_specs=[blk6(u, fmap6), blk6(wq, fmap6), blk6(ak, fmap6), egblk(fmap5),
                  blk6(u, bmap6), blk6(wq, bmap6), blk6(ak, bmap6), egblk(bmap5)],
        out_specs=[pl.BlockSpec((bb, DN_GROUP_ROWS, w), lambda bi, s: (bi, s, 0)),
                   pl.BlockSpec((bb, DN_GROUP_ROWS, w), lambda bi, s: (bi, _bwd_group(s, ng), 0))],
        out_shape=[jax.ShapeDtypeStruct((b, nch * c, w), f32)] * 2,
        scratch_shapes=[pltpu.VMEM((bb, 2, nhd, dh, dh), f32)],
        compiler_params=pltpu.CompilerParams(dimension_semantics=("arbitrary", "arbitrary"),
                                             vmem_limit_bytes=VMEM_LIMIT),
        name="dn_serial",
    )(u, wq, ak, eg, u, wq, ak, eg)


def _dn_out_kernel(of_ref, ob_ref, z_ref, nw_ref, y_ref):
    o = of_ref[0] + ob_ref[0]
    z = z_ref[0].astype(f32)
    nw = nw_ref[...].astype(f32)
    outs = []
    for h in range(DN_HEADS):
        oh = o[:, h * DN_HEAD_DIM:(h + 1) * DN_HEAD_DIM]
        outs.append(oh * lax.rsqrt(jnp.mean(oh * oh, axis=-1, keepdims=True) + NORM_EPS) * nw)
    y_ref[0] = (jnp.concatenate(outs, axis=1) * (z * jax.nn.sigmoid(z))).astype(y_ref.dtype)


def dn_output(o_f, o_b, p, norm_w, first_block):
    b, l, _ = p.shape
    w = DN_WIDTH
    tm = DN_GROUP_ROWS
    omap = lambda bi, i: (bi, first_block + i, 0)
    return pl.pallas_call(
        _dn_out_kernel,
        grid=(b, l // tm),
        in_specs=[pl.BlockSpec((1, tm, w), omap), pl.BlockSpec((1, tm, w), omap),
                  pl.BlockSpec((1, tm, w), lambda bi, i: (bi, i, C_DN_Z // w)),
                  pl.BlockSpec((1, DN_HEAD_DIM), lambda bi, i: (0, 0))],
        out_specs=pl.BlockSpec((1, tm, w), lambda bi, i: (bi, i, 0)),
        out_shape=jax.ShapeDtypeStruct((b, l, w), bf16),
        compiler_params=pltpu.CompilerParams(dimension_semantics=("arbitrary", "arbitrary"),
                                             vmem_limit_bytes=VMEM_LIMIT),
        name="dn_output",
    )(o_f, o_b, p, norm_w.reshape(1, DN_HEAD_DIM))


def _pack_w_in(w_in):
    depth, d, _ = w_in.shape
    main = jnp.concatenate([w_in[:, :, :3072], w_in[:, :, 3088:4112]], axis=-1).astype(bf16)
    gate = jnp.concatenate([w_in[:, :, 3072:3088], jnp.zeros((depth, d, GATE_W - 16), w_in.dtype)],
                           axis=-1).astype(bf16)
    return main, gate


def kernel(x, c, ctx, c_ctx, w_ada, b_ada, g_pre, g_post, w_in, w_out, na_rpb, dn_conv, dn_a_log, dn_dt_bias,
           dn_norm, hy_conv, hy_w1, hy_b1, hy_w2, hy_b2, hy_w3, hy_b3, hy_freq, hy_wout, hy_skip):
    bsz, l, d = x.shape
    lc = ctx.shape[1]
    depth = w_in.shape[0]
    cond = jnp.concatenate([c, c_ctx[None], jnp.zeros((7, d), f32)], axis=0)
    mod = modulation_all(cond, w_ada, b_ada)
    w_main, w_gate = _pack_w_in(w_in)
    w_out_b = w_out.astype(bf16)

    for i in range(depth):
        last = i == depth - 1
        shift_x, scale_x, gate_x = [mod[i, :bsz, j * d:(j + 1) * d][:, None] for j in range(3)]
        shift_c, scale_c, gate_c = [mod[i, bsz:bsz + 1, j * d:(j + 1) * d][:, None] for j in range(3)]
        px, gx = in_projection(x, g_pre[i], scale_x, shift_x, w_main[i], w_gate[i], 512)
        pc, gc = in_projection(ctx, g_pre[i], scale_c, shift_c, w_main[i], w_gate[i], 256)

        hy_args = (hy_conv[i], hy_skip[i], hy_w1[i], hy_b1[i], hy_w2[i], hy_b2[i], hy_w3[i], hy_b3[i], hy_freq[i],
                   hy_wout[i])
        dn_args = (dn_conv[i], dn_a_log[i], dn_dt_bias[i])

        out_a_x = na_attention(px, pc, na_rpb[i])

        dn_c = dn_prep(pc, gc, *dn_args, False)
        dn_x = dn_prep(px, gx, *dn_args, True)
        do_f, do_b = dn_serial(*dn_chunks(*dn_c, *dn_x))
        out_b_x = dn_output(do_f, do_b, px, dn_norm[i], lc // DN_GROUP_ROWS)

        out_c_x = hyena_latent(px, *hy_args)

        new_x = out_projection(out_a_x, out_b_x, out_c_x, w_out_b[i], g_post[i], gate_x, x, 512)

        if not last:
            out_a_c = ctx_attention(pc)
            out_b_c = dn_output(do_f, do_b, pc, dn_norm[i], 0)
            out_c_c = hyena_ctx(pc, *hy_args)
            ctx = out_projection(out_a_c, out_b_c, out_c_c, w_out_b[i], g_post[i], gate_c, ctx, lc)
        x = new_x
    return x
```

```python
import functools
import math

import numpy as np
import jax
import jax.numpy as jnp
from jax import lax
from jax.experimental import pallas as pl
from jax.experimental.pallas import tpu as pltpu

D_MODEL = 1024
GRID_W = 64
NA_HEAD_DIM = 64
NA_WIDTH = 256
NA_HEADS = 4
NA_WIN_ROWS = 8
NA_WIN_COLS = 16
DN_HEAD_DIM = 128
DN_WIDTH = 512
DN_HEADS = 4
DN_CONV_W = 5
DN_CHUNK = 64
HY_WIDTH = 256
HY_ORDER = 2
HY_EMB = 33
HY_DECAY_TARGET = 1e-2
HY_FAST_DECAY = 0.3
HY_SLOW_DECAY = 1.5
ROPE_BASE = 10000.0
NORM_EPS = 1e-6

C_NA_QKV, C_NA_Z, C_DN_QKV, C_DN_Z, C_HY_VX, C_HY_Z = 0, 768, 1024, 2560, 3072, 3840
MAIN_W = 4096
GATE_W = 128
LANE = 128
VMEM_LIMIT = 52 * 1024 * 1024

f32 = jnp.float32
bf16 = jnp.bfloat16
HIGHEST = lax.Precision.HIGHEST
NT_DIMS = (((1,), (1,)), ((), ()))
TN_DIMS = (((0,), (0,)), ((), ()))


def _mod_kernel(c_ref, w_ref, b_ref, o_ref):
    c = c_ref[...]
    a = c * jax.nn.sigmoid(c)
    o_ref[0] = jnp.dot(a, w_ref[0], preferred_element_type=f32, precision=lax.Precision.HIGHEST) + b_ref[0]


def modulation_all(cond, w_ada, b_ada):
    depth, d, d3 = w_ada.shape
    r = cond.shape[0]
    tn = 512
    return pl.pallas_call(
        _mod_kernel,
        grid=(depth, d3 // tn),
        in_specs=[pl.BlockSpec((r, d), lambda i, j: (0, 0)),
                  pl.BlockSpec((1, d, tn), lambda i, j: (i, 0, j)),
                  pl.BlockSpec((1, 1, tn), lambda i, j: (i, 0, j))],
        out_specs=pl.BlockSpec((1, r, tn), lambda i, j: (i, 0, j)),
        out_shape=jax.ShapeDtypeStruct((depth, r, d3), f32),
        compiler_params=pltpu.CompilerParams(dimension_semantics=("arbitrary", "arbitrary"),
                                             vmem_limit_bytes=VMEM_LIMIT),
        name="modulation",
    )(cond, w_ada, b_ada.reshape(depth, 1, d3))


def _inproj_kernel(x_ref, g_ref, sc_ref, sh_ref, w_ref, wg_ref, o_ref, og_ref):
    x = x_ref[0]
    ms = jnp.mean(x * x, axis=-1, keepdims=True)
    h = x * lax.rsqrt(ms + NORM_EPS) * g_ref[...]
    h = (h * (1.0 + sc_ref[0]) + sh_ref[0]).astype(bf16)
    for j in range(MAIN_W // 1024):
        o_ref[0, :, j * 1024:(j + 1) * 1024] = jnp.dot(
            h, w_ref[:, j * 1024:(j + 1) * 1024], preferred_element_type=f32).astype(bf16)
    og_ref[0] = jnp.dot(h, wg_ref[...], preferred_element_type=f32)


def in_projection(x, g, scale, shift, w_main, w_gate, tm):
    b, l, d = x.shape
    per_b = scale.shape[0] == b
    mod_map = (lambda bi, i: (bi, 0, 0)) if per_b else (lambda bi, i: (0, 0, 0))
    return pl.pallas_call(
        _inproj_kernel,
        grid=(b, l // tm),
        in_specs=[pl.BlockSpec((1, tm, d), lambda bi, i: (bi, i, 0)),
                  pl.BlockSpec((1, d), lambda bi, i: (0, 0)),
                  pl.BlockSpec((1, 1, d), mod_map),
                  pl.BlockSpec((1, 1, d), mod_map),
                  pl.BlockSpec((d, MAIN_W), lambda bi, i: (0, 0)),
                  pl.BlockSpec((d, GATE_W), lambda bi, i: (0, 0))],
        out_specs=[pl.BlockSpec((1, tm, MAIN_W), lambda bi, i: (bi, i, 0)),
                   pl.BlockSpec((1, tm, GATE_W), lambda bi, i: (bi, i, 0))],
        out_shape=[jax.ShapeDtypeStruct((b, l, MAIN_W), bf16),
                   jax.ShapeDtypeStruct((b, l, GATE_W), f32)],
        compiler_params=pltpu.CompilerParams(dimension_semantics=("arbitrary", "arbitrary"),
                                             vmem_limit_bytes=VMEM_LIMIT),
        name="in_projection",
    )(x, g.reshape(1, d), scale, shift, w_main, w_gate)


def _outproj_kernel(a_ref, b_ref, c_ref, w_ref, g_ref, gate_ref, x_ref, o_ref):
    y = (jnp.dot(a_ref[0], w_ref[0:NA_WIDTH], preferred_element_type=f32)
         + jnp.dot(b_ref[0], w_ref[NA_WIDTH:NA_WIDTH + DN_WIDTH], preferred_element_type=f32)
         + jnp.dot(c_ref[0], w_ref[NA_WIDTH + DN_WIDTH:], preferred_element_type=f32))
    ms = jnp.mean(y * y, axis=-1, keepdims=True)
    yn = y * lax.rsqrt(ms + NORM_EPS) * g_ref[...]
    o_ref[0] = x_ref[0] + gate_ref[0] * yn


def out_projection(out_a, out_b, out_c, w_out, g_post, gate, x, tm):
    b, l, d = x.shape
    per_b = gate.shape[0] == b
    mod_map = (lambda bi, i: (bi, 0, 0)) if per_b else (lambda bi, i: (0, 0, 0))
    part = lambda a: pl.BlockSpec((1, tm, a.shape[-1]), lambda bi, i: (bi, i, 0))
    return pl.pallas_call(
        _outproj_kernel,
        grid=(b, l // tm),
        in_specs=[part(out_a), part(out_b), part(out_c),
                  pl.BlockSpec((d, d), lambda bi, i: (0, 0)),
                  pl.BlockSpec((1, d), lambda bi, i: (0, 0)),
                  pl.BlockSpec((1, 1, d), mod_map),
                  pl.BlockSpec((1, tm, d), lambda bi, i: (bi, i, 0))],
        out_specs=pl.BlockSpec((1, tm, d), lambda bi, i: (bi, i, 0)),
        out_shape=jax.ShapeDtypeStruct((b, l, d), f32),
        compiler_params=pltpu.CompilerParams(dimension_semantics=("arbitrary", "arbitrary"),
                                             vmem_limit_bytes=VMEM_LIMIT),
        name="out_projection",
    )(out_a, out_b, out_c, w_out, g_post.reshape(1, d), gate, x)


NA_TQ_ROWS = 4
NA_TQ = NA_TQ_ROWS * GRID_W
NA_KV_ROWS = NA_TQ_ROWS + NA_WIN_ROWS
NA_KV = NA_KV_ROWS * GRID_W
MASK_VALUE = -1e30


def _na_window_start(t, rows):
    return jnp.clip(t * NA_TQ_ROWS - NA_WIN_ROWS // 2, 0, rows - NA_KV_ROWS)


def _na_bias_kernel(rpb_ref, sel_ref, mask_ref, o_ref):
    o_ref[...] = jnp.dot(rpb_ref[...], sel_ref[...], preferred_element_type=f32, precision=HIGHEST) + mask_ref[...]


def na_bias_tables(rpb, rows):
    nh, ndr, ndc = rpb.shape
    col = np.arange(GRID_W)
    dc = np.clip(col[None, :] - col[:, None], -(NA_WIN_COLS - 1), NA_WIN_COLS - 1) + (NA_WIN_COLS - 1)
    c0 = np.clip(col - NA_WIN_COLS // 2, 0, GRID_W - NA_WIN_COLS)
    in_win = (col[None, :] >= c0[:, None]) & (col[None, :] < c0[:, None] + NA_WIN_COLS)
    sel = np.zeros((LANE, GRID_W * GRID_W), np.float32)
    sel[dc.reshape(-1), np.arange(GRID_W * GRID_W)] = 1.0
    cmask = np.where(in_win, 0.0, MASK_VALUE).astype(np.float32).reshape(1, -1)
    rpb2 = jnp.pad(rpb.astype(f32).reshape(nh * ndr, ndc), ((0, 0), (0, LANE - ndc)))
    tab = pl.pallas_call(
        _na_bias_kernel,
        out_shape=jax.ShapeDtypeStruct((nh * ndr, GRID_W * GRID_W), f32),
        name="na_bias",
    )(rpb2, jnp.asarray(sel), jnp.asarray(cmask)).reshape(nh, ndr, GRID_W, GRID_W)
    masked = jnp.full((nh, GRID_W, GRID_W), MASK_VALUE, f32)
    n_tiles = rows // NA_TQ_ROWS
    tabs = []
    for t in (0, 1, n_tiles - 1):
        r0 = t * NA_TQ_ROWS
        ws = int(np.clip(r0 - NA_WIN_ROWS // 2, 0, rows - NA_KV_ROWS))
        blocks = []
        for qr in range(r0, r0 + NA_TQ_ROWS):
            band0 = int(np.clip(qr - NA_WIN_ROWS // 2, 0, rows - NA_WIN_ROWS))
            blocks.append(jnp.concatenate(
                [tab[:, kr - qr + NA_WIN_ROWS - 1] if band0 <= kr < band0 + NA_WIN_ROWS else masked
                 for kr in range(ws, ws + NA_KV_ROWS)], axis=-1))
        tabs.append(jnp.concatenate(blocks, axis=-2))
    return jnp.stack(tabs)


def _na_kernel(q_ref, k_ref, v_ref, z_ref, kc_ref, vc_ref, bias_ref, o_ref, *, rows):
    t = pl.program_id(1)
    ws = pl.multiple_of(_na_window_start(t, rows) * GRID_W, GRID_W)
    q = q_ref[0]
    kw = k_ref[0, pl.ds(ws, NA_KV), :]
    vw = v_ref[0, pl.ds(ws, NA_KV), :]
    kc = kc_ref[0]
    vc = vc_ref[0]
    hd = NA_HEAD_DIM
    low_w = lax.broadcasted_iota(jnp.int32, (NA_KV, LANE), 1) < hd
    low_c = lax.broadcasted_iota(jnp.int32, (kc.shape[0], LANE), 1) < hd
    low_q = lax.broadcasted_iota(jnp.int32, (NA_TQ, LANE), 1) < hd
    tiles = []
    for hp in range(NA_HEADS * hd // LANE):
        tile_w = vw[:, hp * LANE:(hp + 1) * LANE]
        tile_c = vc[:, hp * LANE:(hp + 1) * LANE]
        normed = []
        for par in range(LANE // hd):
            h = hp * (LANE // hd) + par
            sl = slice(h * hd, (h + 1) * hd)
            qh = q[:, sl] * (hd ** -0.5)
            s1 = lax.dot_general(qh, kw[:, sl], NT_DIMS, preferred_element_type=f32) + bias_ref[0, h]
            s2 = lax.dot_general(qh, kc[:, sl], NT_DIMS, preferred_element_type=f32)
            m = jnp.maximum(jnp.max(s1, axis=-1, keepdims=True), jnp.max(s2, axis=-1, keepdims=True))
            p1 = jnp.exp((s1 - m).astype(bf16))
            p2 = jnp.exp((s2 - m).astype(bf16))
            keep_w, keep_c = (low_w, low_c) if par == 0 else (~low_w, ~low_c)
            oa = (jnp.dot(p1, jnp.where(keep_w, tile_w, 1.0).astype(bf16), preferred_element_type=f32)
                  + jnp.dot(p2, jnp.where(keep_c, tile_c, 1.0).astype(bf16), preferred_element_type=f32))
            den = oa[:, hd:hd + 1] if par == 0 else oa[:, 0:1]
            normed.append(oa / den)
        tiles.append(jnp.where(low_q, normed[0], normed[1]))
    z = z_ref[0].astype(f32)
    o = jnp.concatenate(tiles, axis=-1) * (z * jax.nn.sigmoid(z))
    o_ref[0] = o.astype(o_ref.dtype)


def na_attention(px, pc, rpb):
    b, l, _ = px.shape
    lc = pc.shape[1]
    rows = l // GRID_W
    n_tiles = rows // NA_TQ_ROWS
    bias = na_bias_tables(rpb, rows)
    w = NA_WIDTH
    cq, ck, cv, cz = [(C_NA_QKV + j * w) // w for j in range(3)] + [C_NA_Z // w]

    def bias_map(bi, t):
        return (jnp.where(t == 0, 0, jnp.where(t == n_tiles - 1, 2, 1)), 0, 0, 0)

    return pl.pallas_call(
        functools.partial(_na_kernel, rows=rows),
        grid=(b, n_tiles),
        in_specs=[pl.BlockSpec((1, NA_TQ, w), lambda bi, t: (bi, t, cq)),
                  pl.BlockSpec((1, l, w), lambda bi, t: (bi, 0, ck)),
                  pl.BlockSpec((1, l, w), lambda bi, t: (bi, 0, cv)),
                  pl.BlockSpec((1, NA_TQ, w), lambda bi, t: (bi, t, cz)),
                  pl.BlockSpec((1, lc, w), lambda bi, t: (bi, 0, ck)),
                  pl.BlockSpec((1, lc, w), lambda bi, t: (bi, 0, cv)),
                  pl.BlockSpec((1, NA_HEADS, NA_TQ, NA_KV), bias_map)],
        out_specs=pl.BlockSpec((1, NA_TQ, w), lambda bi, t: (bi, t, 0)),
        out_shape=jax.ShapeDtypeStruct((b, l, w), bf16),
        compiler_params=pltpu.CompilerParams(dimension_semantics=("arbitrary", "arbitrary"),
                                             vmem_limit_bytes=VMEM_LIMIT),
        name="na_attention",
    )(px, px, px, px, pc, pc, bias)


def _ctx_attn_kernel(q_ref, k_ref, v_ref, z_ref, o_ref):
    q = q_ref[0]
    k = k_ref[0]
    v = v_ref[0]
    outs = []
    for h in range(NA_HEADS):
        sl = slice(h * NA_HEAD_DIM, (h + 1) * NA_HEAD_DIM)
        qh = q[:, sl] * (NA_HEAD_DIM ** -0.5)
        s = lax.dot_general(qh, k[:, sl], NT_DIMS, preferred_element_type=f32)
        p = jnp.exp(s - jnp.max(s, axis=-1, keepdims=True))
        den = jnp.sum(p, axis=-1, keepdims=True)
        outs.append(jnp.dot(p.astype(bf16), v[:, sl], preferred_element_type=f32) / den)
    z = z_ref[0].astype(f32)
    o_ref[0] = (jnp.concatenate(outs, axis=-1) * (z * jax.nn.sigmoid(z))).astype(o_ref.dtype)


def ctx_attention(pc):
    b, lc, _ = pc.shape
    w = NA_WIDTH
    cq, ck, cv, cz = [(C_NA_QKV + j * w) // w for j in range(3)] + [C_NA_Z // w]
    spec = lambda cidx: pl.BlockSpec((1, lc, w), lambda bi: (bi, 0, cidx))
    return pl.pallas_call(
        _ctx_attn_kernel,
        grid=(b,),
        in_specs=[spec(cq), spec(ck), spec(cv), spec(cz)],
        out_specs=pl.BlockSpec((1, lc, w), lambda bi: (bi, 0, 0)),
        out_shape=jax.ShapeDtypeStruct((b, lc, w), bf16),
        compiler_params=pltpu.CompilerParams(dimension_semantics=("arbitrary",), vmem_limit_bytes=VMEM_LIMIT),
        name="ctx_attention",
    )(pc, pc, pc, pc)


HY_CH = 128
FFT_N2 = 128
ROW_CHUNK = 256
STRIDE_PAD = 8
HY_FILT_ROWS = 512
HY_BANDS = (HY_EMB - 1) // 2


def _hy_filter_kernel(fr_ref, w1t_ref, w1c_ref, w1s_ref, b1_ref, w2_ref, b2_ref, w3_ref, b3_ref, freq_ref, wo_ref,
                      dl_ref, o_ref, *, l):
    i = pl.program_id(0)
    n = i * HY_FILT_ROWS + lax.broadcasted_iota(jnp.int32, (HY_FILT_ROWS, 1), 0)
    lag = jnp.where(n < l, n, 2 * l - n).astype(f32)
    t = lag * (1.0 / (l - 1))
    ang = (lag * (2.0 * math.pi / l)) * fr_ref[...]
    dot = functools.partial(jnp.dot, preferred_element_type=f32, precision=HIGHEST)
    pre = t * w1t_ref[...] + dot(jnp.cos(ang), w1c_ref[...]) - dot(jnp.sin(ang), w1s_ref[...]) + b1_ref[...]
    hid = jnp.sin(freq_ref[0:1] * pre)
    hid = jnp.sin(freq_ref[1:2] * (dot(hid, w2_ref[...]) + b2_ref[...]))
    hid = jnp.sin(freq_ref[2:3] * (dot(hid, w3_ref[...]) + b3_ref[...]))
    filt = dot(hid, wo_ref[...])
    window = jnp.exp(-t * dl_ref[...])
    live = jnp.where(n == l, 0.0, 1.0)
    fwd = n < l
    for o in range(HY_ORDER):
        a = filt[:, (2 * o) * HY_WIDTH:(2 * o + 1) * HY_WIDTH]
        b = filt[:, (2 * o + 1) * HY_WIDTH:(2 * o + 2) * HY_WIDTH]
        o_ref[:, o * HY_WIDTH:(o + 1) * HY_WIDTH] = jnp.where(fwd, a, b) * window * live


def hyena_filter_circular(l, w1, b1, w2, b2, w3, b3, freq, wout):
    n = 2 * l
    assert n % HY_FILT_ROWS == 0
    hid = w1.shape[1]
    fr = np.zeros((1, LANE), np.float32)
    fr[0, :HY_BANDS] = np.linspace(1e-4, HY_BANDS - 1, HY_BANDS, dtype=np.float32)
    padrows = lambda w: jnp.concatenate([w, jnp.zeros((LANE - w.shape[0], hid), f32)], axis=0)
    w1t, w1c, w1s = w1[0:1], padrows(w1[1:1 + HY_BANDS]), padrows(w1[1 + HY_BANDS:])
    max_decay = math.log(HY_DECAY_TARGET) / HY_FAST_DECAY
    min_decay = math.log(HY_DECAY_TARGET) / HY_SLOW_DECAY
    deltas = np.abs(np.linspace(min_decay, max_decay, HY_WIDTH, dtype=np.float32))[None]
    full = lambda a: pl.BlockSpec(a.shape, lambda i: (0,) * a.ndim)
    args = [jnp.asarray(fr), w1t, w1c, w1s, b1[None], w2, b2[None], w3, b3[None], freq, wout, jnp.asarray(deltas)]
    return pl.pallas_call(
        functools.partial(_hy_filter_kernel, l=l),
        grid=(n // HY_FILT_ROWS,),
        in_specs=[full(a) for a in args],
        out_specs=pl.BlockSpec((HY_FILT_ROWS, HY_ORDER * HY_WIDTH), lambda i: (i, 0)),
        out_shape=jax.ShapeDtypeStruct((n, HY_ORDER * HY_WIDTH), f32),
        compiler_params=pltpu.CompilerParams(dimension_semantics=("arbitrary",), vmem_limit_bytes=VMEM_LIMIT),
        name="hyena_filter",
    )(*args)


@functools.lru_cache(maxsize=None)
def _fft_consts(n, k_in, k_out):
    n1 = n // FFT_N2
    k1 = np.arange(n1)
    n2 = np.arange(FFT_N2)
    tt = FFT_N2 * np.arange(k_in)[None, None, :] + n2[:, None, None]
    ang = -2.0 * np.pi * (k1[None, :, None] * tt) / n
    f1 = np.concatenate([np.cos(ang), np.sin(ang)], axis=1)
    a2 = -2.0 * np.pi * np.outer(n2, n2) / FFT_N2
    cr, ci = np.cos(a2), np.sin(a2)
    fblk = np.block([[cr, -ci], [ci, cr]])
    fiblk = np.block([[cr, ci], [-ci, cr]])
    to = FFT_N2 * np.arange(k_out)[None, :, None] + n2[:, None, None]
    ango = 2.0 * np.pi * (k1[None, None, :] * to) / n
    hinv = np.concatenate([np.cos(ango), -np.sin(ango)], axis=2) / n
    return f1, fblk, fiblk, hinv


FFT_OUTER_UNROLL = 8
FFT_INNER_UNROLL = 2
FFT_INNER_GROUPS = 8


def _fft_fwd_stage1(x_ref, f1_ref, ar_ref, ai_ref, k_in, n1, sa, dt, prec):
    def body(it, c):
        n2s = [it * FFT_OUTER_UNROLL + u for u in range(FFT_OUTER_UNROLL)]
        xs = [x_ref[pl.ds(n2, k_in, stride=FFT_N2), :].astype(dt) for n2 in n2s]
        res = [jnp.dot(f1_ref[n2], x, preferred_element_type=f32, precision=prec) for n2, x in zip(n2s, xs)]
        for n2, a in zip(n2s, res):
            off = pl.multiple_of(n2 * sa, 8)
            ar_ref[pl.ds(off, n1), :] = a[:n1]
            ai_ref[pl.ds(off, n1), :] = a[n1:]
        return c
    lax.fori_loop(0, FFT_N2 // FFT_OUTER_UNROLL, body, 0)


def _fft_inner_load(ar_ref, ai_ref, k1s, sa, dt):
    zr = jnp.concatenate([ar_ref[pl.ds(k1, FFT_N2, stride=sa), :] for k1 in k1s], axis=1)
    zi = jnp.concatenate([ai_ref[pl.ds(k1, FFT_N2, stride=sa), :] for k1 in k1s], axis=1)
    return jnp.concatenate([zr, zi], axis=0).astype(dt)


def _filter_fft_kernel(k_ref, f1_ref, fblk_ref, hr_ref, hi_ref, ar_ref, ai_ref, *, n):
    n1 = n // FFT_N2
    sa = n1 + STRIDE_PAD
    ch = k_ref.shape[-1]
    _fft_fwd_stage1(k_ref, f1_ref, ar_ref, ai_ref, n1, n1, sa, f32, HIGHEST)

    def body(it, c):
        k1s = [it * FFT_INNER_UNROLL + u for u in range(FFT_INNER_UNROLL)]
        xx = jnp.dot(fblk_ref[...], _fft_inner_load(ar_ref, ai_ref, k1s, sa, f32), preferred_element_type=f32,
                     precision=HIGHEST)
        for u, k1 in enumerate(k1s):
            off = pl.multiple_of(k1 * FFT_N2, FFT_N2)
            hr_ref[pl.ds(off, FFT_N2), :] = xx[:FFT_N2, u * ch:(u + 1) * ch]
            hi_ref[pl.ds(off, FFT_N2), :] = xx[FFT_N2:, u * ch:(u + 1) * ch]
        return c
    lax.fori_loop(0, n1 // FFT_INNER_UNROLL, body, 0)


def filter_spectrum(kern):
    n, c = kern.shape
    n1 = n // FFT_N2
    f1, fblk, _, _ = _fft_consts(n, n1, 1)
    f1 = jnp.asarray(f1, f32)
    fblk = jnp.asarray(fblk, f32)
    sa = n1 + STRIDE_PAD
    return pl.pallas_call(
        functools.partial(_filter_fft_kernel, n=n),
        grid=(c // HY_CH,),
        in_specs=[pl.BlockSpec((n, HY_CH), lambda j: (0, j)),
                  pl.BlockSpec(f1.shape, lambda j: (0, 0, 0)),
                  pl.BlockSpec(fblk.shape, lambda j: (0, 0))],
        out_specs=[pl.BlockSpec((n, HY_CH), lambda j: (0, j))] * 2,
        out_shape=[jax.ShapeDtypeStruct((n, c), f32)] * 2,
        scratch_shapes=[pltpu.VMEM((FFT_N2 * sa, HY_CH), f32)] * 2,
        compiler_params=pltpu.CompilerParams(dimension_semantics=("arbitrary",), vmem_limit_bytes=VMEM_LIMIT),
        name="filter_spectrum",
    )(kern, f1, fblk)


def _conv3_rows(src_ref, w_ref, pad_ref, dst_ref, l):
    zeros = jnp.zeros((8, src_ref.shape[-1]), f32)
    pad_ref[0:8, :] = zeros
    pad_ref[l + 8:l + 16, :] = zeros

    def cp(i, c):
        r = pl.multiple_of(i * ROW_CHUNK, ROW_CHUNK)
        pad_ref[pl.ds(8 + r, ROW_CHUNK), :] = src_ref[0, pl.ds(r, ROW_CHUNK), :].astype(f32)
        return c
    lax.fori_loop(0, l // ROW_CHUNK, cp, 0)
    w = w_ref[...].astype(f32)

    def cv(i, c):
        r = pl.multiple_of(i * ROW_CHUNK, ROW_CHUNK)
        blk = pad_ref[pl.ds(r, ROW_CHUNK + 16), :]
        dst_ref[pl.ds(r, ROW_CHUNK), :] = (w[0:1] * blk[7:7 + ROW_CHUNK] + w[1:2] * blk[8:8 + ROW_CHUNK]
                                           + w[2:3] * blk[9:9 + ROW_CHUNK])
        return c
    lax.fori_loop(0, l // ROW_CHUNK, cv, 0)


def _hy_order_kernel(yin_ref, graw_ref, z_ref, cwy_ref, cwg_ref, skip_ref, hr_ref, hi_ref, f1_ref, fblk_ref, fiblk_ref,
                     hinv_ref, o_ref, pad_ref, y_ref, gate_ref, c_ref, ar_ref, ai_ref, br_ref, bi_ref,
                     *, l, first, last):
    n = 2 * l
    n1 = n // FFT_N2
    k1n = l // FFT_N2
    sa = n1 + STRIDE_PAD
    sb = FFT_N2 + STRIDE_PAD
    if first:
        _conv3_rows(yin_ref, cwy_ref, pad_ref, y_ref, l)
        src_ref = y_ref
    else:
        src_ref = yin_ref.at[0]
    _conv3_rows(graw_ref, cwg_ref, pad_ref, gate_ref, l)

    _fft_fwd_stage1(src_ref, f1_ref, ar_ref, ai_ref, k1n, n1, sa, bf16, None)

    ch = c_ref.shape[-1]

    def mid(it, c):
        groups = [[(it * FFT_INNER_GROUPS + gi) * FFT_INNER_UNROLL + u for u in range(FFT_INNER_UNROLL)]
                  for gi in range(FFT_INNER_GROUPS)]
        zs = [_fft_inner_load(ar_ref, ai_ref, k1s, sa, bf16) for k1s in groups]
        xxs = [jnp.dot(fblk_ref[...], z, preferred_element_type=f32) for z in zs]
        yys = []
        for k1s, xx in zip(groups, xxs):
            xr, xi = xx[:FFT_N2], xx[FFT_N2:]
            hr = jnp.concatenate([hr_ref[pl.ds(pl.multiple_of(k1 * FFT_N2, FFT_N2), FFT_N2), :] for k1 in k1s], axis=1)
            hi = jnp.concatenate([hi_ref[pl.ds(pl.multiple_of(k1 * FFT_N2, FFT_N2), FFT_N2), :] for k1 in k1s], axis=1)
            yys.append(jnp.concatenate([xr * hr - xi * hi, xr * hi + xi * hr], axis=0).astype(bf16))
        bbs = [jnp.dot(fiblk_ref[...], yy, preferred_element_type=f32) for yy in yys]
        for k1s, bb in zip(groups, bbs):
            for u, k1 in enumerate(k1s):
                boff = pl.multiple_of(k1 * sb, 8)
                br_ref[pl.ds(boff, FFT_N2), :] = bb[:FFT_N2, u * ch:(u + 1) * ch]
                bi_ref[pl.ds(boff, FFT_N2), :] = bb[FFT_N2:, u * ch:(u + 1) * ch]
        return c
    lax.fori_loop(0, n1 // (FFT_INNER_UNROLL * FFT_INNER_GROUPS), mid, 0)

    def inv2(it, c):
        n2s = [it * FFT_OUTER_UNROLL + u for u in range(FFT_OUTER_UNROLL)]
        bbs = [jnp.concatenate([br_ref[pl.ds(n2, n1, stride=sb), :], bi_ref[pl.ds(n2, n1, stride=sb), :]],
                               axis=0).astype(bf16) for n2 in n2s]
        res = [jnp.dot(hinv_ref[n2], bb, preferred_element_type=f32) for n2, bb in zip(n2s, bbs)]
        for n2, o in zip(n2s, res):
            c_ref[pl.ds(n2, k1n, stride=FFT_N2), :] = o
        return c
    lax.fori_loop(0, FFT_N2 // FFT_OUTER_UNROLL, inv2, 0)

    skip = skip_ref[0].astype(f32)

    def fin(i, c):
        r = pl.multiple_of(i * ROW_CHUNK, ROW_CHUNK)
        y = src_ref[pl.ds(r, ROW_CHUNK), :]
        y = gate_ref[pl.ds(r, ROW_CHUNK), :] * (c_ref[pl.ds(r, ROW_CHUNK), :] + y * skip)
        if last:
            z = z_ref[0, pl.ds(r, ROW_CHUNK), :].astype(f32)
            y = y * (z * jax.nn.sigmoid(z))
        o_ref[0, pl.ds(r, ROW_CHUNK), :] = y.astype(o_ref.dtype)
        return c
    lax.fori_loop(0, l // ROW_CHUNK, fin, 0)


def hyena_order(yin, px, hr, hi, conv_w, skip, order, first, last):
    b, l, _ = px.shape
    n = 2 * l
    n1 = n // FFT_N2
    k1n = l // FFT_N2
    sa, sb = n1 + STRIDE_PAD, FFT_N2 + STRIDE_PAD
    nh = HY_WIDTH // HY_CH
    f1, fblk, fiblk, hinv = [jnp.asarray(a, bf16) for a in _fft_consts(n, k1n, k1n)]
    cb = lambda col: col // HY_CH
    c_v, c_g, c_z = cb(C_HY_VX), cb(C_HY_VX + (order + 1) * HY_WIDTH), cb(C_HY_Z)
    once = pl.Buffered(1)
    yin_spec = (pl.BlockSpec((1, l, HY_CH), lambda h, bi: (bi, 0, c_v + h)) if first
                else pl.BlockSpec((1, l, HY_CH), lambda h, bi: (bi, 0, h)))
    return pl.pallas_call(
        functools.partial(_hy_order_kernel, l=l, first=first, last=last),
        grid=(nh, b),
        in_specs=[yin_spec,
                  pl.BlockSpec((1, l, HY_CH), lambda h, bi: (bi, 0, c_g + h)),
                  pl.BlockSpec((1, l, HY_CH), lambda h, bi: (bi, 0, c_z + h)),
                  pl.BlockSpec((3, HY_CH), lambda h, bi: (0, h)),
                  pl.BlockSpec((3, HY_CH), lambda h, bi: (0, (order + 1) * nh + h)),
                  pl.BlockSpec((1, 1, HY_CH), lambda h, bi: (order, 0, h)),
                  pl.BlockSpec((n, HY_CH), lambda h, bi: (0, order * nh + h), pipeline_mode=once),
                  pl.BlockSpec((n, HY_CH), lambda h, bi: (0, order * nh + h), pipeline_mode=once),
                  pl.BlockSpec(f1.shape, lambda h, bi: (0, 0, 0), pipeline_mode=once),
                  pl.BlockSpec(fblk.shape, lambda h, bi: (0, 0), pipeline_mode=once),
                  pl.BlockSpec(fiblk.shape, lambda h, bi: (0, 0), pipeline_mode=once),
                  pl.BlockSpec(hinv.shape, lambda h, bi: (0, 0, 0), pipeline_mode=once)],
        out_specs=pl.BlockSpec((1, l, HY_CH), lambda h, bi: (bi, 0, h)),
        out_shape=jax.ShapeDtypeStruct((b, l, HY_WIDTH), bf16 if last else f32),
        scratch_shapes=[pltpu.VMEM((l + 16, HY_CH), f32), pltpu.VMEM((l, HY_CH), f32), pltpu.VMEM((l, HY_CH), f32),
                        pltpu.VMEM((l, HY_CH), f32),
                        pltpu.VMEM((FFT_N2 * sa, HY_CH), f32), pltpu.VMEM((FFT_N2 * sa, HY_CH), f32),
                        pltpu.VMEM((n1 * sb, HY_CH), f32), pltpu.VMEM((n1 * sb, HY_CH), f32)],
        compiler_params=pltpu.CompilerParams(dimension_semantics=("arbitrary", "arbitrary"),
                                             vmem_limit_bytes=56 * 1024 * 1024),
        name=f"hyena_order{order}",
    )(yin, px, px, conv_w, conv_w, skip.reshape(HY_ORDER, 1, HY_WIDTH), hr, hi, f1, fblk, fiblk, hinv)


def hyena_latent(px, conv_w, skip, w1, b1, w2, b2, w3, b3, freq, wout):
    l = px.shape[1]
    kern = hyena_filter_circular(l, w1, b1, w2, b2, w3, b3, freq, wout)
    hr, hi = filter_spectrum(kern)
    y1 = hyena_order(px, px, hr, hi, conv_w, skip, 0, True, False)
    return hyena_order(y1, px, hr, hi, conv_w, skip, 1, False, True)


@functools.lru_cache(maxsize=None)
def _dense_dft_consts(l):
    n = 2 * l
    k = np.arange(n)
    ang = -2.0 * np.pi * np.outer(k, np.arange(n)) / n
    fwd = np.concatenate([np.cos(ang), np.sin(ang)], axis=0)
    angi = 2.0 * np.pi * np.outer(np.arange(l), k) / n
    inv = np.concatenate([np.cos(angi), -np.sin(angi)], axis=1) / n
    return fwd, inv


def _hy_ctx_kernel(p_ref, kern_ref, cw_ref, skip_ref, fwd_ref, inv_ref, o_ref, pad_ref, t_ref, *, l):
    n = 2 * l
    dot = functools.partial(jnp.dot, preferred_element_type=f32, precision=HIGHEST)
    w = HY_WIDTH

    def conv3(col):
        pad_ref[0:8, :] = jnp.zeros((8, w), f32)
        pad_ref[l + 8:l + 16, :] = jnp.zeros((8, w), f32)
        pad_ref[8:8 + l, :] = p_ref[0, :, col:col + w].astype(f32)
        cw = cw_ref[:, col - C_HY_VX:col - C_HY_VX + w].astype(f32)
        return cw[0:1] * pad_ref[7:7 + l, :] + cw[1:2] * pad_ref[8:8 + l, :] + cw[2:3] * pad_ref[9:9 + l, :]

    y = conv3(C_HY_VX)
    for o in range(HY_ORDER):
        gate = conv3(C_HY_VX + (o + 1) * w)
        hh = dot(fwd_ref[...], kern_ref[:, o * w:(o + 1) * w])
        xx = dot(fwd_ref[:, :l], y)
        xr, xi, hr, hi = xx[:n], xx[n:], hh[:n], hh[n:]
        t_ref[0:n, :] = xr * hr - xi * hi
        t_ref[n:2 * n, :] = xr * hi + xi * hr
        conv = dot(inv_ref[...], t_ref[...])
        y = gate * (conv + y * skip_ref[o:o + 1, :].astype(f32))
    z = p_ref[0, :, C_HY_Z:C_HY_Z + w].astype(f32)
    o_ref[0] = (y * (z * jax.nn.sigmoid(z))).astype(o_ref.dtype)


def hyena_ctx(pc, conv_w, skip, w1, b1, w2, b2, w3, b3, freq, wout):
    b, lc, _ = pc.shape
    n = 2 * lc
    kern = hyena_filter_circular(lc, w1, b1, w2, b2, w3, b3, freq, wout)
    fwd, inv = [jnp.asarray(a, f32) for a in _dense_dft_consts(lc)]
    full = lambda a: pl.BlockSpec(a.shape, lambda bi: (0,) * a.ndim)
    return pl.pallas_call(
        functools.partial(_hy_ctx_kernel, l=lc),
        grid=(b,),
        in_specs=[pl.BlockSpec((1, lc, MAIN_W), lambda bi: (bi, 0, 0)), full(kern), full(conv_w), full(skip),
                  full(fwd), full(inv)],
        out_specs=pl.BlockSpec((1, lc, HY_WIDTH), lambda bi: (bi, 0, 0)),
        out_shape=jax.ShapeDtypeStruct((b, lc, HY_WIDTH), bf16),
        scratch_shapes=[pltpu.VMEM((lc + 16, HY_WIDTH), f32), pltpu.VMEM((2 * n, HY_WIDTH), f32)],
        compiler_params=pltpu.CompilerParams(dimension_semantics=("arbitrary",), vmem_limit_bytes=VMEM_LIMIT),
        name="hyena_ctx",
    )(pc, kern, conv_w, skip, fwd, inv)


DN_PREP_ROWS = 256
DN_SUB_ROWS = 64
DN_HALO = 16


@functools.lru_cache(maxsize=None)
def _shift_matrix():
    s = np.zeros((DN_CONV_W * DN_SUB_ROWS, DN_SUB_ROWS + 2 * DN_HALO), np.float32)
    for j in range(DN_CONV_W):
        for m in range(DN_SUB_ROWS):
            s[j * DN_SUB_ROWS + m, m + j + DN_HALO - DN_CONV_W // 2] = 1.0
    return s


@functools.lru_cache(maxsize=None)
def _rope_tables(l):
    half = DN_HEAD_DIM // 2
    nf = half // 2
    t = np.arange(l)
    inv = (np.float32(ROPE_BASE) ** (-np.arange(nf, dtype=np.float32) / nf)).astype(np.float32)
    ang_r = ((t // GRID_W).astype(np.float32)[:, None] * inv[None, :]).astype(np.float32)
    ang_c = ((t % GRID_W).astype(np.float32)[:, None] * inv[None, :]).astype(np.float32)
    cos = np.concatenate([np.cos(ang_r), np.cos(ang_r), np.cos(ang_c), np.cos(ang_c)], axis=1)
    sin = np.concatenate([-np.sin(ang_r), np.sin(ang_r), -np.sin(ang_c), np.sin(ang_c)], axis=1)
    return cos.astype(np.float32), sin.astype(np.float32)


def _dn_prep_kernel(*refs, rope, n_tiles):
    (q_ref, qp_ref, qn_ref, k_ref, kp_ref, kn_ref, v_ref, vp_ref, vn_ref, g_ref, cw_ref, sh_ref, cos_ref, sin_ref,
     al_ref, dt_ref, qo_ref, ko_ref, vo_ref, go_ref, pad_ref, taps_ref) = refs
    i = pl.program_id(1)
    t = DN_PREP_ROWS
    w = DN_WIDTH
    lane = lax.broadcasted_iota(jnp.int32, (DN_SUB_ROWS, w), 1)
    first_half = (lane % (DN_HEAD_DIM // 2)) < (DN_HEAD_DIM // 4)
    for idx, (m_ref, p_ref, n_ref, o_ref) in enumerate(((q_ref, qp_ref, qn_ref, qo_ref), (k_ref, kp_ref, kn_ref, ko_ref),
                                                        (v_ref, vp_ref, vn_ref, vo_ref))):
        pad_ref[0:DN_HALO, :] = jnp.where(i == 0, jnp.zeros_like(p_ref[0]), p_ref[0])
        pad_ref[DN_HALO:DN_HALO + t, :] = m_ref[0]
        pad_ref[DN_HALO + t:2 * DN_HALO + t, :] = jnp.where(i == n_tiles - 1, jnp.zeros_like(n_ref[0]), n_ref[0])
        cw = cw_ref[:, idx * w:(idx + 1) * w].astype(f32)
        for s in range(t // DN_SUB_ROWS):
            taps_ref[s] = jnp.dot(sh_ref[...], pad_ref[s * DN_SUB_ROWS:(s + 1) * DN_SUB_ROWS + 2 * DN_HALO, :],
                                  preferred_element_type=f32)

        def sub(s, c):
            r = s * DN_SUB_ROWS
            y = cw[0:1] * taps_ref[s, 0:DN_SUB_ROWS, :]
            for j in range(1, DN_CONV_W):
                y = y + cw[j:j + 1] * taps_ref[s, j * DN_SUB_ROWS:(j + 1) * DN_SUB_ROWS, :]
            y = y * jax.nn.sigmoid(y)
            if idx < 2:
                if rope:
                    cos = jnp.concatenate([cos_ref[pl.ds(r, DN_SUB_ROWS), :]] * DN_HEADS, axis=1)
                    sin = jnp.concatenate([sin_ref[pl.ds(r, DN_SUB_ROWS), :]] * DN_HEADS, axis=1)
                    q4 = DN_HEAD_DIM // 4
                    swapped = jnp.where(first_half, pltpu.roll(y, w - q4, 1), pltpu.roll(y, q4, 1))
                    y = y * cos + swapped * sin
                outs = []
                for h in range(DN_HEADS):
                    yh = y[:, h * DN_HEAD_DIM:(h + 1) * DN_HEAD_DIM]
                    nrm = lax.rsqrt(jnp.sum(yh * yh, axis=-1, keepdims=True) + NORM_EPS)
                    if idx == 0:
                        nrm = nrm * (DN_HEAD_DIM ** -0.5)
                    outs.append(yh * nrm)
                y = jnp.concatenate(outs, axis=1)
            o_ref[0, pl.ds(r, DN_SUB_ROWS), :] = y.astype(o_ref.dtype)
            return c
        for s in range(t // DN_SUB_ROWS):
            sub(s, 0)

    g = g_ref[0]
    glane = lax.broadcasted_iota(jnp.int32, g.shape, 1)
    xa = g + dt_ref[...]
    softplus = jnp.maximum(xa, 0.0) + jnp.log1p(jnp.exp(-jnp.abs(xa)))
    go_ref[0] = jnp.where(glane < 2 * DN_HEADS, jax.nn.sigmoid(g), -jnp.exp(al_ref[...]) * softplus)


def dn_prep(p, gates_raw, conv_w, a_log, dt_bias, rope):
    b, l, _ = p.shape
    t = DN_PREP_ROWS
    w = DN_WIDTH
    n_tiles = l // t
    hb = t // DN_HALO
    cblk = C_DN_QKV // w
    cos, sin = [jnp.asarray(a) for a in _rope_tables(l)]
    shift = jnp.asarray(_shift_matrix(), bf16)
    pad = jnp.zeros((2 * DN_HEADS,), f32)
    rest = jnp.zeros((GATE_W - 4 * DN_HEADS,), f32)
    al = jnp.concatenate([pad, a_log.reshape(-1), rest])[None]
    dt = jnp.concatenate([pad, dt_bias.reshape(-1), rest])[None]

    def slab(j):
        return [pl.BlockSpec((1, t, w), lambda bi, i: (bi, i, cblk + j)),
                pl.BlockSpec((1, DN_HALO, w), lambda bi, i: (bi, jnp.maximum(i * hb - 1, 0), cblk + j)),
                pl.BlockSpec((1, DN_HALO, w), lambda bi, i: (bi, jnp.minimum((i + 1) * hb, l // DN_HALO - 1), cblk + j))]

    row = lambda a: pl.BlockSpec(a.shape, lambda bi, i: (0, 0))
    tab = pl.BlockSpec((t, DN_HEAD_DIM), lambda bi, i: (i, 0))
    out_blk = pl.BlockSpec((1, t, w), lambda bi, i: (bi, i, 0))
    return pl.pallas_call(
        functools.partial(_dn_prep_kernel, rope=rope, n_tiles=n_tiles),
        grid=(b, n_tiles),
        in_specs=slab(0) + slab(1) + slab(2) + [
            pl.BlockSpec((1, t, GATE_W), lambda bi, i: (bi, i, 0)), row(conv_w), row(shift), tab, tab, row(al),
            row(dt)],
        out_specs=[out_blk, out_blk, out_blk, pl.BlockSpec((1, t, GATE_W), lambda bi, i: (bi, i, 0))],
        out_shape=[jax.ShapeDtypeStruct((b, l, w), bf16)] * 3 + [jax.ShapeDtypeStruct((b, l, GATE_W), f32)],
        scratch_shapes=[pltpu.VMEM((t + 2 * DN_HALO, w), bf16),
                        pltpu.VMEM((t // DN_SUB_ROWS, DN_CONV_W * DN_SUB_ROWS, w), f32)],
        compiler_params=pltpu.CompilerParams(dimension_semantics=("arbitrary", "arbitrary"),
                                             vmem_limit_bytes=VMEM_LIMIT),
        name="dn_prep",
    )(p, p, p, p, p, p, p, p, p, gates_raw, conv_w, shift, cos, sin, al, dt)


DN_GROUP = 4
DN_GROUP_ROWS = DN_GROUP * DN_CHUNK
DN_BATCH_BLOCK = 2
DN_UNROLL = 2


def _bdot(a, b):
    return jnp.dot(a.astype(bf16), b.astype(bf16), preferred_element_type=f32)


def _dn_chunk_kernel(qc_ref, kc_ref, vc_ref, gc_ref, qx_ref, kx_ref, vx_ref, gx_ref, u_ref, wq_ref, ak_ref, eg_ref):
    g = pl.program_id(1)
    c = DN_CHUNK
    dh = DN_HEAD_DIM
    is_ctx = g == 0
    ri = lax.broadcasted_iota(jnp.int32, (c, c), 0)
    ci = lax.broadcasted_iota(jnp.int32, (c, c), 1)
    eye = (ri == ci).astype(f32)
    er = lax.broadcasted_iota(jnp.int32, (GATE_W, GATE_W), 0)
    ec = lax.broadcasted_iota(jnp.int32, (GATE_W, GATE_W), 1)
    eye_b = (er == ec).astype(bf16)
    tri_b = (ri >= ci).astype(bf16)
    masks = [((ri <= ci) if d else (ri >= ci), (ri < ci) if d else (ri > ci)) for d in range(2)]

    def sel_dot(sel, x, dims):
        hi = x.astype(bf16)
        mid = (x - hi.astype(f32)).astype(bf16)
        lo = (x - hi.astype(f32) - mid.astype(f32)).astype(bf16)
        return sum(lax.dot_general(sel, t, dims, preferred_element_type=f32) for t in (hi, mid, lo))

    def body(it, carry):
        units = []
        for jj in range(DN_UNROLL):
            j = it * DN_UNROLL + jj
            r = pl.multiple_of(j * c, c)
            pick = lambda a_ref, b_ref: jnp.where(is_ctx, a_ref[0, pl.ds(r, c), :], b_ref[0, pl.ds(r, c), :])
            q, k, v, gates = pick(qc_ref, qx_ref), pick(kc_ref, kx_ref), pick(vc_ref, vx_ref), pick(gc_ref, gx_ref)
            gtot = jnp.sum(gates, axis=0, keepdims=True)
            gfwd = sel_dot(tri_b, gates, (((1,), (0,)), ((), ())))
            gcs = (gfwd, gtot - gfwd + gates)
            g_t = sel_dot(eye_b, jnp.concatenate([gfwd, gates], axis=0), NT_DIMS)
            gcs_t = (g_t[:, :c], jnp.sum(g_t[:, c:], axis=1, keepdims=True) - g_t[:, :c] + g_t[:, c:])
            qks = [lax.dot_general(q[:, h * dh:(h + 1) * dh], k[:, h * dh:(h + 1) * dh], NT_DIMS,
                                   preferred_element_type=f32) for h in range(DN_HEADS)]
            kks = [lax.dot_general(k[:, h * dh:(h + 1) * dh], k[:, h * dh:(h + 1) * dh], NT_DIMS,
                                   preferred_element_type=f32) for h in range(DN_HEADS)]
            for d in range(2):
                incl, strict = masks[d]
                egs = []
                for h in range(DN_HEADS):
                    hs = slice(h * dh, (h + 1) * dh)
                    col = d * DN_HEADS + h
                    gl = 2 * DN_HEADS + col
                    kh, vh, qh = k[:, hs].astype(f32), v[:, hs].astype(f32), q[:, hs].astype(f32)
                    beta = gates[:, col:col + 1]
                    gcol = gcs[d][:, gl:gl + 1]
                    glast = gtot[:, gl:gl + 1]
                    decay = jnp.where(incl, jnp.exp(jnp.minimum(gcol - gcs_t[d][gl:gl + 1, :], 0.0)), 0.0)
                    eg = jnp.exp(gcol)
                    wq_ref[d, 0, j, h, c:2 * c, :] = (qh * eg).astype(bf16)
                    ak_ref[d, 0, j, h, 0:c, :] = jnp.where(incl, qks[h] * decay, 0.0).astype(bf16)
                    ak_ref[d, 0, j, h, c:c + dh, :] = jnp.transpose(kh * jnp.exp(glast - gcol)).astype(bf16)
                    egs.append(jnp.broadcast_to(jnp.exp(glast), (1, GATE_W)))
                    kb = kh * beta
                    units.append(dict(j=j, d=d, h=h, n=-jnp.where(strict, beta * kks[h] * decay, 0.0),
                                      rhs=jnp.concatenate([vh * beta, kb * eg], axis=1).astype(bf16)))
                eg_ref[d, 0, j] = jnp.concatenate(egs + [jnp.zeros((8 - DN_HEADS, GATE_W), f32)], axis=0)
        for un in units:
            un["t"] = eye + un["n"]
            un["p"] = _bdot(un["n"], un["n"])
        for _ in range(int(math.log2(c)) - 2):
            for un in units:
                both = _bdot(jnp.concatenate([un["p"], un["t"]], axis=0), un["p"])
                un["p"], un["t"] = both[:c], un["t"] + both[c:]
        for un in units:
            un["t"] = un["t"] + _bdot(un["t"], un["p"])
        for un in units:
            sol = _bdot(un["t"], un["rhs"])
            u_ref[un["d"], 0, un["j"], un["h"]] = sol[:, :dh].astype(bf16)
            wq_ref[un["d"], 0, un["j"], un["h"], 0:c, :] = sol[:, dh:].astype(bf16)
        return carry
    lax.fori_loop(0, DN_GROUP // DN_UNROLL, body, 0)


def dn_chunks(qc, kc, vc, gc, qx, kx, vx, gx):
    b, lc, w = qc.shape
    l = qx.shape[1]
    assert lc == DN_GROUP_ROWS and l % DN_GROUP_ROWS == 0
    ng = 1 + l // DN_GROUP_ROWS
    nch = ng * DN_GROUP
    c, dh, nhd = DN_CHUNK, DN_HEAD_DIM, DN_HEADS
    cmap = lambda bi, g: (bi, 0, 0)
    xmap = lambda bi, g: (bi, jnp.maximum(g - 1, 0), 0)
    blk = lambda width, m: pl.BlockSpec((1, DN_GROUP_ROWS, width), m)
    omap = lambda bi, g: (0, bi, g, 0, 0, 0)
    return pl.pallas_call(
        _dn_chunk_kernel,
        grid=(b, ng),
        in_specs=[blk(w, cmap), blk(w, cmap), blk(w, cmap), blk(GATE_W, cmap),
                  blk(w, xmap), blk(w, xmap), blk(w, xmap), blk(GATE_W, xmap)],
        out_specs=[pl.BlockSpec((2, 1, DN_GROUP, nhd, c, dh), omap),
                   pl.BlockSpec((2, 1, DN_GROUP, nhd, 2 * c, dh), omap),
                   pl.BlockSpec((2, 1, DN_GROUP, nhd, c + dh, c), omap),
                   pl.BlockSpec((2, 1, DN_GROUP, 8, GATE_W), lambda bi, g: (0, bi, g, 0, 0))],
        out_shape=[jax.ShapeDtypeStruct((2, b, nch, nhd, c, dh), bf16),
                   jax.ShapeDtypeStruct((2, b, nch, nhd, 2 * c, dh), bf16),
                   jax.ShapeDtypeStruct((2, b, nch, nhd, c + dh, c), bf16),
                   jax.ShapeDtypeStruct((2, b, nch, 8, GATE_W), f32)],
        compiler_params=pltpu.CompilerParams(dimension_semantics=("arbitrary", "arbitrary"),
                                             vmem_limit_bytes=VMEM_LIMIT),
        name="dn_chunks",
    )(qc, kc, vc, gc, qx, kx, vx, gx)


def _bwd_group(s, ng):
    return jnp.where(s == 0, 0, ng - s)


def _dn_serial_kernel(uf_ref, wqf_ref, akf_ref, egf_ref, ub_ref, wqb_ref, akb_ref, egb_ref, of_ref, ob_ref, s_ref):
    c = DN_CHUNK
    dh = DN_HEAD_DIM

    @pl.when(pl.program_id(1) == 0)
    def _():
        s_ref[...] = jnp.zeros(s_ref.shape, f32)

    dirs = ((uf_ref, wqf_ref, akf_ref, egf_ref, of_ref), (ub_ref, wqb_ref, akb_ref, egb_ref, ob_ref))
    units = [(e, d, h) for e in range(DN_BATCH_BLOCK) for d in range(2) for h in range(DN_HEADS)]
    states = {un: s_ref[un] for un in units}
    for jj in range(DN_GROUP):
        js = (jj, DN_GROUP - 1 - jj)
        r1 = {(e, d, h): jnp.dot(dirs[d][1][0, e, js[d], h], states[e, d, h].astype(bf16), preferred_element_type=f32)
              for e, d, h in units}
        r2 = {}
        for e, d, h in units:
            v_new = dirs[d][0][0, e, js[d], h].astype(f32) - r1[e, d, h][:c]
            r2[e, d, h] = jnp.dot(dirs[d][2][0, e, js[d], h], v_new.astype(bf16), preferred_element_type=f32)
        for e, d, h in units:
            j = js[d]
            dirs[d][4][e, j * c:(j + 1) * c, h * dh:(h + 1) * dh] = (r1[e, d, h][c:] + r2[e, d, h][:c]).astype(bf16)
            states[e, d, h] = states[e, d, h] * dirs[d][3][0, e, j, h:h + 1, :] + r2[e, d, h][c:]
    for un in units:
        s_ref[un] = states[un]


def dn_serial(u, wq, ak, eg):
    _, b, nch, nhd, c, dh = u.shape
    ng = nch // DN_GROUP
    w = nhd * dh
    fmap6 = lambda bi, s: (0, bi, s, 0, 0, 0)
    bmap6 = lambda bi, s: (1, bi, _bwd_group(s, ng), 0, 0, 0)
    fmap5 = lambda bi, s: (0, bi, s, 0, 0)
    bmap5 = lambda bi, s: (1, bi, _bwd_group(s, ng), 0, 0)
    bb = DN_BATCH_BLOCK
    assert b % bb == 0
    blk6 = lambda a, m: pl.BlockSpec((1, bb, DN_GROUP) + a.shape[3:], m)
    egblk = lambda m: pl.BlockSpec((1, bb, DN_GROUP, 8, GATE_W), m)
    return pl.pallas_call(
        _dn_serial_kernel,
        grid=(b // bb, ng),
        in_specs=[blk6(u, fmap6), blk6(wq, fmap6), blk6(ak, fmap6), egblk(fmap5),
                  blk6(u, bmap6), blk6(wq, bmap6), blk6(ak, bmap6), egblk(bmap5)],
        out_specs=[pl.BlockSpec((bb, DN_GROUP_ROWS, w), lambda bi, s: (bi, s, 0)),
                   pl.BlockSpec((bb, DN_GROUP_ROWS, w), lambda bi, s: (bi, _bwd_group(s, ng), 0))],
        out_shape=[jax.ShapeDtypeStruct((b, nch * c, w), bf16)] * 2,
        scratch_shapes=[pltpu.VMEM((bb, 2, nhd, dh, dh), f32)],
        compiler_params=pltpu.CompilerParams(dimension_semantics=("arbitrary", "arbitrary"),
                                             vmem_limit_bytes=VMEM_LIMIT),
        name="dn_serial",
    )(u, wq, ak, eg, u, wq, ak, eg)


def _dn_out_kernel(of_ref, ob_ref, z_ref, nw_ref, y_ref):
    o = of_ref[0].astype(f32) + ob_ref[0].astype(f32)
    z = z_ref[0].astype(f32)
    nw = nw_ref[...].astype(f32)
    outs = []
    for h in range(DN_HEADS):
        oh = o[:, h * DN_HEAD_DIM:(h + 1) * DN_HEAD_DIM]
        outs.append(oh * lax.rsqrt(jnp.mean(oh * oh, axis=-1, keepdims=True) + NORM_EPS) * nw)
    y_ref[0] = (jnp.concatenate(outs, axis=1) * (z * jax.nn.sigmoid(z))).astype(y_ref.dtype)


def dn_output(o_f, o_b, p, norm_w, first_block):
    b, l, _ = p.shape
    w = DN_WIDTH
    tm = DN_GROUP_ROWS
    omap = lambda bi, i: (bi, first_block + i, 0)
    return pl.pallas_call(
        _dn_out_kernel,
        grid=(b, l // tm),
        in_specs=[pl.BlockSpec((1, tm, w), omap), pl.BlockSpec((1, tm, w), omap),
                  pl.BlockSpec((1, tm, w), lambda bi, i: (bi, i, C_DN_Z // w)),
                  pl.BlockSpec((1, DN_HEAD_DIM), lambda bi, i: (0, 0))],
        out_specs=pl.BlockSpec((1, tm, w), lambda bi, i: (bi, i, 0)),
        out_shape=jax.ShapeDtypeStruct((b, l, w), bf16),
        compiler_params=pltpu.CompilerParams(dimension_semantics=("arbitrary", "arbitrary"),
                                             vmem_limit_bytes=VMEM_LIMIT),
        name="dn_output",
    )(o_f, o_b, p, norm_w.reshape(1, DN_HEAD_DIM))


def _pack_w_in(w_in):
    depth, d, _ = w_in.shape
    main = jnp.concatenate([w_in[:, :, :3072], w_in[:, :, 3088:4112]], axis=-1).astype(bf16)
    gate = jnp.concatenate([w_in[:, :, 3072:3088], jnp.zeros((depth, d, GATE_W - 16), w_in.dtype)],
                           axis=-1).astype(bf16)
    return main, gate


def kernel(x, c, ctx, c_ctx, w_ada, b_ada, g_pre, g_post, w_in, w_out, na_rpb, dn_conv, dn_a_log, dn_dt_bias,
           dn_norm, hy_conv, hy_w1, hy_b1, hy_w2, hy_b2, hy_w3, hy_b3, hy_freq, hy_wout, hy_skip):
    bsz, l, d = x.shape
    lc = ctx.shape[1]
    depth = w_in.shape[0]
    cond = jnp.concatenate([c, c_ctx[None], jnp.zeros((7, d), f32)], axis=0)
    mod = modulation_all(cond, w_ada, b_ada)
    w_main, w_gate = _pack_w_in(w_in)
    w_out_b = w_out.astype(bf16)

    for i in range(depth):
        last = i == depth - 1
        shift_x, scale_x, gate_x = [mod[i, :bsz, j * d:(j + 1) * d][:, None] for j in range(3)]
        shift_c, scale_c, gate_c = [mod[i, bsz:bsz + 1, j * d:(j + 1) * d][:, None] for j in range(3)]
        px, gx = in_projection(x, g_pre[i], scale_x, shift_x, w_main[i], w_gate[i], 512)
        pc, gc = in_projection(ctx, g_pre[i], scale_c, shift_c, w_main[i], w_gate[i], 256)

        hy_args = (hy_conv[i], hy_skip[i], hy_w1[i], hy_b1[i], hy_w2[i], hy_b2[i], hy_w3[i], hy_b3[i], hy_freq[i],
                   hy_wout[i])
        dn_args = (dn_conv[i], dn_a_log[i], dn_dt_bias[i])

        out_a_x = na_attention(px, pc, na_rpb[i])

        dn_c = dn_prep(pc, gc, *dn_args, False)
        dn_x = dn_prep(px, gx, *dn_args, True)
        do_f, do_b = dn_serial(*dn_chunks(*dn_c, *dn_x))
        out_b_x = dn_output(do_f, do_b, px, dn_norm[i], lc // DN_GROUP_ROWS)

        out_c_x = hyena_latent(px, *hy_args)

        new_x = out_projection(out_a_x, out_b_x, out_c_x, w_out_b[i], g_post[i], gate_x, x, 512)

        if not last:
            out_a_c = ctx_attention(pc)
            out_b_c = dn_output(do_f, do_b, pc, dn_norm[i], 0)
            out_c_c = hyena_ctx(pc, *hy_args)
            ctx = out_projection(out_a_c, out_b_c, out_c_c, w_out_b[i], g_post[i], gate_c, ctx, lc)
        x = new_x
    return x
```

```python
import functools
import math

import numpy as np
import jax
import jax.numpy as jnp
from jax import lax
from jax.experimental import pallas as pl
from jax.experimental.pallas import tpu as pltpu

D_MODEL = 1024
GRID_W = 64
NA_HEAD_DIM = 64
NA_WIDTH = 256
NA_HEADS = 4
NA_WIN_ROWS = 8
NA_WIN_COLS = 16
DN_HEAD_DIM = 128
DN_WIDTH = 512
DN_HEADS = 4
DN_CONV_W = 5
DN_CHUNK = 64
HY_WIDTH = 256
HY_ORDER = 2
HY_EMB = 33
HY_DECAY_TARGET = 1e-2
HY_FAST_DECAY = 0.3
HY_SLOW_DECAY = 1.5
ROPE_BASE = 10000.0
NORM_EPS = 1e-6

C_NA_QKV, C_NA_Z, C_DN_QKV, C_DN_Z, C_HY_VX, C_HY_Z = 0, 768, 1024, 2560, 3072, 3840
MAIN_W = 4096
GATE_W = 128
LANE = 128
VMEM_LIMIT = 52 * 1024 * 1024

f32 = jnp.float32
bf16 = jnp.bfloat16
HIGHEST = lax.Precision.HIGHEST
NT_DIMS = (((1,), (1,)), ((), ()))
TN_DIMS = (((0,), (0,)), ((), ()))


def _mod_kernel(c_ref, w_ref, b_ref, o_ref):
    c = c_ref[...]
    a = c * jax.nn.sigmoid(c)
    o_ref[0] = jnp.dot(a, w_ref[0], preferred_element_type=f32, precision=lax.Precision.HIGHEST) + b_ref[0]


def modulation_all(cond, w_ada, b_ada):
    depth, d, d3 = w_ada.shape
    r = cond.shape[0]
    tn = 512
    return pl.pallas_call(
        _mod_kernel,
        grid=(depth, d3 // tn),
        in_specs=[pl.BlockSpec((r, d), lambda i, j: (0, 0)),
                  pl.BlockSpec((1, d, tn), lambda i, j: (i, 0, j)),
                  pl.BlockSpec((1, 1, tn), lambda i, j: (i, 0, j))],
        out_specs=pl.BlockSpec((1, r, tn), lambda i, j: (i, 0, j)),
        out_shape=jax.ShapeDtypeStruct((depth, r, d3), f32),
        compiler_params=pltpu.CompilerParams(dimension_semantics=("arbitrary", "arbitrary"),
                                             vmem_limit_bytes=VMEM_LIMIT),
        name="modulation",
    )(cond, w_ada, b_ada.reshape(depth, 1, d3))


def _inproj_kernel(x_ref, g_ref, sc_ref, sh_ref, w_ref, wg_ref, o_ref, og_ref):
    x = x_ref[0]
    ms = jnp.mean(x * x, axis=-1, keepdims=True)
    h = x * lax.rsqrt(ms + NORM_EPS) * g_ref[...]
    h = (h * (1.0 + sc_ref[0]) + sh_ref[0]).astype(bf16)
    for j in range(MAIN_W // 1024):
        o_ref[0, :, j * 1024:(j + 1) * 1024] = jnp.dot(
            h, w_ref[:, j * 1024:(j + 1) * 1024], preferred_element_type=f32).astype(bf16)
    og_ref[0] = jnp.dot(h, wg_ref[...], preferred_element_type=f32)


def in_projection(x, g, scale, shift, w_main, w_gate, tm):
    b, l, d = x.shape
    per_b = scale.shape[0] == b
    mod_map = (lambda bi, i: (bi, 0, 0)) if per_b else (lambda bi, i: (0, 0, 0))
    return pl.pallas_call(
        _inproj_kernel,
        grid=(b, l // tm),
        in_specs=[pl.BlockSpec((1, tm, d), lambda bi, i: (bi, i, 0)),
                  pl.BlockSpec((1, d), lambda bi, i: (0, 0)),
                  pl.BlockSpec((1, 1, d), mod_map),
                  pl.BlockSpec((1, 1, d), mod_map),
                  pl.BlockSpec((d, MAIN_W), lambda bi, i: (0, 0)),
                  pl.BlockSpec((d, GATE_W), lambda bi, i: (0, 0))],
        out_specs=[pl.BlockSpec((1, tm, MAIN_W), lambda bi, i: (bi, i, 0)),
                   pl.BlockSpec((1, tm, GATE_W), lambda bi, i: (bi, i, 0))],
        out_shape=[jax.ShapeDtypeStruct((b, l, MAIN_W), bf16),
                   jax.ShapeDtypeStruct((b, l, GATE_W), f32)],
        compiler_params=pltpu.CompilerParams(dimension_semantics=("arbitrary", "arbitrary"),
                                             vmem_limit_bytes=VMEM_LIMIT),
        name="in_projection",
    )(x, g.reshape(1, d), scale, shift, w_main, w_gate)


def _outproj_kernel(a_ref, b_ref, c_ref, w_ref, g_ref, gate_ref, x_ref, o_ref):
    y = (jnp.dot(a_ref[0], w_ref[0:NA_WIDTH], preferred_element_type=f32)
         + jnp.dot(b_ref[0], w_ref[NA_WIDTH:NA_WIDTH + DN_WIDTH], preferred_element_type=f32)
         + jnp.dot(c_ref[0], w_ref[NA_WIDTH + DN_WIDTH:], preferred_element_type=f32))
    ms = jnp.mean(y * y, axis=-1, keepdims=True)
    yn = y * lax.rsqrt(ms + NORM_EPS) * g_ref[...]
    o_ref[0] = x_ref[0] + gate_ref[0] * yn


def out_projection(out_a, out_b, out_c, w_out, g_post, gate, x, tm):
    b, l, d = x.shape
    per_b = gate.shape[0] == b
    mod_map = (lambda bi, i: (bi, 0, 0)) if per_b else (lambda bi, i: (0, 0, 0))
    part = lambda a: pl.BlockSpec((1, tm, a.shape[-1]), lambda bi, i: (bi, i, 0))
    return pl.pallas_call(
        _outproj_kernel,
        grid=(b, l // tm),
        in_specs=[part(out_a), part(out_b), part(out_c),
                  pl.BlockSpec((d, d), lambda bi, i: (0, 0)),
                  pl.BlockSpec((1, d), lambda bi, i: (0, 0)),
                  pl.BlockSpec((1, 1, d), mod_map),
                  pl.BlockSpec((1, tm, d), lambda bi, i: (bi, i, 0))],
        out_specs=pl.BlockSpec((1, tm, d), lambda bi, i: (bi, i, 0)),
        out_shape=jax.ShapeDtypeStruct((b, l, d), f32),
        compiler_params=pltpu.CompilerParams(dimension_semantics=("arbitrary", "arbitrary"),
                                             vmem_limit_bytes=VMEM_LIMIT),
        name="out_projection",
    )(out_a, out_b, out_c, w_out, g_post.reshape(1, d), gate, x)


NA_TQ_ROWS = 4
NA_TQ = NA_TQ_ROWS * GRID_W
NA_KV_ROWS = NA_TQ_ROWS + NA_WIN_ROWS
NA_KV = NA_KV_ROWS * GRID_W
MASK_VALUE = -1e30


def _na_window_start(t, rows):
    return jnp.clip(t * NA_TQ_ROWS - NA_WIN_ROWS // 2, 0, rows - NA_KV_ROWS)


def _na_bias_kernel(rpb_ref, sel_ref, mask_ref, o_ref):
    o_ref[...] = jnp.dot(rpb_ref[...], sel_ref[...], preferred_element_type=f32, precision=HIGHEST) + mask_ref[...]


def na_bias_tables(rpb, rows):
    nh, ndr, ndc = rpb.shape
    col = np.arange(GRID_W)
    dc = np.clip(col[None, :] - col[:, None], -(NA_WIN_COLS - 1), NA_WIN_COLS - 1) + (NA_WIN_COLS - 1)
    c0 = np.clip(col - NA_WIN_COLS // 2, 0, GRID_W - NA_WIN_COLS)
    in_win = (col[None, :] >= c0[:, None]) & (col[None, :] < c0[:, None] + NA_WIN_COLS)
    sel = np.zeros((LANE, GRID_W * GRID_W), np.float32)
    sel[dc.reshape(-1), np.arange(GRID_W * GRID_W)] = 1.0
    cmask = np.where(in_win, 0.0, MASK_VALUE).astype(np.float32).reshape(1, -1)
    rpb2 = jnp.pad(rpb.astype(f32).reshape(nh * ndr, ndc), ((0, 0), (0, LANE - ndc)))
    tab = pl.pallas_call(
        _na_bias_kernel,
        out_shape=jax.ShapeDtypeStruct((nh * ndr, GRID_W * GRID_W), f32),
        name="na_bias",
    )(rpb2, jnp.asarray(sel), jnp.asarray(cmask)).reshape(nh, ndr, GRID_W, GRID_W)
    masked = jnp.full((nh, GRID_W, GRID_W), MASK_VALUE, f32)
    n_tiles = rows // NA_TQ_ROWS
    tabs = []
    for t in (0, 1, n_tiles - 1):
        r0 = t * NA_TQ_ROWS
        ws = int(np.clip(r0 - NA_WIN_ROWS // 2, 0, rows - NA_KV_ROWS))
        blocks = []
        for qr in range(r0, r0 + NA_TQ_ROWS):
            band0 = int(np.clip(qr - NA_WIN_ROWS // 2, 0, rows - NA_WIN_ROWS))
            blocks.append(jnp.concatenate(
                [tab[:, kr - qr + NA_WIN_ROWS - 1] if band0 <= kr < band0 + NA_WIN_ROWS else masked
                 for kr in range(ws, ws + NA_KV_ROWS)], axis=-1))
        tabs.append(jnp.concatenate(blocks, axis=-2))
    return jnp.stack(tabs)


def _na_kernel(q_ref, k_ref, v_ref, z_ref, kc_ref, vc_ref, bias_ref, o_ref, *, rows):
    t = pl.program_id(1)
    ws = pl.multiple_of(_na_window_start(t, rows) * GRID_W, GRID_W)
    q = q_ref[0]
    kw = k_ref[0, pl.ds(ws, NA_KV), :]
    vw = v_ref[0, pl.ds(ws, NA_KV), :]
    kc = kc_ref[0]
    vc = vc_ref[0]
    hd = NA_HEAD_DIM
    low_w = lax.broadcasted_iota(jnp.int32, (NA_KV, LANE), 1) < hd
    low_c = lax.broadcasted_iota(jnp.int32, (kc.shape[0], LANE), 1) < hd
    low_q = lax.broadcasted_iota(jnp.int32, (NA_TQ, LANE), 1) < hd
    tiles = []
    for hp in range(NA_HEADS * hd // LANE):
        tile_w = vw[:, hp * LANE:(hp + 1) * LANE]
        tile_c = vc[:, hp * LANE:(hp + 1) * LANE]
        normed = []
        for par in range(LANE // hd):
            h = hp * (LANE // hd) + par
            sl = slice(h * hd, (h + 1) * hd)
            qh = q[:, sl] * (hd ** -0.5)
            s1 = lax.dot_general(qh, kw[:, sl], NT_DIMS, preferred_element_type=f32) + bias_ref[0, h]
            s2 = lax.dot_general(qh, kc[:, sl], NT_DIMS, preferred_element_type=f32)
            m = jnp.maximum(jnp.max(s1, axis=-1, keepdims=True), jnp.max(s2, axis=-1, keepdims=True))
            p1 = jnp.exp((s1 - m).astype(bf16))
            p2 = jnp.exp((s2 - m).astype(bf16))
            keep_w, keep_c = (low_w, low_c) if par == 0 else (~low_w, ~low_c)
            oa = (jnp.dot(p1, jnp.where(keep_w, tile_w, 1.0).astype(bf16), preferred_element_type=f32)
                  + jnp.dot(p2, jnp.where(keep_c, tile_c, 1.0).astype(bf16), preferred_element_type=f32))
            den = oa[:, hd:hd + 1] if par == 0 else oa[:, 0:1]
            normed.append(oa / den)
        tiles.append(jnp.where(low_q, normed[0], normed[1]))
    z = z_ref[0].astype(f32)
    o = jnp.concatenate(tiles, axis=-1) * (z * jax.nn.sigmoid(z))
    o_ref[0] = o.astype(o_ref.dtype)


def na_attention(px, pc, rpb):
    b, l, _ = px.shape
    lc = pc.shape[1]
    rows = l // GRID_W
    n_tiles = rows // NA_TQ_ROWS
    bias = na_bias_tables(rpb, rows)
    w = NA_WIDTH
    cq, ck, cv, cz = [(C_NA_QKV + j * w) // w for j in range(3)] + [C_NA_Z // w]

    def bias_map(bi, t):
        return (jnp.where(t == 0, 0, jnp.where(t == n_tiles - 1, 2, 1)), 0, 0, 0)

    return pl.pallas_call(
        functools.partial(_na_kernel, rows=rows),
        grid=(b, n_tiles),
        in_specs=[pl.BlockSpec((1, NA_TQ, w), lambda bi, t: (bi, t, cq)),
                  pl.BlockSpec((1, l, w), lambda bi, t: (bi, 0, ck)),
                  pl.BlockSpec((1, l, w), lambda bi, t: (bi, 0, cv)),
                  pl.BlockSpec((1, NA_TQ, w), lambda bi, t: (bi, t, cz)),
                  pl.BlockSpec((1, lc, w), lambda bi, t: (bi, 0, ck)),
                  pl.BlockSpec((1, lc, w), lambda bi, t: (bi, 0, cv)),
                  pl.BlockSpec((1, NA_HEADS, NA_TQ, NA_KV), bias_map)],
        out_specs=pl.BlockSpec((1, NA_TQ, w), lambda bi, t: (bi, t, 0)),
        out_shape=jax.ShapeDtypeStruct((b, l, w), bf16),
        compiler_params=pltpu.CompilerParams(dimension_semantics=("arbitrary", "arbitrary"),
                                             vmem_limit_bytes=VMEM_LIMIT),
        name="na_attention",
    )(px, px, px, px, pc, pc, bias)


def _ctx_attn_kernel(q_ref, k_ref, v_ref, z_ref, o_ref):
    q = q_ref[0]
    k = k_ref[0]
    v = v_ref[0]
    outs = []
    for h in range(NA_HEADS):
        sl = slice(h * NA_HEAD_DIM, (h + 1) * NA_HEAD_DIM)
        qh = q[:, sl] * (NA_HEAD_DIM ** -0.5)
        s = lax.dot_general(qh, k[:, sl], NT_DIMS, preferred_element_type=f32)
        p = jnp.exp(s - jnp.max(s, axis=-1, keepdims=True))
        den = jnp.sum(p, axis=-1, keepdims=True)
        outs.append(jnp.dot(p.astype(bf16), v[:, sl], preferred_element_type=f32) / den)
    z = z_ref[0].astype(f32)
    o_ref[0] = (jnp.concatenate(outs, axis=-1) * (z * jax.nn.sigmoid(z))).astype(o_ref.dtype)


def ctx_attention(pc):
    b, lc, _ = pc.shape
    w = NA_WIDTH
    cq, ck, cv, cz = [(C_NA_QKV + j * w) // w for j in range(3)] + [C_NA_Z // w]
    spec = lambda cidx: pl.BlockSpec((1, lc, w), lambda bi: (bi, 0, cidx))
    return pl.pallas_call(
        _ctx_attn_kernel,
        grid=(b,),
        in_specs=[spec(cq), spec(ck), spec(cv), spec(cz)],
        out_specs=pl.BlockSpec((1, lc, w), lambda bi: (bi, 0, 0)),
        out_shape=jax.ShapeDtypeStruct((b, lc, w), bf16),
        compiler_params=pltpu.CompilerParams(dimension_semantics=("arbitrary",), vmem_limit_bytes=VMEM_LIMIT),
        name="ctx_attention",
    )(pc, pc, pc, pc)


HY_CH = 128
FFT_N2 = 128
ROW_CHUNK = 256
STRIDE_PAD = 8
HY_FILT_ROWS = 512
HY_BANDS = (HY_EMB - 1) // 2


def _hy_filter_kernel(fr_ref, w1t_ref, w1c_ref, w1s_ref, b1_ref, w2_ref, b2_ref, w3_ref, b3_ref, freq_ref, wo_ref,
                      dl_ref, o_ref, *, l):
    i = pl.program_id(0)
    n = i * HY_FILT_ROWS + lax.broadcasted_iota(jnp.int32, (HY_FILT_ROWS, 1), 0)
    lag = jnp.where(n < l, n, 2 * l - n).astype(f32)
    t = lag * (1.0 / (l - 1))
    ang = (lag * (2.0 * math.pi / l)) * fr_ref[...]
    dot = functools.partial(jnp.dot, preferred_element_type=f32, precision=HIGHEST)
    pre = t * w1t_ref[...] + dot(jnp.cos(ang), w1c_ref[...]) - dot(jnp.sin(ang), w1s_ref[...]) + b1_ref[...]
    hid = jnp.sin(freq_ref[0:1] * pre)
    hid = jnp.sin(freq_ref[1:2] * (dot(hid, w2_ref[...]) + b2_ref[...]))
    hid = jnp.sin(freq_ref[2:3] * (dot(hid, w3_ref[...]) + b3_ref[...]))
    filt = dot(hid, wo_ref[...])
    window = jnp.exp(-t * dl_ref[...])
    live = jnp.where(n == l, 0.0, 1.0)
    fwd = n < l
    for o in range(HY_ORDER):
        a = filt[:, (2 * o) * HY_WIDTH:(2 * o + 1) * HY_WIDTH]
        b = filt[:, (2 * o + 1) * HY_WIDTH:(2 * o + 2) * HY_WIDTH]
        o_ref[:, o * HY_WIDTH:(o + 1) * HY_WIDTH] = jnp.where(fwd, a, b) * window * live


def hyena_filter_circular(l, w1, b1, w2, b2, w3, b3, freq, wout):
    n = 2 * l
    assert n % HY_FILT_ROWS == 0
    hid = w1.shape[1]
    fr = np.zeros((1, LANE), np.float32)
    fr[0, :HY_BANDS] = np.linspace(1e-4, HY_BANDS - 1, HY_BANDS, dtype=np.float32)
    padrows = lambda w: jnp.concatenate([w, jnp.zeros((LANE - w.shape[0], hid), f32)], axis=0)
    w1t, w1c, w1s = w1[0:1], padrows(w1[1:1 + HY_BANDS]), padrows(w1[1 + HY_BANDS:])
    max_decay = math.log(HY_DECAY_TARGET) / HY_FAST_DECAY
    min_decay = math.log(HY_DECAY_TARGET) / HY_SLOW_DECAY
    deltas = np.abs(np.linspace(min_decay, max_decay, HY_WIDTH, dtype=np.float32))[None]
    full = lambda a: pl.BlockSpec(a.shape, lambda i: (0,) * a.ndim)
    args = [jnp.asarray(fr), w1t, w1c, w1s, b1[None], w2, b2[None], w3, b3[None], freq, wout, jnp.asarray(deltas)]
    return pl.pallas_call(
        functools.partial(_hy_filter_kernel, l=l),
        grid=(n // HY_FILT_ROWS,),
        in_specs=[full(a) for a in args],
        out_specs=pl.BlockSpec((HY_FILT_ROWS, HY_ORDER * HY_WIDTH), lambda i: (i, 0)),
        out_shape=jax.ShapeDtypeStruct((n, HY_ORDER * HY_WIDTH), f32),
        compiler_params=pltpu.CompilerParams(dimension_semantics=("arbitrary",), vmem_limit_bytes=VMEM_LIMIT),
        name="hyena_filter",
    )(*args)


@functools.lru_cache(maxsize=None)
def _fft_consts(n, k_in, k_out):
    n1 = n // FFT_N2
    k1 = np.arange(n1)
    n2 = np.arange(FFT_N2)
    tt = FFT_N2 * np.arange(k_in)[None, None, :] + n2[:, None, None]
    ang = -2.0 * np.pi * (k1[None, :, None] * tt) / n
    f1 = np.concatenate([np.cos(ang), np.sin(ang)], axis=1)
    a2 = -2.0 * np.pi * np.outer(n2, n2) / FFT_N2
    cr, ci = np.cos(a2), np.sin(a2)
    fblk = np.block([[cr, -ci], [ci, cr]])
    fiblk = np.block([[cr, ci], [-ci, cr]])
    to = FFT_N2 * np.arange(k_out)[None, :, None] + n2[:, None, None]
    ango = 2.0 * np.pi * (k1[None, None, :] * to) / n
    hinv = np.concatenate([np.cos(ango), -np.sin(ango)], axis=2) / n
    return f1, fblk, fiblk, hinv


FFT_OUTER_UNROLL = 8
FFT_INNER_UNROLL = 2
FFT_INNER_GROUPS = 8


def _fft_fwd_stage1(x_ref, f1_ref, ar_ref, ai_ref, k_in, n1, sa, dt, prec):
    def body(it, c):
        n2s = [it * FFT_OUTER_UNROLL + u for u in range(FFT_OUTER_UNROLL)]
        xs = [x_ref[pl.ds(n2, k_in, stride=FFT_N2), :].astype(dt) for n2 in n2s]
        res = [jnp.dot(f1_ref[n2], x, preferred_element_type=f32, precision=prec) for n2, x in zip(n2s, xs)]
        for n2, a in zip(n2s, res):
            off = pl.multiple_of(n2 * sa, 8)
            ar_ref[pl.ds(off, n1), :] = a[:n1]
            ai_ref[pl.ds(off, n1), :] = a[n1:]
        return c
    lax.fori_loop(0, FFT_N2 // FFT_OUTER_UNROLL, body, 0)


def _fft_inner_load(ar_ref, ai_ref, k1s, sa, dt):
    zr = jnp.concatenate([ar_ref[pl.ds(k1, FFT_N2, stride=sa), :] for k1 in k1s], axis=1)
    zi = jnp.concatenate([ai_ref[pl.ds(k1, FFT_N2, stride=sa), :] for k1 in k1s], axis=1)
    return jnp.concatenate([zr, zi], axis=0).astype(dt)


def _filter_fft_kernel(k_ref, f1_ref, fblk_ref, hr_ref, hi_ref, ar_ref, ai_ref, *, n):
    n1 = n // FFT_N2
    sa = n1 + STRIDE_PAD
    ch = k_ref.shape[-1]
    _fft_fwd_stage1(k_ref, f1_ref, ar_ref, ai_ref, n1, n1, sa, bf16, None)

    def body(it, c):
        k1s = [it * FFT_INNER_UNROLL + u for u in range(FFT_INNER_UNROLL)]
        xx = jnp.dot(fblk_ref[...], _fft_inner_load(ar_ref, ai_ref, k1s, sa, bf16), preferred_element_type=f32)
        for u, k1 in enumerate(k1s):
            off = pl.multiple_of(k1 * FFT_N2, FFT_N2)
            hr_ref[pl.ds(off, FFT_N2), :] = xx[:FFT_N2, u * ch:(u + 1) * ch]
            hi_ref[pl.ds(off, FFT_N2), :] = xx[FFT_N2:, u * ch:(u + 1) * ch]
        return c
    lax.fori_loop(0, n1 // FFT_INNER_UNROLL, body, 0)


def filter_spectrum(kern):
    n, c = kern.shape
    n1 = n // FFT_N2
    f1, fblk, _, _ = _fft_consts(n, n1, 1)
    f1 = jnp.asarray(f1, bf16)
    fblk = jnp.asarray(fblk, bf16)
    sa = n1 + STRIDE_PAD
    return pl.pallas_call(
        functools.partial(_filter_fft_kernel, n=n),
        grid=(c // HY_CH,),
        in_specs=[pl.BlockSpec((n, HY_CH), lambda j: (0, j)),
                  pl.BlockSpec(f1.shape, lambda j: (0, 0, 0)),
                  pl.BlockSpec(fblk.shape, lambda j: (0, 0))],
        out_specs=[pl.BlockSpec((n, HY_CH), lambda j: (0, j))] * 2,
        out_shape=[jax.ShapeDtypeStruct((n, c), f32)] * 2,
        scratch_shapes=[pltpu.VMEM((FFT_N2 * sa, HY_CH), f32)] * 2,
        compiler_params=pltpu.CompilerParams(dimension_semantics=("arbitrary",), vmem_limit_bytes=VMEM_LIMIT),
        name="filter_spectrum",
    )(kern, f1, fblk)


def _conv3_rows(src_ref, w_ref, pad_ref, dst_ref, l):
    zeros = jnp.zeros((8, src_ref.shape[-1]), f32)
    pad_ref[0:8, :] = zeros
    pad_ref[l + 8:l + 16, :] = zeros

    def cp(i, c):
        r = pl.multiple_of(i * ROW_CHUNK, ROW_CHUNK)
        pad_ref[pl.ds(8 + r, ROW_CHUNK), :] = src_ref[0, pl.ds(r, ROW_CHUNK), :].astype(f32)
        return c
    lax.fori_loop(0, l // ROW_CHUNK, cp, 0)
    w = w_ref[...].astype(f32)

    def cv(i, c):
        r = pl.multiple_of(i * ROW_CHUNK, ROW_CHUNK)
        blk = pad_ref[pl.ds(r, ROW_CHUNK + 16), :]
        dst_ref[pl.ds(r, ROW_CHUNK), :] = (w[0:1] * blk[7:7 + ROW_CHUNK] + w[1:2] * blk[8:8 + ROW_CHUNK]
                                           + w[2:3] * blk[9:9 + ROW_CHUNK])
        return c
    lax.fori_loop(0, l // ROW_CHUNK, cv, 0)


def _hy_order_kernel(yin_ref, graw_ref, z_ref, cwy_ref, cwg_ref, skip_ref, hr_ref, hi_ref, f1_ref, fblk_ref, fiblk_ref,
                     hinv_ref, o_ref, pad_ref, y_ref, gate_ref, c_ref, ar_ref, ai_ref, br_ref, bi_ref,
                     *, l, first, last):
    n = 2 * l
    n1 = n // FFT_N2
    k1n = l // FFT_N2
    sa = n1 + STRIDE_PAD
    sb = FFT_N2 + STRIDE_PAD
    if first:
        _conv3_rows(yin_ref, cwy_ref, pad_ref, y_ref, l)
        src_ref = y_ref
    else:
        src_ref = yin_ref.at[0]
    _conv3_rows(graw_ref, cwg_ref, pad_ref, gate_ref, l)

    _fft_fwd_stage1(src_ref, f1_ref, ar_ref, ai_ref, k1n, n1, sa, bf16, None)

    ch = c_ref.shape[-1]

    def mid(it, c):
        groups = [[(it * FFT_INNER_GROUPS + gi) * FFT_INNER_UNROLL + u for u in range(FFT_INNER_UNROLL)]
                  for gi in range(FFT_INNER_GROUPS)]
        zs = [_fft_inner_load(ar_ref, ai_ref, k1s, sa, bf16) for k1s in groups]
        xxs = [jnp.dot(fblk_ref[...], z, preferred_element_type=f32) for z in zs]
        yys = []
        for k1s, xx in zip(groups, xxs):
            xr, xi = xx[:FFT_N2], xx[FFT_N2:]
            hr = jnp.concatenate([hr_ref[pl.ds(pl.multiple_of(k1 * FFT_N2, FFT_N2), FFT_N2), :] for k1 in k1s], axis=1)
            hi = jnp.concatenate([hi_ref[pl.ds(pl.multiple_of(k1 * FFT_N2, FFT_N2), FFT_N2), :] for k1 in k1s], axis=1)
            yys.append(jnp.concatenate([xr * hr - xi * hi, xr * hi + xi * hr], axis=0).astype(bf16))
        bbs = [jnp.dot(fiblk_ref[...], yy, preferred_element_type=f32) for yy in yys]
        for k1s, bb in zip(groups, bbs):
            for u, k1 in enumerate(k1s):
                boff = pl.multiple_of(k1 * sb, 8)
                br_ref[pl.ds(boff, FFT_N2), :] = bb[:FFT_N2, u * ch:(u + 1) * ch]
                bi_ref[pl.ds(boff, FFT_N2), :] = bb[FFT_N2:, u * ch:(u + 1) * ch]
        return c
    lax.fori_loop(0, n1 // (FFT_INNER_UNROLL * FFT_INNER_GROUPS), mid, 0)

    def inv2(it, c):
        n2s = [it * FFT_OUTER_UNROLL + u for u in range(FFT_OUTER_UNROLL)]
        bbs = [jnp.concatenate([br_ref[pl.ds(n2, n1, stride=sb), :], bi_ref[pl.ds(n2, n1, stride=sb), :]],
                               axis=0).astype(bf16) for n2 in n2s]
        res = [jnp.dot(hinv_ref[n2], bb, preferred_element_type=f32) for n2, bb in zip(n2s, bbs)]
        for n2, o in zip(n2s, res):
            c_ref[pl.ds(n2, k1n, stride=FFT_N2), :] = o
        return c
    lax.fori_loop(0, FFT_N2 // FFT_OUTER_UNROLL, inv2, 0)

    skip = skip_ref[0].astype(f32)

    def fin(i, c):
        r = pl.multiple_of(i * ROW_CHUNK, ROW_CHUNK)
        y = src_ref[pl.ds(r, ROW_CHUNK), :]
        y = gate_ref[pl.ds(r, ROW_CHUNK), :] * (c_ref[pl.ds(r, ROW_CHUNK), :] + y * skip)
        if last:
            z = z_ref[0, pl.ds(r, ROW_CHUNK), :].astype(f32)
            y = y * (z * jax.nn.sigmoid(z))
        o_ref[0, pl.ds(r, ROW_CHUNK), :] = y.astype(o_ref.dtype)
        return c
    lax.fori_loop(0, l // ROW_CHUNK, fin, 0)


def hyena_order(yin, px, hr, hi, conv_w, skip, order, first, last):
    b, l, _ = px.shape
    n = 2 * l
    n1 = n // FFT_N2
    k1n = l // FFT_N2
    sa, sb = n1 + STRIDE_PAD, FFT_N2 + STRIDE_PAD
    nh = HY_WIDTH // HY_CH
    f1, fblk, fiblk, hinv = [jnp.asarray(a, bf16) for a in _fft_consts(n, k1n, k1n)]
    cb = lambda col: col // HY_CH
    c_v, c_g, c_z = cb(C_HY_VX), cb(C_HY_VX + (order + 1) * HY_WIDTH), cb(C_HY_Z)
    once = pl.Buffered(1)
    yin_spec = (pl.BlockSpec((1, l, HY_CH), lambda h, bi: (bi, 0, c_v + h)) if first
                else pl.BlockSpec((1, l, HY_CH), lambda h, bi: (bi, 0, h)))
    return pl.pallas_call(
        functools.partial(_hy_order_kernel, l=l, first=first, last=last),
        grid=(nh, b),
        in_specs=[yin_spec,
                  pl.BlockSpec((1, l, HY_CH), lambda h, bi: (bi, 0, c_g + h)),
                  pl.BlockSpec((1, l, HY_CH), lambda h, bi: (bi, 0, c_z + h)),
                  pl.BlockSpec((3, HY_CH), lambda h, bi: (0, h)),
                  pl.BlockSpec((3, HY_CH), lambda h, bi: (0, (order + 1) * nh + h)),
                  pl.BlockSpec((1, 1, HY_CH), lambda h, bi: (order, 0, h)),
                  pl.BlockSpec((n, HY_CH), lambda h, bi: (0, order * nh + h), pipeline_mode=once),
                  pl.BlockSpec((n, HY_CH), lambda h, bi: (0, order * nh + h), pipeline_mode=once),
                  pl.BlockSpec(f1.shape, lambda h, bi: (0, 0, 0), pipeline_mode=once),
                  pl.BlockSpec(fblk.shape, lambda h, bi: (0, 0), pipeline_mode=once),
                  pl.BlockSpec(fiblk.shape, lambda h, bi: (0, 0), pipeline_mode=once),
                  pl.BlockSpec(hinv.shape, lambda h, bi: (0, 0, 0), pipeline_mode=once)],
        out_specs=pl.BlockSpec((1, l, HY_CH), lambda h, bi: (bi, 0, h)),
        out_shape=jax.ShapeDtypeStruct((b, l, HY_WIDTH), bf16 if last else f32),
        scratch_shapes=[pltpu.VMEM((l + 16, HY_CH), f32), pltpu.VMEM((l, HY_CH), f32), pltpu.VMEM((l, HY_CH), f32),
                        pltpu.VMEM((l, HY_CH), f32),
                        pltpu.VMEM((FFT_N2 * sa, HY_CH), f32), pltpu.VMEM((FFT_N2 * sa, HY_CH), f32),
                        pltpu.VMEM((n1 * sb, HY_CH), f32), pltpu.VMEM((n1 * sb, HY_CH), f32)],
        compiler_params=pltpu.CompilerParams(dimension_semantics=("arbitrary", "arbitrary"),
                                             vmem_limit_bytes=56 * 1024 * 1024),
        name=f"hyena_order{order}",
    )(yin, px, px, conv_w, conv_w, skip.reshape(HY_ORDER, 1, HY_WIDTH), hr, hi, f1, fblk, fiblk, hinv)


def hyena_latent(px, conv_w, skip, w1, b1, w2, b2, w3, b3, freq, wout):
    l = px.shape[1]
    kern = hyena_filter_circular(l, w1, b1, w2, b2, w3, b3, freq, wout)
    hr, hi = filter_spectrum(kern)
    y1 = hyena_order(px, px, hr, hi, conv_w, skip, 0, True, False)
    return hyena_order(y1, px, hr, hi, conv_w, skip, 1, False, True)


@functools.lru_cache(maxsize=None)
def _dense_dft_consts(l):
    n = 2 * l
    k = np.arange(n)
    ang = -2.0 * np.pi * np.outer(k, np.arange(n)) / n
    fwd = np.concatenate([np.cos(ang), np.sin(ang)], axis=0)
    angi = 2.0 * np.pi * np.outer(np.arange(l), k) / n
    inv = np.concatenate([np.cos(angi), -np.sin(angi)], axis=1) / n
    return fwd, inv


def _hy_ctx_kernel(p_ref, kern_ref, cw_ref, skip_ref, fwd_ref, inv_ref, o_ref, pad_ref, t_ref, *, l):
    n = 2 * l
    dot = functools.partial(jnp.dot, preferred_element_type=f32, precision=HIGHEST)
    w = HY_WIDTH

    def conv3(col):
        pad_ref[0:8, :] = jnp.zeros((8, w), f32)
        pad_ref[l + 8:l + 16, :] = jnp.zeros((8, w), f32)
        pad_ref[8:8 + l, :] = p_ref[0, :, col:col + w].astype(f32)
        cw = cw_ref[:, col - C_HY_VX:col - C_HY_VX + w].astype(f32)
        return cw[0:1] * pad_ref[7:7 + l, :] + cw[1:2] * pad_ref[8:8 + l, :] + cw[2:3] * pad_ref[9:9 + l, :]

    y = conv3(C_HY_VX)
    for o in range(HY_ORDER):
        gate = conv3(C_HY_VX + (o + 1) * w)
        hh = dot(fwd_ref[...], kern_ref[:, o * w:(o + 1) * w])
        xx = dot(fwd_ref[:, :l], y)
        xr, xi, hr, hi = xx[:n], xx[n:], hh[:n], hh[n:]
        t_ref[0:n, :] = xr * hr - xi * hi
        t_ref[n:2 * n, :] = xr * hi + xi * hr
        conv = dot(inv_ref[...], t_ref[...])
        y = gate * (conv + y * skip_ref[o:o + 1, :].astype(f32))
    z = p_ref[0, :, C_HY_Z:C_HY_Z + w].astype(f32)
    o_ref[0] = (y * (z * jax.nn.sigmoid(z))).astype(o_ref.dtype)


def hyena_ctx(pc, conv_w, skip, w1, b1, w2, b2, w3, b3, freq, wout):
    b, lc, _ = pc.shape
    n = 2 * lc
    kern = hyena_filter_circular(lc, w1, b1, w2, b2, w3, b3, freq, wout)
    fwd, inv = [jnp.asarray(a, f32) for a in _dense_dft_consts(lc)]
    full = lambda a: pl.BlockSpec(a.shape, lambda bi: (0,) * a.ndim)
    return pl.pallas_call(
        functools.partial(_hy_ctx_kernel, l=lc),
        grid=(b,),
        in_specs=[pl.BlockSpec((1, lc, MAIN_W), lambda bi: (bi, 0, 0)), full(kern), full(conv_w), full(skip),
                  full(fwd), full(inv)],
        out_specs=pl.BlockSpec((1, lc, HY_WIDTH), lambda bi: (bi, 0, 0)),
        out_shape=jax.ShapeDtypeStruct((b, lc, HY_WIDTH), bf16),
        scratch_shapes=[pltpu.VMEM((lc + 16, HY_WIDTH), f32), pltpu.VMEM((2 * n, HY_WIDTH), f32)],
        compiler_params=pltpu.CompilerParams(dimension_semantics=("arbitrary",), vmem_limit_bytes=VMEM_LIMIT),
        name="hyena_ctx",
    )(pc, kern, conv_w, skip, fwd, inv)


DN_PREP_ROWS = 256
DN_SUB_ROWS = 64
DN_HALO = 16


@functools.lru_cache(maxsize=None)
def _shift_matrix():
    s = np.zeros((DN_CONV_W * DN_SUB_ROWS, DN_SUB_ROWS + 2 * DN_HALO), np.float32)
    for j in range(DN_CONV_W):
        for m in range(DN_SUB_ROWS):
            s[j * DN_SUB_ROWS + m, m + j + DN_HALO - DN_CONV_W // 2] = 1.0
    return s


@functools.lru_cache(maxsize=None)
def _rope_tables(l):
    half = DN_HEAD_DIM // 2
    nf = half // 2
    t = np.arange(l)
    inv = (np.float32(ROPE_BASE) ** (-np.arange(nf, dtype=np.float32) / nf)).astype(np.float32)
    ang_r = ((t // GRID_W).astype(np.float32)[:, None] * inv[None, :]).astype(np.float32)
    ang_c = ((t % GRID_W).astype(np.float32)[:, None] * inv[None, :]).astype(np.float32)
    cos = np.concatenate([np.cos(ang_r), np.cos(ang_r), np.cos(ang_c), np.cos(ang_c)], axis=1)
    sin = np.concatenate([-np.sin(ang_r), np.sin(ang_r), -np.sin(ang_c), np.sin(ang_c)], axis=1)
    return cos.astype(np.float32), sin.astype(np.float32)


def _dn_prep_kernel(*refs, rope, n_tiles):
    (q_ref, qp_ref, qn_ref, k_ref, kp_ref, kn_ref, v_ref, vp_ref, vn_ref, g_ref, cw_ref, sh_ref, cos_ref, sin_ref,
     al_ref, dt_ref, qo_ref, ko_ref, vo_ref, go_ref, pad_ref, taps_ref) = refs
    i = pl.program_id(1)
    t = DN_PREP_ROWS
    w = DN_WIDTH
    lane = lax.broadcasted_iota(jnp.int32, (DN_SUB_ROWS, w), 1)
    first_half = (lane % (DN_HEAD_DIM // 2)) < (DN_HEAD_DIM // 4)
    for idx, (m_ref, p_ref, n_ref, o_ref) in enumerate(((q_ref, qp_ref, qn_ref, qo_ref), (k_ref, kp_ref, kn_ref, ko_ref),
                                                        (v_ref, vp_ref, vn_ref, vo_ref))):
        pad_ref[0:DN_HALO, :] = jnp.where(i == 0, jnp.zeros_like(p_ref[0]), p_ref[0])
        pad_ref[DN_HALO:DN_HALO + t, :] = m_ref[0]
        pad_ref[DN_HALO + t:2 * DN_HALO + t, :] = jnp.where(i == n_tiles - 1, jnp.zeros_like(n_ref[0]), n_ref[0])
        cw = cw_ref[:, idx * w:(idx + 1) * w].astype(f32)
        for s in range(t // DN_SUB_ROWS):
            taps_ref[s] = jnp.dot(sh_ref[...], pad_ref[s * DN_SUB_ROWS:(s + 1) * DN_SUB_ROWS + 2 * DN_HALO, :],
                                  preferred_element_type=f32)

        def sub(s, c):
            r = s * DN_SUB_ROWS
            y = cw[0:1] * taps_ref[s, 0:DN_SUB_ROWS, :]
            for j in range(1, DN_CONV_W):
                y = y + cw[j:j + 1] * taps_ref[s, j * DN_SUB_ROWS:(j + 1) * DN_SUB_ROWS, :]
            y = y * jax.nn.sigmoid(y)
            if idx < 2:
                if rope:
                    cos = jnp.concatenate([cos_ref[pl.ds(r, DN_SUB_ROWS), :]] * DN_HEADS, axis=1)
                    sin = jnp.concatenate([sin_ref[pl.ds(r, DN_SUB_ROWS), :]] * DN_HEADS, axis=1)
                    q4 = DN_HEAD_DIM // 4
                    swapped = jnp.where(first_half, pltpu.roll(y, w - q4, 1), pltpu.roll(y, q4, 1))
                    y = y * cos + swapped * sin
                outs = []
                for h in range(DN_HEADS):
                    yh = y[:, h * DN_HEAD_DIM:(h + 1) * DN_HEAD_DIM]
                    nrm = lax.rsqrt(jnp.sum(yh * yh, axis=-1, keepdims=True) + NORM_EPS)
                    if idx == 0:
                        nrm = nrm * (DN_HEAD_DIM ** -0.5)
                    outs.append(yh * nrm)
                y = jnp.concatenate(outs, axis=1)
            o_ref[0, pl.ds(r, DN_SUB_ROWS), :] = y.astype(o_ref.dtype)
            return c
        for s in range(t // DN_SUB_ROWS):
            sub(s, 0)

    g = g_ref[0]
    glane = lax.broadcasted_iota(jnp.int32, g.shape, 1)
    xa = g + dt_ref[...]
    softplus = jnp.maximum(xa, 0.0) + jnp.log1p(jnp.exp(-jnp.abs(xa)))
    go_ref[0] = jnp.where(glane < 2 * DN_HEADS, jax.nn.sigmoid(g), -jnp.exp(al_ref[...]) * softplus)


def dn_prep(p, gates_raw, conv_w, a_log, dt_bias, rope):
    b, l, _ = p.shape
    t = DN_PREP_ROWS
    w = DN_WIDTH
    n_tiles = l // t
    hb = t // DN_HALO
    cblk = C_DN_QKV // w
    cos, sin = [jnp.asarray(a) for a in _rope_tables(l)]
    shift = jnp.asarray(_shift_matrix(), bf16)
    pad = jnp.zeros((2 * DN_HEADS,), f32)
    rest = jnp.zeros((GATE_W - 4 * DN_HEADS,), f32)
    al = jnp.concatenate([pad, a_log.reshape(-1), rest])[None]
    dt = jnp.concatenate([pad, dt_bias.reshape(-1), rest])[None]

    def slab(j):
        return [pl.BlockSpec((1, t, w), lambda bi, i: (bi, i, cblk + j)),
                pl.BlockSpec((1, DN_HALO, w), lambda bi, i: (bi, jnp.maximum(i * hb - 1, 0), cblk + j)),
                pl.BlockSpec((1, DN_HALO, w), lambda bi, i: (bi, jnp.minimum((i + 1) * hb, l // DN_HALO - 1), cblk + j))]

    row = lambda a: pl.BlockSpec(a.shape, lambda bi, i: (0, 0))
    tab = pl.BlockSpec((t, DN_HEAD_DIM), lambda bi, i: (i, 0))
    out_blk = pl.BlockSpec((1, t, w), lambda bi, i: (bi, i, 0))
    return pl.pallas_call(
        functools.partial(_dn_prep_kernel, rope=rope, n_tiles=n_tiles),
        grid=(b, n_tiles),
        in_specs=slab(0) + slab(1) + slab(2) + [
            pl.BlockSpec((1, t, GATE_W), lambda bi, i: (bi, i, 0)), row(conv_w), row(shift), tab, tab, row(al),
            row(dt)],
        out_specs=[out_blk, out_blk, out_blk, pl.BlockSpec((1, t, GATE_W), lambda bi, i: (bi, i, 0))],
        out_shape=[jax.ShapeDtypeStruct((b, l, w), bf16)] * 3 + [jax.ShapeDtypeStruct((b, l, GATE_W), f32)],
        scratch_shapes=[pltpu.VMEM((t + 2 * DN_HALO, w), bf16),
                        pltpu.VMEM((t // DN_SUB_ROWS, DN_CONV_W * DN_SUB_ROWS, w), f32)],
        compiler_params=pltpu.CompilerParams(dimension_semantics=("arbitrary", "arbitrary"),
                                             vmem_limit_bytes=VMEM_LIMIT),
        name="dn_prep",
    )(p, p, p, p, p, p, p, p, p, gates_raw, conv_w, shift, cos, sin, al, dt)


DN_GROUP = 4
DN_GROUP_ROWS = DN_GROUP * DN_CHUNK
DN_BATCH_BLOCK = 2
DN_UNROLL = 2


def _bdot(a, b):
    return jnp.dot(a.astype(bf16), b.astype(bf16), preferred_element_type=f32)


def _dn_chunk_kernel(qc_ref, kc_ref, vc_ref, gc_ref, qx_ref, kx_ref, vx_ref, gx_ref, u_ref, wq_ref, ak_ref, eg_ref):
    g = pl.program_id(1)
    c = DN_CHUNK
    dh = DN_HEAD_DIM
    is_ctx = g == 0
    ri = lax.broadcasted_iota(jnp.int32, (c, c), 0)
    ci = lax.broadcasted_iota(jnp.int32, (c, c), 1)
    eye = (ri == ci).astype(f32)
    er = lax.broadcasted_iota(jnp.int32, (GATE_W, GATE_W), 0)
    ec = lax.broadcasted_iota(jnp.int32, (GATE_W, GATE_W), 1)
    eye_b = (er == ec).astype(bf16)
    tri_b = (ri >= ci).astype(bf16)
    masks = [((ri <= ci) if d else (ri >= ci), (ri < ci) if d else (ri > ci)) for d in range(2)]

    def sel_dot(sel, x, dims):
        hi = x.astype(bf16)
        mid = (x - hi.astype(f32)).astype(bf16)
        lo = (x - hi.astype(f32) - mid.astype(f32)).astype(bf16)
        return sum(lax.dot_general(sel, t, dims, preferred_element_type=f32) for t in (hi, mid, lo))

    def body(it, carry):
        units = []
        for jj in range(DN_UNROLL):
            j = it * DN_UNROLL + jj
            r = pl.multiple_of(j * c, c)
            pick = lambda a_ref, b_ref: jnp.where(is_ctx, a_ref[0, pl.ds(r, c), :], b_ref[0, pl.ds(r, c), :])
            q, k, v, gates = pick(qc_ref, qx_ref), pick(kc_ref, kx_ref), pick(vc_ref, vx_ref), pick(gc_ref, gx_ref)
            gtot = jnp.sum(gates, axis=0, keepdims=True)
            gfwd = sel_dot(tri_b, gates, (((1,), (0,)), ((), ())))
            gcs = (gfwd, gtot - gfwd + gates)
            g_t = sel_dot(eye_b, jnp.concatenate([gfwd, gates], axis=0), NT_DIMS)
            gcs_t = (g_t[:, :c], jnp.sum(g_t[:, c:], axis=1, keepdims=True) - g_t[:, :c] + g_t[:, c:])
            qks = [lax.dot_general(q[:, h * dh:(h + 1) * dh], k[:, h * dh:(h + 1) * dh], NT_DIMS,
                                   preferred_element_type=f32) for h in range(DN_HEADS)]
            kks = [lax.dot_general(k[:, h * dh:(h + 1) * dh], k[:, h * dh:(h + 1) * dh], NT_DIMS,
                                   preferred_element_type=f32) for h in range(DN_HEADS)]
            for d in range(2):
                incl, strict = masks[d]
                egs = []
                for h in range(DN_HEADS):
                    hs = slice(h * dh, (h + 1) * dh)
                    col = d * DN_HEADS + h
                    gl = 2 * DN_HEADS + col
                    kh, vh, qh = k[:, hs].astype(f32), v[:, hs].astype(f32), q[:, hs].astype(f32)
                    beta = gates[:, col:col + 1]
                    gcol = gcs[d][:, gl:gl + 1]
                    glast = gtot[:, gl:gl + 1]
                    decay = jnp.where(incl, jnp.exp(jnp.minimum(gcol - gcs_t[d][gl:gl + 1, :], 0.0)), 0.0)
                    eg = jnp.exp(gcol)
                    wq_ref[d, 0, j, h, c:2 * c, :] = (qh * eg).astype(bf16)
                    ak_ref[d, 0, j, h, 0:c, :] = jnp.where(incl, qks[h] * decay, 0.0).astype(bf16)
                    ak_ref[d, 0, j, h, c:c + dh, :] = jnp.transpose(kh * jnp.exp(glast - gcol)).astype(bf16)
                    egs.append(jnp.broadcast_to(jnp.exp(glast), (1, GATE_W)))
                    kb = kh * beta
                    units.append(dict(j=j, d=d, h=h, n=-jnp.where(strict, beta * kks[h] * decay, 0.0),
                                      rhs=jnp.concatenate([vh * beta, kb * eg], axis=1).astype(bf16)))
                eg_ref[d, 0, j] = jnp.concatenate(egs + [jnp.zeros((8 - DN_HEADS, GATE_W), f32)], axis=0)
        for un in units:
            un["t"] = eye + un["n"]
            un["p"] = _bdot(un["n"], un["n"])
        for _ in range(int(math.log2(c)) - 2):
            for un in units:
                both = _bdot(jnp.concatenate([un["p"], un["t"]], axis=0), un["p"])
                un["p"], un["t"] = both[:c], un["t"] + both[c:]
        for un in units:
            un["t"] = un["t"] + _bdot(un["t"], un["p"])
        for un in units:
            sol = _bdot(un["t"], un["rhs"])
            u_ref[un["d"], 0, un["j"], un["h"]] = sol[:, :dh].astype(bf16)
            wq_ref[un["d"], 0, un["j"], un["h"], 0:c, :] = sol[:, dh:].astype(bf16)
        return carry
    lax.fori_loop(0, DN_GROUP // DN_UNROLL, body, 0)


def dn_chunks(qc, kc, vc, gc, qx, kx, vx, gx):
    b, lc, w = qc.shape
    l = qx.shape[1]
    assert lc == DN_GROUP_ROWS and l % DN_GROUP_ROWS == 0
    ng = 1 + l // DN_GROUP_ROWS
    nch = ng * DN_GROUP
    c, dh, nhd = DN_CHUNK, DN_HEAD_DIM, DN_HEADS
    cmap = lambda bi, g: (bi, 0, 0)
    xmap = lambda bi, g: (bi, jnp.maximum(g - 1, 0), 0)
    blk = lambda width, m: pl.BlockSpec((1, DN_GROUP_ROWS, width), m)
    omap = lambda bi, g: (0, bi, g, 0, 0, 0)
    return pl.pallas_call(
        _dn_chunk_kernel,
        grid=(b, ng),
        in_specs=[blk(w, cmap), blk(w, cmap), blk(w, cmap), blk(GATE_W, cmap),
                  blk(w, xmap), blk(w, xmap), blk(w, xmap), blk(GATE_W, xmap)],
        out_specs=[pl.BlockSpec((2, 1, DN_GROUP, nhd, c, dh), omap),
                   pl.BlockSpec((2, 1, DN_GROUP, nhd, 2 * c, dh), omap),
                   pl.BlockSpec((2, 1, DN_GROUP, nhd, c + dh, c), omap),
                   pl.BlockSpec((2, 1, DN_GROUP, 8, GATE_W), lambda bi, g: (0, bi, g, 0, 0))],
        out_shape=[jax.ShapeDtypeStruct((2, b, nch, nhd, c, dh), bf16),
                   jax.ShapeDtypeStruct((2, b, nch, nhd, 2 * c, dh), bf16),
                   jax.ShapeDtypeStruct((2, b, nch, nhd, c + dh, c), bf16),
                   jax.ShapeDtypeStruct((2, b, nch, 8, GATE_W), f32)],
        compiler_params=pltpu.CompilerParams(dimension_semantics=("arbitrary", "arbitrary"),
                                             vmem_limit_bytes=VMEM_LIMIT),
        name="dn_chunks",
    )(qc, kc, vc, gc, qx, kx, vx, gx)


def _bwd_group(s, ng):
    return jnp.where(s == 0, 0, ng - s)


def _dn_serial_kernel(uf_ref, wqf_ref, akf_ref, egf_ref, ub_ref, wqb_ref, akb_ref, egb_ref, of_ref, ob_ref, s_ref):
    c = DN_CHUNK
    dh = DN_HEAD_DIM

    @pl.when(pl.program_id(1) == 0)
    def _():
        s_ref[...] = jnp.zeros(s_ref.shape, f32)

    dirs = ((uf_ref, wqf_ref, akf_ref, egf_ref, of_ref), (ub_ref, wqb_ref, akb_ref, egb_ref, ob_ref))
    units = [(e, d, h) for e in range(DN_BATCH_BLOCK) for d in range(2) for h in range(DN_HEADS)]
    states = {un: s_ref[un] for un in units}
    for jj in range(DN_GROUP):
        js = (jj, DN_GROUP - 1 - jj)
        r1 = {(e, d, h): jnp.dot(dirs[d][1][0, e, js[d], h], states[e, d, h].astype(bf16), preferred_element_type=f32)
              for e, d, h in units}
        r2 = {}
        for e, d, h in units:
            v_new = dirs[d][0][0, e, js[d], h].astype(f32) - r1[e, d, h][:c]
            r2[e, d, h] = jnp.dot(dirs[d][2][0, e, js[d], h], v_new.astype(bf16), preferred_element_type=f32)
        for e, d, h in units:
            j = js[d]
            dirs[d][4][e, j * c:(j + 1) * c, h * dh:(h + 1) * dh] = (r1[e, d, h][c:] + r2[e, d, h][:c]).astype(bf16)
            states[e, d, h] = states[e, d, h] * dirs[d][3][0, e, j, h:h + 1, :] + r2[e, d, h][c:]
    for un in units:
        s_ref[un] = states[un]


def dn_serial(u, wq, ak, eg):
    _, b, nch, nhd, c, dh = u.shape
    ng = nch // DN_GROUP
    w = nhd * dh
    fmap6 = lambda bi, s: (0, bi, s, 0, 0, 0)
    bmap6 = lambda bi, s: (1, bi, _bwd_group(s, ng), 0, 0, 0)
    fmap5 = lambda bi, s: (0, bi, s, 0, 0)
    bmap5 = lambda bi, s: (1, bi, _bwd_group(s, ng), 0, 0)
    bb = DN_BATCH_BLOCK
    assert b % bb == 0
    blk6 = lambda a, m: pl.BlockSpec((1, bb, DN_GROUP) + a.shape[3:], m)
    egblk = lambda m: pl.BlockSpec((1, bb, DN_GROUP, 8, GATE_W), m)
    return pl.pallas_call(
        _dn_serial_kernel,
        grid=(b // bb, ng),
        in_specs=[blk6(u, fmap6), blk6(wq, fmap6), blk6(ak, fmap6), egblk(fmap5),
                  blk6(u, bmap6), blk6(wq, bmap6), blk6(ak, bmap6), egblk(bmap5)],
        out_specs=[pl.BlockSpec((bb, DN_GROUP_ROWS, w), lambda bi, s: (bi, s, 0)),
                   pl.BlockSpec((bb, DN_GROUP_ROWS, w), lambda bi, s: (bi, _bwd_group(s, ng), 0))],
        out_shape=[jax.ShapeDtypeStruct((b, nch * c, w), bf16)] * 2,
        scratch_shapes=[pltpu.VMEM((bb, 2, nhd, dh, dh), f32)],
        compiler_params=pltpu.CompilerParams(dimension_semantics=("arbitrary", "arbitrary"),
                                             vmem_limit_bytes=VMEM_LIMIT),
        name="dn_serial",
    )(u, wq, ak, eg, u, wq, ak, eg)


def _dn_out_kernel(of_ref, ob_ref, z_ref, nw_ref, y_ref):
    o = of_ref[0].astype(f32) + ob_ref[0].astype(f32)
    z = z_ref[0].astype(f32)
    nw = nw_ref[...].astype(f32)
    outs = []
    for h in range(DN_HEADS):
        oh = o[:, h * DN_HEAD_DIM:(h + 1) * DN_HEAD_DIM]
        outs.append(oh * lax.rsqrt(jnp.mean(oh * oh, axis=-1, keepdims=True) + NORM_EPS) * nw)
    y_ref[0] = (jnp.concatenate(outs, axis=1) * (z * jax.nn.sigmoid(z))).astype(y_ref.dtype)


def dn_output(o_f, o_b, p, norm_w, first_block):
    b, l, _ = p.shape
    w = DN_WIDTH
    tm = DN_GROUP_ROWS
    omap = lambda bi, i: (bi, first_block + i, 0)
    return pl.pallas_call(
        _dn_out_kernel,
        grid=(b, l // tm),
        in_specs=[pl.BlockSpec((1, tm, w), omap), pl.BlockSpec((1, tm, w), omap),
                  pl.BlockSpec((1, tm, w), lambda bi, i: (bi, i, C_DN_Z // w)),
                  pl.BlockSpec((1, DN_HEAD_DIM), lambda bi, i: (0, 0))],
        out_specs=pl.BlockSpec((1, tm, w), lambda bi, i: (bi, i, 0)),
        out_shape=jax.ShapeDtypeStruct((b, l, w), bf16),
        compiler_params=pltpu.CompilerParams(dimension_semantics=("arbitrary", "arbitrary"),
                                             vmem_limit_bytes=VMEM_LIMIT),
        name="dn_output",
    )(o_f, o_b, p, norm_w.reshape(1, DN_HEAD_DIM))


def _pack_w_in(w_in):
    depth, d, _ = w_in.shape
    main = jnp.concatenate([w_in[:, :, :3072], w_in[:, :, 3088:4112]], axis=-1).astype(bf16)
    gate = jnp.concatenate([w_in[:, :, 3072:3088], jnp.zeros((depth, d, GATE_W - 16), w_in.dtype)],
                           axis=-1).astype(bf16)
    return main, gate


def kernel(x, c, ctx, c_ctx, w_ada, b_ada, g_pre, g_post, w_in, w_out, na_rpb, dn_conv, dn_a_log, dn_dt_bias,
           dn_norm, hy_conv, hy_w1, hy_b1, hy_w2, hy_b2, hy_w3, hy_b3, hy_freq, hy_wout, hy_skip):
    bsz, l, d = x.shape
    lc = ctx.shape[1]
    depth = w_in.shape[0]
    cond = jnp.concatenate([c, c_ctx[None], jnp.zeros((7, d), f32)], axis=0)
    mod = modulation_all(cond, w_ada, b_ada)
    w_main, w_gate = _pack_w_in(w_in)
    w_out_b = w_out.astype(bf16)

    for i in range(depth):
        last = i == depth - 1
        shift_x, scale_x, gate_x = [mod[i, :bsz, j * d:(j + 1) * d][:, None] for j in range(3)]
        shift_c, scale_c, gate_c = [mod[i, bsz:bsz + 1, j * d:(j + 1) * d][:, None] for j in range(3)]
        px, gx = in_projection(x, g_pre[i], scale_x, shift_x, w_main[i], w_gate[i], 512)
        pc, gc = in_projection(ctx, g_pre[i], scale_c, shift_c, w_main[i], w_gate[i], 256)

        hy_args = (hy_conv[i], hy_skip[i], hy_w1[i], hy_b1[i], hy_w2[i], hy_b2[i], hy_w3[i], hy_b3[i], hy_freq[i],
                   hy_wout[i])
        dn_args = (dn_conv[i], dn_a_log[i], dn_dt_bias[i])

        out_a_x = na_attention(px, pc, na_rpb[i])

        dn_c = dn_prep(pc, gc, *dn_args, False)
        dn_x = dn_prep(px, gx, *dn_args, True)
        do_f, do_b = dn_serial(*dn_chunks(*dn_c, *dn_x))
        out_b_x = dn_output(do_f, do_b, px, dn_norm[i], lc // DN_GROUP_ROWS)

        out_c_x = hyena_latent(px, *hy_args)

        new_x = out_projection(out_a_x, out_b_x, out_c_x, w_out_b[i], g_post[i], gate_x, x, 512)

        if not last:
            out_a_c = ctx_attention(pc)
            out_b_c = dn_output(do_f, do_b, pc, dn_norm[i], 0)
            out_c_c = hyena_ctx(pc, *hy_args)
            ctx = out_projection(out_a_c, out_b_c, out_c_c, w_out_b[i], g_post[i], gate_c, ctx, lc)
        x = new_x
    return x
```

```python
import functools
import math

import numpy as np
import jax
import jax.numpy as jnp
from jax import lax
from jax.experimental import pallas as pl
from jax.experimental.pallas import tpu as pltpu

D_MODEL = 1024
GRID_W = 64
NA_HEAD_DIM = 64
NA_WIDTH = 256
NA_HEADS = 4
NA_WIN_ROWS = 8
NA_WIN_COLS = 16
DN_HEAD_DIM = 128
DN_WIDTH = 512
DN_HEADS = 4
DN_CONV_W = 5
DN_CHUNK = 64
HY_WIDTH = 256
HY_ORDER = 2
HY_EMB = 33
HY_DECAY_TARGET = 1e-2
HY_FAST_DECAY = 0.3
HY_SLOW_DECAY = 1.5
ROPE_BASE = 10000.0
NORM_EPS = 1e-6

C_NA_QKV, C_NA_Z, C_DN_QKV, C_DN_Z, C_HY_VX, C_HY_Z = 0, 768, 1024, 2560, 3072, 3840
MAIN_W = 4096
GATE_W = 128
LANE = 128
VMEM_LIMIT = 52 * 1024 * 1024

f32 = jnp.float32
bf16 = jnp.bfloat16
HIGHEST = lax.Precision.HIGHEST
NT_DIMS = (((1,), (1,)), ((), ()))
TN_DIMS = (((0,), (0,)), ((), ()))


def _mod_kernel(c_ref, w_ref, b_ref, o_ref):
    c = c_ref[...]
    a = c * jax.nn.sigmoid(c)
    o_ref[0] = jnp.dot(a, w_ref[0], preferred_element_type=f32, precision=lax.Precision.HIGHEST) + b_ref[0]


def modulation_all(cond, w_ada, b_ada):
    depth, d, d3 = w_ada.shape
    r = cond.shape[0]
    tn = 512
    return pl.pallas_call(
        _mod_kernel,
        grid=(depth, d3 // tn),
        in_specs=[pl.BlockSpec((r, d), lambda i, j: (0, 0)),
                  pl.BlockSpec((1, d, tn), lambda i, j: (i, 0, j)),
                  pl.BlockSpec((1, 1, tn), lambda i, j: (i, 0, j))],
        out_specs=pl.BlockSpec((1, r, tn), lambda i, j: (i, 0, j)),
        out_shape=jax.ShapeDtypeStruct((depth, r, d3), f32),
        compiler_params=pltpu.CompilerParams(dimension_semantics=("arbitrary", "arbitrary"),
                                             vmem_limit_bytes=VMEM_LIMIT),
        name="modulation",
    )(cond, w_ada, b_ada.reshape(depth, 1, d3))


def _inproj_kernel(x_ref, g_ref, sc_ref, sh_ref, w_ref, wg_ref, o_ref, og_ref):
    x = x_ref[0]
    ms = jnp.mean(x * x, axis=-1, keepdims=True)
    h = x * lax.rsqrt(ms + NORM_EPS) * g_ref[...]
    h = (h * (1.0 + sc_ref[0]) + sh_ref[0]).astype(bf16)
    for j in range(MAIN_W // 1024):
        o_ref[0, :, j * 1024:(j + 1) * 1024] = jnp.dot(
            h, w_ref[:, j * 1024:(j + 1) * 1024], preferred_element_type=f32).astype(bf16)
    og_ref[0] = jnp.dot(h, wg_ref[...], preferred_element_type=f32)


def in_projection(x, g, scale, shift, w_main, w_gate, tm):
    b, l, d = x.shape
    per_b = scale.shape[0] == b
    mod_map = (lambda bi, i: (bi, 0, 0)) if per_b else (lambda bi, i: (0, 0, 0))
    return pl.pallas_call(
        _inproj_kernel,
        grid=(b, l // tm),
        in_specs=[pl.BlockSpec((1, tm, d), lambda bi, i: (bi, i, 0)),
                  pl.BlockSpec((1, d), lambda bi, i: (0, 0)),
                  pl.BlockSpec((1, 1, d), mod_map),
                  pl.BlockSpec((1, 1, d), mod_map),
                  pl.BlockSpec((d, MAIN_W), lambda bi, i: (0, 0)),
                  pl.BlockSpec((d, GATE_W), lambda bi, i: (0, 0))],
        out_specs=[pl.BlockSpec((1, tm, MAIN_W), lambda bi, i: (bi, i, 0)),
                   pl.BlockSpec((1, tm, GATE_W), lambda bi, i: (bi, i, 0))],
        out_shape=[jax.ShapeDtypeStruct((b, l, MAIN_W), bf16),
                   jax.ShapeDtypeStruct((b, l, GATE_W), f32)],
        compiler_params=pltpu.CompilerParams(dimension_semantics=("arbitrary", "arbitrary"),
                                             vmem_limit_bytes=VMEM_LIMIT),
        name="in_projection",
    )(x, g.reshape(1, d), scale, shift, w_main, w_gate)


def _outproj_kernel(a_ref, b_ref, c_ref, w_ref, g_ref, gate_ref, x_ref, o_ref):
    y = (jnp.dot(a_ref[0], w_ref[0:NA_WIDTH], preferred_element_type=f32)
         + jnp.dot(b_ref[0], w_ref[NA_WIDTH:NA_WIDTH + DN_WIDTH], preferred_element_type=f32)
         + jnp.dot(c_ref[0], w_ref[NA_WIDTH + DN_WIDTH:], preferred_element_type=f32))
    ms = jnp.mean(y * y, axis=-1, keepdims=True)
    yn = y * lax.rsqrt(ms + NORM_EPS) * g_ref[...]
    o_ref[0] = x_ref[0] + gate_ref[0] * yn


def out_projection(out_a, out_b, out_c, w_out, g_post, gate, x, tm):
    b, l, d = x.shape
    per_b = gate.shape[0] == b
    mod_map = (lambda bi, i: (bi, 0, 0)) if per_b else (lambda bi, i: (0, 0, 0))
    part = lambda a: pl.BlockSpec((1, tm, a.shape[-1]), lambda bi, i: (bi, i, 0))
    return pl.pallas_call(
        _outproj_kernel,
        grid=(b, l // tm),
        in_specs=[part(out_a), part(out_b), part(out_c),
                  pl.BlockSpec((d, d), lambda bi, i: (0, 0)),
                  pl.BlockSpec((1, d), lambda bi, i: (0, 0)),
                  pl.BlockSpec((1, 1, d), mod_map),
                  pl.BlockSpec((1, tm, d), lambda bi, i: (bi, i, 0))],
        out_specs=pl.BlockSpec((1, tm, d), lambda bi, i: (bi, i, 0)),
        out_shape=jax.ShapeDtypeStruct((b, l, d), f32),
        compiler_params=pltpu.CompilerParams(dimension_semantics=("arbitrary", "arbitrary"),
                                             vmem_limit_bytes=VMEM_LIMIT),
        name="out_projection",
    )(out_a, out_b, out_c, w_out, g_post.reshape(1, d), gate, x)


NA_TQ_ROWS = 4
NA_TQ = NA_TQ_ROWS * GRID_W
NA_KV_ROWS = NA_TQ_ROWS + NA_WIN_ROWS
NA_KV = NA_KV_ROWS * GRID_W
MASK_VALUE = -1e30


def _na_window_start(t, rows):
    return jnp.clip(t * NA_TQ_ROWS - NA_WIN_ROWS // 2, 0, rows - NA_KV_ROWS)


def _na_bias_kernel(rpb_ref, sel_ref, mask_ref, o_ref):
    o_ref[...] = jnp.dot(rpb_ref[...], sel_ref[...], preferred_element_type=f32, precision=HIGHEST) + mask_ref[...]


def na_bias_tables(rpb, rows):
    nh, ndr, ndc = rpb.shape
    col = np.arange(GRID_W)
    dc = np.clip(col[None, :] - col[:, None], -(NA_WIN_COLS - 1), NA_WIN_COLS - 1) + (NA_WIN_COLS - 1)
    c0 = np.clip(col - NA_WIN_COLS // 2, 0, GRID_W - NA_WIN_COLS)
    in_win = (col[None, :] >= c0[:, None]) & (col[None, :] < c0[:, None] + NA_WIN_COLS)
    sel = np.zeros((LANE, GRID_W * GRID_W), np.float32)
    sel[dc.reshape(-1), np.arange(GRID_W * GRID_W)] = 1.0
    cmask = np.where(in_win, 0.0, MASK_VALUE).astype(np.float32).reshape(1, -1)
    rpb2 = jnp.pad(rpb.astype(f32).reshape(nh * ndr, ndc), ((0, 0), (0, LANE - ndc)))
    tab = pl.pallas_call(
        _na_bias_kernel,
        out_shape=jax.ShapeDtypeStruct((nh * ndr, GRID_W * GRID_W), f32),
        name="na_bias",
    )(rpb2, jnp.asarray(sel), jnp.asarray(cmask)).reshape(nh, ndr, GRID_W, GRID_W)
    masked = jnp.full((nh, GRID_W, GRID_W), MASK_VALUE, f32)
    n_tiles = rows // NA_TQ_ROWS
    tabs = []
    for t in (0, 1, n_tiles - 1):
        r0 = t * NA_TQ_ROWS
        ws = int(np.clip(r0 - NA_WIN_ROWS // 2, 0, rows - NA_KV_ROWS))
        blocks = []
        for qr in range(r0, r0 + NA_TQ_ROWS):
            band0 = int(np.clip(qr - NA_WIN_ROWS // 2, 0, rows - NA_WIN_ROWS))
            blocks.append(jnp.concatenate(
                [tab[:, kr - qr + NA_WIN_ROWS - 1] if band0 <= kr < band0 + NA_WIN_ROWS else masked
                 for kr in range(ws, ws + NA_KV_ROWS)], axis=-1))
        tabs.append(jnp.concatenate(blocks, axis=-2))
    return jnp.stack(tabs)


def _na_kernel(q_ref, k_ref, v_ref, z_ref, kc_ref, vc_ref, bias_ref, o_ref, *, rows):
    t = pl.program_id(1)
    ws = pl.multiple_of(_na_window_start(t, rows) * GRID_W, GRID_W)
    q = q_ref[0]
    kw = k_ref[0, pl.ds(ws, NA_KV), :]
    vw = v_ref[0, pl.ds(ws, NA_KV), :]
    kc = kc_ref[0]
    vc = vc_ref[0]
    hd = NA_HEAD_DIM
    low_w = lax.broadcasted_iota(jnp.int32, (NA_KV, LANE), 1) < hd
    low_c = lax.broadcasted_iota(jnp.int32, (kc.shape[0], LANE), 1) < hd
    low_q = lax.broadcasted_iota(jnp.int32, (NA_TQ, LANE), 1) < hd
    tiles = []
    for hp in range(NA_HEADS * hd // LANE):
        tile_w = vw[:, hp * LANE:(hp + 1) * LANE]
        tile_c = vc[:, hp * LANE:(hp + 1) * LANE]
        normed = []
        for par in range(LANE // hd):
            h = hp * (LANE // hd) + par
            sl = slice(h * hd, (h + 1) * hd)
            qh = q[:, sl] * (hd ** -0.5)
            s1 = lax.dot_general(qh, kw[:, sl], NT_DIMS, preferred_element_type=f32) + bias_ref[0, h]
            s2 = lax.dot_general(qh, kc[:, sl], NT_DIMS, preferred_element_type=f32)
            m = jnp.maximum(jnp.max(s1, axis=-1, keepdims=True), jnp.max(s2, axis=-1, keepdims=True))
            p1 = jnp.exp((s1 - m).astype(bf16))
            p2 = jnp.exp((s2 - m).astype(bf16))
            keep_w, keep_c = (low_w, low_c) if par == 0 else (~low_w, ~low_c)
            oa = (jnp.dot(p1, jnp.where(keep_w, tile_w, 1.0).astype(bf16), preferred_element_type=f32)
                  + jnp.dot(p2, jnp.where(keep_c, tile_c, 1.0).astype(bf16), preferred_element_type=f32))
            den = oa[:, hd:hd + 1] if par == 0 else oa[:, 0:1]
            normed.append(oa / den)
        tiles.append(jnp.where(low_q, normed[0], normed[1]))
    z = z_ref[0].astype(f32)
    o = jnp.concatenate(tiles, axis=-1) * (z * jax.nn.sigmoid(z))
    o_ref[0] = o.astype(o_ref.dtype)


def na_attention(px, pc, rpb):
    b, l, _ = px.shape
    lc = pc.shape[1]
    rows = l // GRID_W
    n_tiles = rows // NA_TQ_ROWS
    bias = na_bias_tables(rpb, rows)
    w = NA_WIDTH
    cq, ck, cv, cz = [(C_NA_QKV + j * w) // w for j in range(3)] + [C_NA_Z // w]

    def bias_map(bi, t):
        return (jnp.where(t == 0, 0, jnp.where(t == n_tiles - 1, 2, 1)), 0, 0, 0)

    return pl.pallas_call(
        functools.partial(_na_kernel, rows=rows),
        grid=(b, n_tiles),
        in_specs=[pl.BlockSpec((1, NA_TQ, w), lambda bi, t: (bi, t, cq)),
                  pl.BlockSpec((1, l, w), lambda bi, t: (bi, 0, ck)),
                  pl.BlockSpec((1, l, w), lambda bi, t: (bi, 0, cv)),
                  pl.BlockSpec((1, NA_TQ, w), lambda bi, t: (bi, t, cz)),
                  pl.BlockSpec((1, lc, w), lambda bi, t: (bi, 0, ck)),
                  pl.BlockSpec((1, lc, w), lambda bi, t: (bi, 0, cv)),
                  pl.BlockSpec((1, NA_HEADS, NA_TQ, NA_KV), bias_map)],
        out_specs=pl.BlockSpec((1, NA_TQ, w), lambda bi, t: (bi, t, 0)),
        out_shape=jax.ShapeDtypeStruct((b, l, w), bf16),
        compiler_params=pltpu.CompilerParams(dimension_semantics=("arbitrary", "arbitrary"),
                                             vmem_limit_bytes=VMEM_LIMIT),
        name="na_attention",
    )(px, px, px, px, pc, pc, bias)


def _ctx_attn_kernel(q_ref, k_ref, v_ref, z_ref, o_ref):
    q = q_ref[0]
    k = k_ref[0]
    v = v_ref[0]
    outs = []
    for h in range(NA_HEADS):
        sl = slice(h * NA_HEAD_DIM, (h + 1) * NA_HEAD_DIM)
        qh = q[:, sl] * (NA_HEAD_DIM ** -0.5)
        s = lax.dot_general(qh, k[:, sl], NT_DIMS, preferred_element_type=f32)
        p = jnp.exp(s - jnp.max(s, axis=-1, keepdims=True))
        den = jnp.sum(p, axis=-1, keepdims=True)
        outs.append(jnp.dot(p.astype(bf16), v[:, sl], preferred_element_type=f32) / den)
    z = z_ref[0].astype(f32)
    o_ref[0] = (jnp.concatenate(outs, axis=-1) * (z * jax.nn.sigmoid(z))).astype(o_ref.dtype)


def ctx_attention(pc):
    b, lc, _ = pc.shape
    w = NA_WIDTH
    cq, ck, cv, cz = [(C_NA_QKV + j * w) // w for j in range(3)] + [C_NA_Z // w]
    spec = lambda cidx: pl.BlockSpec((1, lc, w), lambda bi: (bi, 0, cidx))
    return pl.pallas_call(
        _ctx_attn_kernel,
        grid=(b,),
        in_specs=[spec(cq), spec(ck), spec(cv), spec(cz)],
        out_specs=pl.BlockSpec((1, lc, w), lambda bi: (bi, 0, 0)),
        out_shape=jax.ShapeDtypeStruct((b, lc, w), bf16),
        compiler_params=pltpu.CompilerParams(dimension_semantics=("arbitrary",), vmem_limit_bytes=VMEM_LIMIT),
        name="ctx_attention",
    )(pc, pc, pc, pc)


HY_CH = 128
FFT_N2 = 128
ROW_CHUNK = 256
STRIDE_PAD = 8
HY_FILT_ROWS = 512
HY_BANDS = (HY_EMB - 1) // 2


def _hy_filter_kernel(fr_ref, w1t_ref, w1c_ref, w1s_ref, b1_ref, w2_ref, b2_ref, w3_ref, b3_ref, freq_ref, wo_ref,
                      dl_ref, o_ref, *, l):
    i = pl.program_id(0)
    n = i * HY_FILT_ROWS + lax.broadcasted_iota(jnp.int32, (HY_FILT_ROWS, 1), 0)
    lag = jnp.where(n < l, n, 2 * l - n).astype(f32)
    t = lag * (1.0 / (l - 1))
    ang = (lag * (2.0 * math.pi / l)) * fr_ref[...]
    dot = functools.partial(jnp.dot, preferred_element_type=f32, precision=HIGHEST)
    pre = t * w1t_ref[...] + dot(jnp.cos(ang), w1c_ref[...]) - dot(jnp.sin(ang), w1s_ref[...]) + b1_ref[...]
    hid = jnp.sin(freq_ref[0:1] * pre)
    hid = jnp.sin(freq_ref[1:2] * (dot(hid, w2_ref[...]) + b2_ref[...]))
    hid = jnp.sin(freq_ref[2:3] * (dot(hid, w3_ref[...]) + b3_ref[...]))
    filt = dot(hid, wo_ref[...])
    window = jnp.exp(-t * dl_ref[...])
    live = jnp.where(n == l, 0.0, 1.0)
    fwd = n < l
    for o in range(HY_ORDER):
        a = filt[:, (2 * o) * HY_WIDTH:(2 * o + 1) * HY_WIDTH]
        b = filt[:, (2 * o + 1) * HY_WIDTH:(2 * o + 2) * HY_WIDTH]
        o_ref[:, o * HY_WIDTH:(o + 1) * HY_WIDTH] = jnp.where(fwd, a, b) * window * live


def hyena_filter_circular(l, w1, b1, w2, b2, w3, b3, freq, wout):
    n = 2 * l
    assert n % HY_FILT_ROWS == 0
    hid = w1.shape[1]
    fr = np.zeros((1, LANE), np.float32)
    fr[0, :HY_BANDS] = np.linspace(1e-4, HY_BANDS - 1, HY_BANDS, dtype=np.float32)
    padrows = lambda w: jnp.concatenate([w, jnp.zeros((LANE - w.shape[0], hid), f32)], axis=0)
    w1t, w1c, w1s = w1[0:1], padrows(w1[1:1 + HY_BANDS]), padrows(w1[1 + HY_BANDS:])
    max_decay = math.log(HY_DECAY_TARGET) / HY_FAST_DECAY
    min_decay = math.log(HY_DECAY_TARGET) / HY_SLOW_DECAY
    deltas = np.abs(np.linspace(min_decay, max_decay, HY_WIDTH, dtype=np.float32))[None]
    full = lambda a: pl.BlockSpec(a.shape, lambda i: (0,) * a.ndim)
    args = [jnp.asarray(fr), w1t, w1c, w1s, b1[None], w2, b2[None], w3, b3[None], freq, wout, jnp.asarray(deltas)]
    return pl.pallas_call(
        functools.partial(_hy_filter_kernel, l=l),
        grid=(n // HY_FILT_ROWS,),
        in_specs=[full(a) for a in args],
        out_specs=pl.BlockSpec((HY_FILT_ROWS, HY_ORDER * HY_WIDTH), lambda i: (i, 0)),
        out_shape=jax.ShapeDtypeStruct((n, HY_ORDER * HY_WIDTH), f32),
        compiler_params=pltpu.CompilerParams(dimension_semantics=("arbitrary",), vmem_limit_bytes=VMEM_LIMIT),
        name="hyena_filter",
    )(*args)


@functools.lru_cache(maxsize=None)
def _fft_consts(n, k_in, k_out):
    n1 = n // FFT_N2
    k1 = np.arange(n1)
    n2 = np.arange(FFT_N2)
    tt = FFT_N2 * np.arange(k_in)[None, None, :] + n2[:, None, None]
    ang = -2.0 * np.pi * (k1[None, :, None] * tt) / n
    f1 = np.concatenate([np.cos(ang), np.sin(ang)], axis=1)
    a2 = -2.0 * np.pi * np.outer(n2, n2) / FFT_N2
    cr, ci = np.cos(a2), np.sin(a2)
    fblk = np.block([[cr, -ci], [ci, cr]])
    fiblk = np.block([[cr, ci], [-ci, cr]])
    to = FFT_N2 * np.arange(k_out)[None, :, None] + n2[:, None, None]
    ango = 2.0 * np.pi * (k1[None, None, :] * to) / n
    hinv = np.concatenate([np.cos(ango), -np.sin(ango)], axis=2) / n
    return f1, fblk, fiblk, hinv


FFT_OUTER_UNROLL = 16
FFT_INNER_UNROLL = 2
FFT_INNER_GROUPS = 8


def _fft_fwd_stage1(x_ref, f1_ref, ar_ref, ai_ref, k_in, n1, sa, dt, prec):
    def body(it, c):
        n2s = [it * FFT_OUTER_UNROLL + u for u in range(FFT_OUTER_UNROLL)]
        xs = [x_ref[pl.ds(n2, k_in, stride=FFT_N2), :].astype(dt) for n2 in n2s]
        res = [jnp.dot(f1_ref[n2], x, preferred_element_type=f32, precision=prec) for n2, x in zip(n2s, xs)]
        for n2, a in zip(n2s, res):
            off = pl.multiple_of(n2 * sa, 8)
            ar_ref[pl.ds(off, n1), :] = a[:n1]
            ai_ref[pl.ds(off, n1), :] = a[n1:]
        return c
    lax.fori_loop(0, FFT_N2 // FFT_OUTER_UNROLL, body, 0)


def _fft_inner_load(ar_ref, ai_ref, k1s, sa, dt):
    zr = jnp.concatenate([ar_ref[pl.ds(k1, FFT_N2, stride=sa), :] for k1 in k1s], axis=1)
    zi = jnp.concatenate([ai_ref[pl.ds(k1, FFT_N2, stride=sa), :] for k1 in k1s], axis=1)
    return jnp.concatenate([zr, zi], axis=0).astype(dt)


def _filter_fft_kernel(k_ref, f1_ref, fblk_ref, hr_ref, hi_ref, ar_ref, ai_ref, *, n):
    n1 = n // FFT_N2
    sa = n1 + STRIDE_PAD
    ch = k_ref.shape[-1]
    _fft_fwd_stage1(k_ref, f1_ref, ar_ref, ai_ref, n1, n1, sa, bf16, None)

    def body(it, c):
        k1s = [it * FFT_INNER_UNROLL + u for u in range(FFT_INNER_UNROLL)]
        xx = jnp.dot(fblk_ref[...], _fft_inner_load(ar_ref, ai_ref, k1s, sa, bf16), preferred_element_type=f32)
        for u, k1 in enumerate(k1s):
            off = pl.multiple_of(k1 * FFT_N2, FFT_N2)
            hr_ref[pl.ds(off, FFT_N2), :] = xx[:FFT_N2, u * ch:(u + 1) * ch]
            hi_ref[pl.ds(off, FFT_N2), :] = xx[FFT_N2:, u * ch:(u + 1) * ch]
        return c
    lax.fori_loop(0, n1 // FFT_INNER_UNROLL, body, 0)


def filter_spectrum(kern):
    n, c = kern.shape
    n1 = n // FFT_N2
    f1, fblk, _, _ = _fft_consts(n, n1, 1)
    f1 = jnp.asarray(f1, bf16)
    fblk = jnp.asarray(fblk, bf16)
    sa = n1 + STRIDE_PAD
    return pl.pallas_call(
        functools.partial(_filter_fft_kernel, n=n),
        grid=(c // HY_CH,),
        in_specs=[pl.BlockSpec((n, HY_CH), lambda j: (0, j)),
                  pl.BlockSpec(f1.shape, lambda j: (0, 0, 0)),
                  pl.BlockSpec(fblk.shape, lambda j: (0, 0))],
        out_specs=[pl.BlockSpec((n, HY_CH), lambda j: (0, j))] * 2,
        out_shape=[jax.ShapeDtypeStruct((n, c), f32)] * 2,
        scratch_shapes=[pltpu.VMEM((FFT_N2 * sa, HY_CH), f32)] * 2,
        compiler_params=pltpu.CompilerParams(dimension_semantics=("arbitrary",), vmem_limit_bytes=VMEM_LIMIT),
        name="filter_spectrum",
    )(kern, f1, fblk)


def _conv3_rows(src_ref, w_ref, pad_ref, dst_ref, l):
    zeros = jnp.zeros((8, src_ref.shape[-1]), f32)
    pad_ref[0:8, :] = zeros
    pad_ref[l + 8:l + 16, :] = zeros

    def cp(i, c):
        r = pl.multiple_of(i * ROW_CHUNK, ROW_CHUNK)
        pad_ref[pl.ds(8 + r, ROW_CHUNK), :] = src_ref[0, pl.ds(r, ROW_CHUNK), :].astype(f32)
        return c
    lax.fori_loop(0, l // ROW_CHUNK, cp, 0)
    w = w_ref[...].astype(f32)

    def cv(i, c):
        r = pl.multiple_of(i * ROW_CHUNK, ROW_CHUNK)
        blk = pad_ref[pl.ds(r, ROW_CHUNK + 16), :]
        dst_ref[pl.ds(r, ROW_CHUNK), :] = (w[0:1] * blk[7:7 + ROW_CHUNK] + w[1:2] * blk[8:8 + ROW_CHUNK]
                                           + w[2:3] * blk[9:9 + ROW_CHUNK])
        return c
    lax.fori_loop(0, l // ROW_CHUNK, cv, 0)


def _hy_order_kernel(yin_ref, graw_ref, z_ref, cwy_ref, cwg_ref, skip_ref, hr_ref, hi_ref, f1_ref, fblk_ref, fiblk_ref,
                     hinv_ref, o_ref, pad_ref, y_ref, gate_ref, c_ref, ar_ref, ai_ref, br_ref, bi_ref,
                     *, l, first, last):
    n = 2 * l
    n1 = n // FFT_N2
    k1n = l // FFT_N2
    sa = n1 + STRIDE_PAD
    sb = FFT_N2 + STRIDE_PAD
    if first:
        _conv3_rows(yin_ref, cwy_ref, pad_ref, y_ref, l)
        src_ref = y_ref
    else:
        src_ref = yin_ref.at[0]
    _conv3_rows(graw_ref, cwg_ref, pad_ref, gate_ref, l)

    _fft_fwd_stage1(src_ref, f1_ref, ar_ref, ai_ref, k1n, n1, sa, bf16, None)

    ch = c_ref.shape[-1]

    def mid(it, c):
        groups = [[(it * FFT_INNER_GROUPS + gi) * FFT_INNER_UNROLL + u for u in range(FFT_INNER_UNROLL)]
                  for gi in range(FFT_INNER_GROUPS)]
        zs = [_fft_inner_load(ar_ref, ai_ref, k1s, sa, bf16) for k1s in groups]
        xxs = [jnp.dot(fblk_ref[...], z, preferred_element_type=f32) for z in zs]
        yys = []
        for k1s, xx in zip(groups, xxs):
            xr, xi = xx[:FFT_N2], xx[FFT_N2:]
            hr = jnp.concatenate([hr_ref[pl.ds(pl.multiple_of(k1 * FFT_N2, FFT_N2), FFT_N2), :] for k1 in k1s], axis=1)
            hi = jnp.concatenate([hi_ref[pl.ds(pl.multiple_of(k1 * FFT_N2, FFT_N2), FFT_N2), :] for k1 in k1s], axis=1)
            yys.append(jnp.concatenate([xr * hr - xi * hi, xr * hi + xi * hr], axis=0).astype(bf16))
        bbs = [jnp.dot(fiblk_ref[...], yy, preferred_element_type=f32) for yy in yys]
        for k1s, bb in zip(groups, bbs):
            for u, k1 in enumerate(k1s):
                boff = pl.multiple_of(k1 * sb, 8)
                br_ref[pl.ds(boff, FFT_N2), :] = bb[:FFT_N2, u * ch:(u + 1) * ch]
                bi_ref[pl.ds(boff, FFT_N2), :] = bb[FFT_N2:, u * ch:(u + 1) * ch]
        return c
    lax.fori_loop(0, n1 // (FFT_INNER_UNROLL * FFT_INNER_GROUPS), mid, 0)

    def inv2(it, c):
        n2s = [it * FFT_OUTER_UNROLL + u for u in range(FFT_OUTER_UNROLL)]
        bbs = [jnp.concatenate([br_ref[pl.ds(n2, n1, stride=sb), :], bi_ref[pl.ds(n2, n1, stride=sb), :]],
                               axis=0).astype(bf16) for n2 in n2s]
        res = [jnp.dot(hinv_ref[n2], bb, preferred_element_type=f32) for n2, bb in zip(n2s, bbs)]
        for n2, o in zip(n2s, res):
            c_ref[pl.ds(n2, k1n, stride=FFT_N2), :] = o
        return c
    lax.fori_loop(0, FFT_N2 // FFT_OUTER_UNROLL, inv2, 0)

    skip = skip_ref[0].astype(f32)

    def fin(i, c):
        r = pl.multiple_of(i * ROW_CHUNK, ROW_CHUNK)
        y = src_ref[pl.ds(r, ROW_CHUNK), :]
        y = gate_ref[pl.ds(r, ROW_CHUNK), :] * (c_ref[pl.ds(r, ROW_CHUNK), :] + y * skip)
        if last:
            z = z_ref[0, pl.ds(r, ROW_CHUNK), :].astype(f32)
            y = y * (z * jax.nn.sigmoid(z))
        o_ref[0, pl.ds(r, ROW_CHUNK), :] = y.astype(o_ref.dtype)
        return c
    lax.fori_loop(0, l // ROW_CHUNK, fin, 0)


def hyena_order(yin, px, hr, hi, conv_w, skip, order, first, last):
    b, l, _ = px.shape
    n = 2 * l
    n1 = n // FFT_N2
    k1n = l // FFT_N2
    sa, sb = n1 + STRIDE_PAD, FFT_N2 + STRIDE_PAD
    nh = HY_WIDTH // HY_CH
    f1, fblk, fiblk, hinv = [jnp.asarray(a, bf16) for a in _fft_consts(n, k1n, k1n)]
    cb = lambda col: col // HY_CH
    c_v, c_g, c_z = cb(C_HY_VX), cb(C_HY_VX + (order + 1) * HY_WIDTH), cb(C_HY_Z)
    once = pl.Buffered(1)
    yin_spec = (pl.BlockSpec((1, l, HY_CH), lambda h, bi: (bi, 0, c_v + h)) if first
                else pl.BlockSpec((1, l, HY_CH), lambda h, bi: (bi, 0, h)))
    return pl.pallas_call(
        functools.partial(_hy_order_kernel, l=l, first=first, last=last),
        grid=(nh, b),
        in_specs=[yin_spec,
                  pl.BlockSpec((1, l, HY_CH), lambda h, bi: (bi, 0, c_g + h)),
                  pl.BlockSpec((1, l, HY_CH), lambda h, bi: (bi, 0, c_z + h)),
                  pl.BlockSpec((3, HY_CH), lambda h, bi: (0, h)),
                  pl.BlockSpec((3, HY_CH), lambda h, bi: (0, (order + 1) * nh + h)),
                  pl.BlockSpec((1, 1, HY_CH), lambda h, bi: (order, 0, h)),
                  pl.BlockSpec((n, HY_CH), lambda h, bi: (0, order * nh + h), pipeline_mode=once),
                  pl.BlockSpec((n, HY_CH), lambda h, bi: (0, order * nh + h), pipeline_mode=once),
                  pl.BlockSpec(f1.shape, lambda h, bi: (0, 0, 0), pipeline_mode=once),
                  pl.BlockSpec(fblk.shape, lambda h, bi: (0, 0), pipeline_mode=once),
                  pl.BlockSpec(fiblk.shape, lambda h, bi: (0, 0), pipeline_mode=once),
                  pl.BlockSpec(hinv.shape, lambda h, bi: (0, 0, 0), pipeline_mode=once)],
        out_specs=pl.BlockSpec((1, l, HY_CH), lambda h, bi: (bi, 0, h)),
        out_shape=jax.ShapeDtypeStruct((b, l, HY_WIDTH), bf16 if last else f32),
        scratch_shapes=[pltpu.VMEM((l + 16, HY_CH), f32), pltpu.VMEM((l, HY_CH), f32), pltpu.VMEM((l, HY_CH), f32),
                        pltpu.VMEM((l, HY_CH), f32),
                        pltpu.VMEM((FFT_N2 * sa, HY_CH), f32), pltpu.VMEM((FFT_N2 * sa, HY_CH), f32),
                        pltpu.VMEM((n1 * sb, HY_CH), f32), pltpu.VMEM((n1 * sb, HY_CH), f32)],
        compiler_params=pltpu.CompilerParams(dimension_semantics=("arbitrary", "arbitrary"),
                                             vmem_limit_bytes=56 * 1024 * 1024),
        name=f"hyena_order{order}",
    )(yin, px, px, conv_w, conv_w, skip.reshape(HY_ORDER, 1, HY_WIDTH), hr, hi, f1, fblk, fiblk, hinv)


def hyena_latent(px, conv_w, skip, w1, b1, w2, b2, w3, b3, freq, wout):
    l = px.shape[1]
    kern = hyena_filter_circular(l, w1, b1, w2, b2, w3, b3, freq, wout)
    hr, hi = filter_spectrum(kern)
    y1 = hyena_order(px, px, hr, hi, conv_w, skip, 0, True, False)
    return hyena_order(y1, px, hr, hi, conv_w, skip, 1, False, True)


@functools.lru_cache(maxsize=None)
def _dense_dft_consts(l):
    n = 2 * l
    k = np.arange(n)
    ang = -2.0 * np.pi * np.outer(k, np.arange(n)) / n
    fwd = np.concatenate([np.cos(ang), np.sin(ang)], axis=0)
    angi = 2.0 * np.pi * np.outer(np.arange(l), k) / n
    inv = np.concatenate([np.cos(angi), -np.sin(angi)], axis=1) / n
    return fwd, inv


def _hy_ctx_kernel(p_ref, kern_ref, cw_ref, skip_ref, fwd_ref, inv_ref, o_ref, pad_ref, t_ref, *, l):
    n = 2 * l
    dot = functools.partial(jnp.dot, preferred_element_type=f32, precision=HIGHEST)
    w = HY_WIDTH

    def conv3(col):
        pad_ref[0:8, :] = jnp.zeros((8, w), f32)
        pad_ref[l + 8:l + 16, :] = jnp.zeros((8, w), f32)
        pad_ref[8:8 + l, :] = p_ref[0, :, col:col + w].astype(f32)
        cw = cw_ref[:, col - C_HY_VX:col - C_HY_VX + w].astype(f32)
        return cw[0:1] * pad_ref[7:7 + l, :] + cw[1:2] * pad_ref[8:8 + l, :] + cw[2:3] * pad_ref[9:9 + l, :]

    y = conv3(C_HY_VX)
    for o in range(HY_ORDER):
        gate = conv3(C_HY_VX + (o + 1) * w)
        hh = dot(fwd_ref[...], kern_ref[:, o * w:(o + 1) * w])
        xx = dot(fwd_ref[:, :l], y)
        xr, xi, hr, hi = xx[:n], xx[n:], hh[:n], hh[n:]
        t_ref[0:n, :] = xr * hr - xi * hi
        t_ref[n:2 * n, :] = xr * hi + xi * hr
        conv = dot(inv_ref[...], t_ref[...])
        y = gate * (conv + y * skip_ref[o:o + 1, :].astype(f32))
    z = p_ref[0, :, C_HY_Z:C_HY_Z + w].astype(f32)
    o_ref[0] = (y * (z * jax.nn.sigmoid(z))).astype(o_ref.dtype)


def hyena_ctx(pc, conv_w, skip, w1, b1, w2, b2, w3, b3, freq, wout):
    b, lc, _ = pc.shape
    n = 2 * lc
    kern = hyena_filter_circular(lc, w1, b1, w2, b2, w3, b3, freq, wout)
    fwd, inv = [jnp.asarray(a, f32) for a in _dense_dft_consts(lc)]
    full = lambda a: pl.BlockSpec(a.shape, lambda bi: (0,) * a.ndim)
    return pl.pallas_call(
        functools.partial(_hy_ctx_kernel, l=lc),
        grid=(b,),
        in_specs=[pl.BlockSpec((1, lc, MAIN_W), lambda bi: (bi, 0, 0)), full(kern), full(conv_w), full(skip),
                  full(fwd), full(inv)],
        out_specs=pl.BlockSpec((1, lc, HY_WIDTH), lambda bi: (bi, 0, 0)),
        out_shape=jax.ShapeDtypeStruct((b, lc, HY_WIDTH), bf16),
        scratch_shapes=[pltpu.VMEM((lc + 16, HY_WIDTH), f32), pltpu.VMEM((2 * n, HY_WIDTH), f32)],
        compiler_params=pltpu.CompilerParams(dimension_semantics=("arbitrary",), vmem_limit_bytes=VMEM_LIMIT),
        name="hyena_ctx",
    )(pc, kern, conv_w, skip, fwd, inv)


DN_PREP_ROWS = 256
DN_SUB_ROWS = 64
DN_HALO = 16


@functools.lru_cache(maxsize=None)
def _shift_matrix():
    s = np.zeros((DN_CONV_W * DN_SUB_ROWS, DN_SUB_ROWS + 2 * DN_HALO), np.float32)
    for j in range(DN_CONV_W):
        for m in range(DN_SUB_ROWS):
            s[j * DN_SUB_ROWS + m, m + j + DN_HALO - DN_CONV_W // 2] = 1.0
    return s


@functools.lru_cache(maxsize=None)
def _rope_tables(l):
    half = DN_HEAD_DIM // 2
    nf = half // 2
    t = np.arange(l)
    inv = (np.float32(ROPE_BASE) ** (-np.arange(nf, dtype=np.float32) / nf)).astype(np.float32)
    ang_r = ((t // GRID_W).astype(np.float32)[:, None] * inv[None, :]).astype(np.float32)
    ang_c = ((t % GRID_W).astype(np.float32)[:, None] * inv[None, :]).astype(np.float32)
    cos = np.concatenate([np.cos(ang_r), np.cos(ang_r), np.cos(ang_c), np.cos(ang_c)], axis=1)
    sin = np.concatenate([-np.sin(ang_r), np.sin(ang_r), -np.sin(ang_c), np.sin(ang_c)], axis=1)
    return cos.astype(np.float32), sin.astype(np.float32)


def _dn_prep_kernel(*refs, rope, n_tiles):
    (q_ref, qp_ref, qn_ref, k_ref, kp_ref, kn_ref, v_ref, vp_ref, vn_ref, g_ref, cw_ref, sh_ref, cos_ref, sin_ref,
     al_ref, dt_ref, qo_ref, ko_ref, vo_ref, go_ref, pad_ref, taps_ref) = refs
    i = pl.program_id(1)
    t = DN_PREP_ROWS
    w = DN_WIDTH
    lane = lax.broadcasted_iota(jnp.int32, (DN_SUB_ROWS, w), 1)
    first_half = (lane % (DN_HEAD_DIM // 2)) < (DN_HEAD_DIM // 4)
    for idx, (m_ref, p_ref, n_ref, o_ref) in enumerate(((q_ref, qp_ref, qn_ref, qo_ref), (k_ref, kp_ref, kn_ref, ko_ref),
                                                        (v_ref, vp_ref, vn_ref, vo_ref))):
        pad_ref[0:DN_HALO, :] = jnp.where(i == 0, jnp.zeros_like(p_ref[0]), p_ref[0])
        pad_ref[DN_HALO:DN_HALO + t, :] = m_ref[0]
        pad_ref[DN_HALO + t:2 * DN_HALO + t, :] = jnp.where(i == n_tiles - 1, jnp.zeros_like(n_ref[0]), n_ref[0])
        cw = cw_ref[:, idx * w:(idx + 1) * w].astype(f32)
        for s in range(t // DN_SUB_ROWS):
            taps_ref[s] = jnp.dot(sh_ref[...], pad_ref[s * DN_SUB_ROWS:(s + 1) * DN_SUB_ROWS + 2 * DN_HALO, :],
                                  preferred_element_type=f32)

        def sub(s, c):
            r = s * DN_SUB_ROWS
            y = cw[0:1] * taps_ref[s, 0:DN_SUB_ROWS, :]
            for j in range(1, DN_CONV_W):
                y = y + cw[j:j + 1] * taps_ref[s, j * DN_SUB_ROWS:(j + 1) * DN_SUB_ROWS, :]
            y = y * jax.nn.sigmoid(y)
            if idx < 2:
                if rope:
                    cos = jnp.concatenate([cos_ref[pl.ds(r, DN_SUB_ROWS), :]] * DN_HEADS, axis=1)
                    sin = jnp.concatenate([sin_ref[pl.ds(r, DN_SUB_ROWS), :]] * DN_HEADS, axis=1)
                    q4 = DN_HEAD_DIM // 4
                    swapped = jnp.where(first_half, pltpu.roll(y, w - q4, 1), pltpu.roll(y, q4, 1))
                    y = y * cos + swapped * sin
                outs = []
                for h in range(DN_HEADS):
                    yh = y[:, h * DN_HEAD_DIM:(h + 1) * DN_HEAD_DIM]
                    nrm = lax.rsqrt(jnp.sum(yh * yh, axis=-1, keepdims=True) + NORM_EPS)
                    if idx == 0:
                        nrm = nrm * (DN_HEAD_DIM ** -0.5)
                    outs.append(yh * nrm)
                y = jnp.concatenate(outs, axis=1)
            o_ref[0, pl.ds(r, DN_SUB_ROWS), :] = y.astype(o_ref.dtype)
            return c
        for s in range(t // DN_SUB_ROWS):
            sub(s, 0)

    g = g_ref[0]
    glane = lax.broadcasted_iota(jnp.int32, g.shape, 1)
    xa = g + dt_ref[...]
    softplus = jnp.maximum(xa, 0.0) + jnp.log1p(jnp.exp(-jnp.abs(xa)))
    go_ref[0] = jnp.where(glane < 2 * DN_HEADS, jax.nn.sigmoid(g), -jnp.exp(al_ref[...]) * softplus)


def dn_prep(p, gates_raw, conv_w, a_log, dt_bias, rope):
    b, l, _ = p.shape
    t = DN_PREP_ROWS
    w = DN_WIDTH
    n_tiles = l // t
    hb = t // DN_HALO
    cblk = C_DN_QKV // w
    cos, sin = [jnp.asarray(a) for a in _rope_tables(l)]
    shift = jnp.asarray(_shift_matrix(), bf16)
    pad = jnp.zeros((2 * DN_HEADS,), f32)
    rest = jnp.zeros((GATE_W - 4 * DN_HEADS,), f32)
    al = jnp.concatenate([pad, a_log.reshape(-1), rest])[None]
    dt = jnp.concatenate([pad, dt_bias.reshape(-1), rest])[None]

    def slab(j):
        return [pl.BlockSpec((1, t, w), lambda bi, i: (bi, i, cblk + j)),
                pl.BlockSpec((1, DN_HALO, w), lambda bi, i: (bi, jnp.maximum(i * hb - 1, 0), cblk + j)),
                pl.BlockSpec((1, DN_HALO, w), lambda bi, i: (bi, jnp.minimum((i + 1) * hb, l // DN_HALO - 1), cblk + j))]

    row = lambda a: pl.BlockSpec(a.shape, lambda bi, i: (0, 0))
    tab = pl.BlockSpec((t, DN_HEAD_DIM), lambda bi, i: (i, 0))
    out_blk = pl.BlockSpec((1, t, w), lambda bi, i: (bi, i, 0))
    return pl.pallas_call(
        functools.partial(_dn_prep_kernel, rope=rope, n_tiles=n_tiles),
        grid=(b, n_tiles),
        in_specs=slab(0) + slab(1) + slab(2) + [
            pl.BlockSpec((1, t, GATE_W), lambda bi, i: (bi, i, 0)), row(conv_w), row(shift), tab, tab, row(al),
            row(dt)],
        out_specs=[out_blk, out_blk, out_blk, pl.BlockSpec((1, t, GATE_W), lambda bi, i: (bi, i, 0))],
        out_shape=[jax.ShapeDtypeStruct((b, l, w), bf16)] * 3 + [jax.ShapeDtypeStruct((b, l, GATE_W), f32)],
        scratch_shapes=[pltpu.VMEM((t + 2 * DN_HALO, w), bf16),
                        pltpu.VMEM((t // DN_SUB_ROWS, DN_CONV_W * DN_SUB_ROWS, w), f32)],
        compiler_params=pltpu.CompilerParams(dimension_semantics=("arbitrary", "arbitrary"),
                                             vmem_limit_bytes=VMEM_LIMIT),
        name="dn_prep",
    )(p, p, p, p, p, p, p, p, p, gates_raw, conv_w, shift, cos, sin, al, dt)


DN_GROUP = 4
DN_GROUP_ROWS = DN_GROUP * DN_CHUNK
DN_BATCH_BLOCK = 2
DN_UNROLL = 2


def _bdot(a, b):
    return jnp.dot(a.astype(bf16), b.astype(bf16), preferred_element_type=f32)


def _dn_chunk_kernel(qc_ref, kc_ref, vc_ref, gc_ref, qx_ref, kx_ref, vx_ref, gx_ref, u_ref, wq_ref, ak_ref, eg_ref):
    g = pl.program_id(1)
    c = DN_CHUNK
    dh = DN_HEAD_DIM
    is_ctx = g == 0
    ri = lax.broadcasted_iota(jnp.int32, (c, c), 0)
    ci = lax.broadcasted_iota(jnp.int32, (c, c), 1)
    eye = (ri == ci).astype(f32)
    er = lax.broadcasted_iota(jnp.int32, (GATE_W, GATE_W), 0)
    ec = lax.broadcasted_iota(jnp.int32, (GATE_W, GATE_W), 1)
    eye_b = (er == ec).astype(bf16)
    tri_b = (ri >= ci).astype(bf16)
    masks = [((ri <= ci) if d else (ri >= ci), (ri < ci) if d else (ri > ci)) for d in range(2)]

    def sel_dot(sel, x, dims):
        hi = x.astype(bf16)
        mid = (x - hi.astype(f32)).astype(bf16)
        lo = (x - hi.astype(f32) - mid.astype(f32)).astype(bf16)
        return sum(lax.dot_general(sel, t, dims, preferred_element_type=f32) for t in (hi, mid, lo))

    def body(it, carry):
        units = []
        for jj in range(DN_UNROLL):
            j = it * DN_UNROLL + jj
            r = pl.multiple_of(j * c, c)
            pick = lambda a_ref, b_ref: jnp.where(is_ctx, a_ref[0, pl.ds(r, c), :], b_ref[0, pl.ds(r, c), :])
            q, k, v, gates = pick(qc_ref, qx_ref), pick(kc_ref, kx_ref), pick(vc_ref, vx_ref), pick(gc_ref, gx_ref)
            gtot = jnp.sum(gates, axis=0, keepdims=True)
            gfwd = sel_dot(tri_b, gates, (((1,), (0,)), ((), ())))
            gcs = (gfwd, gtot - gfwd + gates)
            g_t = sel_dot(eye_b, jnp.concatenate([gfwd, gates], axis=0), NT_DIMS)
            gcs_t = (g_t[:, :c], jnp.sum(g_t[:, c:], axis=1, keepdims=True) - g_t[:, :c] + g_t[:, c:])
            qks = [lax.dot_general(q[:, h * dh:(h + 1) * dh], k[:, h * dh:(h + 1) * dh], NT_DIMS,
                                   preferred_element_type=f32) for h in range(DN_HEADS)]
            kks = [lax.dot_general(k[:, h * dh:(h + 1) * dh], k[:, h * dh:(h + 1) * dh], NT_DIMS,
                                   preferred_element_type=f32) for h in range(DN_HEADS)]
            for d in range(2):
                incl, strict = masks[d]
                egs = []
                for h in range(DN_HEADS):
                    hs = slice(h * dh, (h + 1) * dh)
                    col = d * DN_HEADS + h
                    gl = 2 * DN_HEADS + col
                    kh, vh, qh = k[:, hs].astype(f32), v[:, hs].astype(f32), q[:, hs].astype(f32)
                    beta = gates[:, col:col + 1]
                    gcol = gcs[d][:, gl:gl + 1]
                    glast = gtot[:, gl:gl + 1]
                    decay = jnp.where(incl, jnp.exp(jnp.minimum(gcol - gcs_t[d][gl:gl + 1, :], 0.0)), 0.0)
                    eg = jnp.exp(gcol)
                    wq_ref[d, 0, j, h, c:2 * c, :] = (qh * eg).astype(bf16)
                    ak_ref[d, 0, j, h, 0:c, :] = jnp.where(incl, qks[h] * decay, 0.0).astype(bf16)
                    ak_ref[d, 0, j, h, c:c + dh, :] = jnp.transpose(kh * jnp.exp(glast - gcol)).astype(bf16)
                    egs.append(jnp.broadcast_to(jnp.exp(glast), (1, GATE_W)))
                    kb = kh * beta
                    units.append(dict(j=j, d=d, h=h, n=-jnp.where(strict, beta * kks[h] * decay, 0.0),
                                      rhs=jnp.concatenate([vh * beta, kb * eg], axis=1).astype(bf16)))
                eg_ref[d, 0, j] = jnp.concatenate(egs + [jnp.zeros((8 - DN_HEADS, GATE_W), f32)], axis=0)
        for un in units:
            un["t"] = eye + un["n"]
            un["p"] = _bdot(un["n"], un["n"])
        for _ in range(int(math.log2(c)) - 2):
            for un in units:
                both = _bdot(jnp.concatenate([un["p"], un["t"]], axis=0), un["p"])
                un["p"], un["t"] = both[:c], un["t"] + both[c:]
        for un in units:
            un["t"] = un["t"] + _bdot(un["t"], un["p"])
        for un in units:
            sol = _bdot(un["t"], un["rhs"])
            u_ref[un["d"], 0, un["j"], un["h"]] = sol[:, :dh].astype(bf16)
            wq_ref[un["d"], 0, un["j"], un["h"], 0:c, :] = sol[:, dh:].astype(bf16)
        return carry
    lax.fori_loop(0, DN_GROUP // DN_UNROLL, body, 0)


def dn_chunks(qc, kc, vc, gc, qx, kx, vx, gx):
    b, lc, w = qc.shape
    l = qx.shape[1]
    assert lc == DN_GROUP_ROWS and l % DN_GROUP_ROWS == 0
    ng = 1 + l // DN_GROUP_ROWS
    nch = ng * DN_GROUP
    c, dh, nhd = DN_CHUNK, DN_HEAD_DIM, DN_HEADS
    cmap = lambda bi, g: (bi, 0, 0)
    xmap = lambda bi, g: (bi, jnp.maximum(g - 1, 0), 0)
    blk = lambda width, m: pl.BlockSpec((1, DN_GROUP_ROWS, width), m)
    omap = lambda bi, g: (0, bi, g, 0, 0, 0)
    return pl.pallas_call(
        _dn_chunk_kernel,
        grid=(b, ng),
        in_specs=[blk(w, cmap), blk(w, cmap), blk(w, cmap), blk(GATE_W, cmap),
                  blk(w, xmap), blk(w, xmap), blk(w, xmap), blk(GATE_W, xmap)],
        out_specs=[pl.BlockSpec((2, 1, DN_GROUP, nhd, c, dh), omap),
                   pl.BlockSpec((2, 1, DN_GROUP, nhd, 2 * c, dh), omap),
                   pl.BlockSpec((2, 1, DN_GROUP, nhd, c + dh, c), omap),
                   pl.BlockSpec((2, 1, DN_GROUP, 8, GATE_W), lambda bi, g: (0, bi, g, 0, 0))],
        out_shape=[jax.ShapeDtypeStruct((2, b, nch, nhd, c, dh), bf16),
                   jax.ShapeDtypeStruct((2, b, nch, nhd, 2 * c, dh), bf16),
                   jax.ShapeDtypeStruct((2, b, nch, nhd, c + dh, c), bf16),
                   jax.ShapeDtypeStruct((2, b, nch, 8, GATE_W), f32)],
        compiler_params=pltpu.CompilerParams(dimension_semantics=("arbitrary", "arbitrary"),
                                             vmem_limit_bytes=VMEM_LIMIT),
        name="dn_chunks",
    )(qc, kc, vc, gc, qx, kx, vx, gx)


def _bwd_group(s, ng):
    return jnp.where(s == 0, 0, ng - s)


def _dn_serial_kernel(uf_ref, wqf_ref, akf_ref, egf_ref, ub_ref, wqb_ref, akb_ref, egb_ref, of_ref, ob_ref, s_ref):
    c = DN_CHUNK
    dh = DN_HEAD_DIM

    @pl.when(pl.program_id(1) == 0)
    def _():
        s_ref[...] = jnp.zeros(s_ref.shape, f32)

    dirs = ((uf_ref, wqf_ref, akf_ref, egf_ref, of_ref), (ub_ref, wqb_ref, akb_ref, egb_ref, ob_ref))
    units = [(e, d, h) for e in range(DN_BATCH_BLOCK) for d in range(2) for h in range(DN_HEADS)]
    states = {un: s_ref[un] for un in units}
    for jj in range(DN_GROUP):
        js = (jj, DN_GROUP - 1 - jj)
        r1 = {(e, d, h): jnp.dot(dirs[d][1][0, e, js[d], h], states[e, d, h].astype(bf16), preferred_element_type=f32)
              for e, d, h in units}
        r2 = {}
        for e, d, h in units:
            v_new = dirs[d][0][0, e, js[d], h].astype(f32) - r1[e, d, h][:c]
            r2[e, d, h] = jnp.dot(dirs[d][2][0, e, js[d], h], v_new.astype(bf16), preferred_element_type=f32)
        for e, d, h in units:
            j = js[d]
            dirs[d][4][e, j * c:(j + 1) * c, h * dh:(h + 1) * dh] = (r1[e, d, h][c:] + r2[e, d, h][:c]).astype(bf16)
            states[e, d, h] = states[e, d, h] * dirs[d][3][0, e, j, h:h + 1, :] + r2[e, d, h][c:]
    for un in units:
        s_ref[un] = states[un]


def dn_serial(u, wq, ak, eg):
    _, b, nch, nhd, c, dh = u.shape
    ng = nch // DN_GROUP
    w = nhd * dh
    fmap6 = lambda bi, s: (0, bi, s, 0, 0, 0)
    bmap6 = lambda bi, s: (1, bi, _bwd_group(s, ng), 0, 0, 0)
    fmap5 = lambda bi, s: (0, bi, s, 0, 0)
    bmap5 = lambda bi, s: (1, bi, _bwd_group(s, ng), 0, 0)
    bb = DN_BATCH_BLOCK
    assert b % bb == 0
    blk6 = lambda a, m: pl.BlockSpec((1, bb, DN_GROUP) + a.shape[3:], m)
    egblk = lambda m: pl.BlockSpec((1, bb, DN_GROUP, 8, GATE_W), m)
    return pl.pallas_call(
        _dn_serial_kernel,
        grid=(b // bb, ng),
        in_specs=[blk6(u, fmap6), blk6(wq, fmap6), blk6(ak, fmap6), egblk(fmap5),
                  blk6(u, bmap6), blk6(wq, bmap6), blk6(ak, bmap6), egblk(bmap5)],
        out_specs=[pl.BlockSpec((bb, DN_GROUP_ROWS, w), lambda bi, s: (bi, s, 0)),
                   pl.BlockSpec((bb, DN_GROUP_ROWS, w), lambda bi, s: (bi, _bwd_group(s, ng), 0))],
        out_shape=[jax.ShapeDtypeStruct((b, nch * c, w), bf16)] * 2,
        scratch_shapes=[pltpu.VMEM((bb, 2, nhd, dh, dh), f32)],
        compiler_params=pltpu.CompilerParams(dimension_semantics=("arbitrary", "arbitrary"),
                                             vmem_limit_bytes=VMEM_LIMIT),
        name="dn_serial",
    )(u, wq, ak, eg, u, wq, ak, eg)


def _dn_out_kernel(of_ref, ob_ref, z_ref, nw_ref, y_ref):
    o = of_ref[0].astype(f32) + ob_ref[0].astype(f32)
    z = z_ref[0].astype(f32)
    nw = nw_ref[...].astype(f32)
    outs = []
    for h in range(DN_HEADS):
        oh = o[:, h * DN_HEAD_DIM:(h + 1) * DN_HEAD_DIM]
        outs.append(oh * lax.rsqrt(jnp.mean(oh * oh, axis=-1, keepdims=True) + NORM_EPS) * nw)
    y_ref[0] = (jnp.concatenate(outs, axis=1) * (z * jax.nn.sigmoid(z))).astype(y_ref.dtype)


def dn_output(o_f, o_b, p, norm_w, first_block):
    b, l, _ = p.shape
    w = DN_WIDTH
    tm = DN_GROUP_ROWS
    omap = lambda bi, i: (bi, first_block + i, 0)
    return pl.pallas_call(
        _dn_out_kernel,
        grid=(b, l // tm),
        in_specs=[pl.BlockSpec((1, tm, w), omap), pl.BlockSpec((1, tm, w), omap),
                  pl.BlockSpec((1, tm, w), lambda bi, i: (bi, i, C_DN_Z // w)),
                  pl.BlockSpec((1, DN_HEAD_DIM), lambda bi, i: (0, 0))],
        out_specs=pl.BlockSpec((1, tm, w), lambda bi, i: (bi, i, 0)),
        out_shape=jax.ShapeDtypeStruct((b, l, w), bf16),
        compiler_params=pltpu.CompilerParams(dimension_semantics=("arbitrary", "arbitrary"),
                                             vmem_limit_bytes=VMEM_LIMIT),
        name="dn_output",
    )(o_f, o_b, p, norm_w.reshape(1, DN_HEAD_DIM))


def _pack_w_in(w_in):
    depth, d, _ = w_in.shape
    main = jnp.concatenate([w_in[:, :, :3072], w_in[:, :, 3088:4112]], axis=-1).astype(bf16)
    gate = jnp.concatenate([w_in[:, :, 3072:3088], jnp.zeros((depth, d, GATE_W - 16), w_in.dtype)],
                           axis=-1).astype(bf16)
    return main, gate


def kernel(x, c, ctx, c_ctx, w_ada, b_ada, g_pre, g_post, w_in, w_out, na_rpb, dn_conv, dn_a_log, dn_dt_bias,
           dn_norm, hy_conv, hy_w1, hy_b1, hy_w2, hy_b2, hy_w3, hy_b3, hy_freq, hy_wout, hy_skip):
    bsz, l, d = x.shape
    lc = ctx.shape[1]
    depth = w_in.shape[0]
    cond = jnp.concatenate([c, c_ctx[None], jnp.zeros((7, d), f32)], axis=0)
    mod = modulation_all(cond, w_ada, b_ada)
    w_main, w_gate = _pack_w_in(w_in)
    w_out_b = w_out.astype(bf16)

    for i in range(depth):
        last = i == depth - 1
        shift_x, scale_x, gate_x = [mod[i, :bsz, j * d:(j + 1) * d][:, None] for j in range(3)]
        shift_c, scale_c, gate_c = [mod[i, bsz:bsz + 1, j * d:(j + 1) * d][:, None] for j in range(3)]
        px, gx = in_projection(x, g_pre[i], scale_x, shift_x, w_main[i], w_gate[i], 512)
        pc, gc = in_projection(ctx, g_pre[i], scale_c, shift_c, w_main[i], w_gate[i], 256)

        hy_args = (hy_conv[i], hy_skip[i], hy_w1[i], hy_b1[i], hy_w2[i], hy_b2[i], hy_w3[i], hy_b3[i], hy_freq[i],
                   hy_wout[i])
        dn_args = (dn_conv[i], dn_a_log[i], dn_dt_bias[i])

        out_a_x = na_attention(px, pc, na_rpb[i])

        dn_c = dn_prep(pc, gc, *dn_args, False)
        dn_x = dn_prep(px, gx, *dn_args, True)
        do_f, do_b = dn_serial(*dn_chunks(*dn_c, *dn_x))
        out_b_x = dn_output(do_f, do_b, px, dn_norm[i], lc // DN_GROUP_ROWS)

        out_c_x = hyena_latent(px, *hy_args)

        new_x = out_projection(out_a_x, out_b_x, out_c_x, w_out_b[i], g_post[i], gate_x, x, 512)

        if not last:
            out_a_c = ctx_attention(pc)
            out_b_c = dn_output(do_f, do_b, pc, dn_norm[i], 0)
            out_c_c = hyena_ctx(pc, *hy_args)
            ctx = out_projection(out_a_c, out_b_c, out_c_c, w_out_b[i], g_post[i], gate_c, ctx, lc)
        x = new_x
    return x
```
